```python
import jax, jax.numpy as jnp
from jax import lax
import numpy as np

D_MODEL = 4096
BATCH = 2
SEQ = 8192
DEPTH = 1

CHUNK = 64
QUERY_BLOCK = 128
LN_EPS = 1e-5
FOX_HEAD_DIM = 128
FOX_WIDTH = D_MODEL // 2
FOX_HEADS = FOX_WIDTH // FOX_HEAD_DIM
ML_HEADS = 4
ML_QK_DIM = 256
ML_V_DIM = 512
ML_QK_WIDTH = ML_HEADS * ML_QK_DIM
ML_V_WIDTH = ML_HEADS * ML_V_DIM
CONV_WIDTH = 4
N_GROUPS = 8
EXPERTS_PER_GROUP = 8
N_EXPERTS = N_GROUPS * EXPERTS_PER_GROUP
TOP_K = 2
D_EXPERT = 512
EXPERT_BLOCK = 128
DEEPNORM_ALPHA = (2 * DEPTH) ** 0.25
DEEPNORM_BETA = (8 * DEPTH) ** -0.25
IN_SPLIT_WIDTHS = (FOX_WIDTH, FOX_WIDTH, FOX_WIDTH, FOX_HEADS,
                   ML_QK_WIDTH, ML_QK_WIDTH, ML_V_WIDTH, ML_HEADS, ML_HEADS, ML_V_WIDTH,
                   D_MODEL, D_MODEL)
IN_WIDTH = sum(IN_SPLIT_WIDTHS)
IN_SPLIT_POINTS = tuple(int(p) for p in np.cumsum(IN_SPLIT_WIDTHS)[:-1])

kernel_name = "fox_mlstm_hier_moe_deepnorm"


def layer_norm(x, g, b):
    xf = x.astype(jnp.float32)
    mu = jnp.mean(xf, axis=-1, keepdims=True)
    var = jnp.mean(jnp.square(xf - mu), axis=-1, keepdims=True)
    return ((xf - mu) * lax.rsqrt(var + LN_EPS) * g + b).astype(x.dtype)


def causal_depthwise_conv(x, w, b):
    k_w = w.shape[0]
    s = x.shape[1]
    xp = jnp.pad(x, ((0, 0), (k_w - 1, 0), (0, 0)))
    return sum(w[j] * xp[:, j:j + s] for j in range(k_w)) + b


def forgetting_attention(q, k, v, log_f):
    b_, s, h, dh = q.shape
    c = jnp.cumsum(log_f, axis=1).transpose(0, 2, 1)
    n_blk = s // QUERY_BLOCK
    qb = q.reshape(b_, n_blk, QUERY_BLOCK, h, dh).transpose(1, 0, 2, 3, 4)
    cb = c.reshape(b_, h, n_blk, QUERY_BLOCK).transpose(2, 0, 1, 3)
    starts = jnp.arange(n_blk, dtype=jnp.int32) * QUERY_BLOCK
    k_pos = jnp.arange(s, dtype=jnp.int32)
    scale = FOX_HEAD_DIM ** -0.5

    def block(args):
        q_blk, c_blk, start = args
        logits = jnp.einsum('bqhd,bkhd->bhqk', q_blk, k,
                            preferred_element_type=jnp.float32) * scale
        logits = logits + (c_blk[..., :, None] - c[..., None, :])
        q_pos = start + jnp.arange(QUERY_BLOCK, dtype=jnp.int32)
        logits = jnp.where(k_pos[None, :] <= q_pos[:, None], logits, -jnp.inf)
        p = jax.nn.softmax(logits, axis=-1)
        return jnp.einsum('bhqk,bkhd->bqhd', p.astype(v.dtype), v)

    out = lax.map(block, (qb, cb, starts))
    return out.transpose(1, 0, 2, 3, 4).reshape(b_, s, h * dh)


def mlstm_chunkwise(q, k, v, i_pre, log_f):
    b_, s, h, dk = q.shape
    dv = v.shape[-1]
    nc = s // CHUNK

    def to_chunks(t):
        t = t.reshape(b_, nc, CHUNK, h, *t.shape[3:])
        return jnp.moveaxis(jnp.moveaxis(t, 1, 0), 3, 2)

    f32 = jnp.float32
    xs = (to_chunks(q.astype(f32)), to_chunks(k.astype(f32)), to_chunks(v.astype(f32)),
          to_chunks(i_pre), to_chunks(log_f))
    tri = jnp.tril(jnp.ones((CHUNK, CHUNK), dtype=bool))

    def step(carry, chunk):
        c_mem, n_mem, m_prev = carry
        qc, kc, vc, ic, fc = chunk
        bcum = jnp.cumsum(fc, axis=-1)
        d = bcum[..., :, None] - bcum[..., None, :] + ic[..., None, :]
        d = jnp.where(tri, d, -jnp.inf)
        inter = bcum + m_prev[..., None]
        m = jnp.maximum(inter, jnp.max(d, axis=-1))
        w_intra = jnp.exp(d - m[..., None])
        w_inter = jnp.exp(inter - m)
        sc = jnp.einsum('bhtd,bhsd->bhts', qc, kc) * w_intra
        num = (jnp.einsum('bhts,bhsv->bhtv', sc, vc)
               + w_inter[..., None] * jnp.einsum('bhtd,bhdv->bhtv', qc, c_mem))
        den = jnp.sum(sc, axis=-1) + w_inter * jnp.einsum('bhtd,bhd->bht', qc, n_mem)
        h_out = num / jnp.maximum(jnp.abs(den), jnp.exp(-m))[..., None]
        b_last = bcum[..., -1]
        g = b_last[..., None] - bcum + ic
        m_new = jnp.maximum(b_last + m_prev, jnp.max(g, axis=-1))
        w_k = jnp.exp(g - m_new[..., None])
        decay = jnp.exp(b_last + m_prev - m_new)
        c_new = decay[..., None, None] * c_mem + jnp.einsum('bhs,bhsd,bhsv->bhdv', w_k, kc, vc)
        n_new = decay[..., None] * n_mem + jnp.einsum('bhs,bhsd->bhd', w_k, kc)
        return (c_new, n_new, m_new), h_out

    init = (jnp.zeros((b_, h, dk, dv), f32), jnp.zeros((b_, h, dk), f32), jnp.zeros((b_, h), f32))
    _, hs = lax.scan(step, init, xs)
    return hs.transpose(1, 0, 3, 2, 4).reshape(b_, s, h, dv)


def hybrid_mixer(h, w_in, b_fox_f, b_ml_i, b_ml_f, conv_w, conv_b, ml_norm_g,
                 w_proj_fox, w_proj_ml, w_out):
    b_, s, _ = h.shape
    f32 = jnp.float32
    proj = jnp.einsum('bsd,de->bse', h, w_in)
    fq, fk, fv, ff, mq, mk, mv, mi, mf, mo, ga, gb = jnp.split(proj, IN_SPLIT_POINTS, axis=-1)

    log_f_fox = jax.nn.log_sigmoid((ff + b_fox_f).astype(f32))
    fox_shape = (b_, s, FOX_HEADS, FOX_HEAD_DIM)
    y_fox = forgetting_attention(fq.reshape(fox_shape), fk.reshape(fox_shape),
                                 fv.reshape(fox_shape), log_f_fox)

    qk = jax.nn.silu(causal_depthwise_conv(jnp.concatenate([mq, mk], axis=-1), conv_w, conv_b))
    mq, mk = jnp.split(qk, 2, axis=-1)
    qk_shape = (b_, s, ML_HEADS, ML_QK_DIM)
    h_ml = mlstm_chunkwise(mq.reshape(qk_shape), mk.reshape(qk_shape) * (ML_QK_DIM ** -0.5),
                           mv.reshape(b_, s, ML_HEADS, ML_V_DIM),
                           (mi + b_ml_i).astype(f32),
                           jax.nn.log_sigmoid((mf + b_ml_f).astype(f32)))
    mu = jnp.mean(h_ml, axis=-1, keepdims=True)
    var = jnp.mean(jnp.square(h_ml - mu), axis=-1, keepdims=True)
    h_ml = ((h_ml - mu) * lax.rsqrt(var + LN_EPS)).reshape(b_, s, ML_V_WIDTH)
    y_ml = (h_ml * ml_norm_g * jax.nn.sigmoid(mo.astype(f32))).astype(h.dtype)

    merged = (jax.nn.sigmoid(ga) * jnp.einsum('bsc,cd->bsd', y_fox, w_proj_fox)
              + jax.nn.sigmoid(gb) * jnp.einsum('bsc,cd->bsd', y_ml, w_proj_ml))
    return jnp.einsum('bsd,de->bse', merged, w_out)


def hierarchical_moe(h, w_group, b_group, w_router, b_router, w_gate, w_up, w_down):
    b_, s, d = h.shape
    xt = h.reshape(-1, d)
    n_tok = xt.shape[0]
    f32 = jnp.float32
    group_logits = (xt @ w_group + b_group).astype(f32)
    p_group = jax.nn.softmax(group_logits, axis=-1)
    g_sel = jnp.argmax(group_logits, axis=-1).astype(jnp.int32)
    p_g_sel = jnp.take_along_axis(p_group, g_sel[:, None], axis=-1)
    e_logits = (xt @ w_router + b_router).astype(f32).reshape(n_tok, N_GROUPS, EXPERTS_PER_GROUP)
    e_logits = jnp.take_along_axis(e_logits, g_sel[:, None, None], axis=1)[:, 0]
    top_p, top_i = lax.top_k(jax.nn.softmax(e_logits, axis=-1), TOP_K)
    gate = p_g_sel * top_p / jnp.sum(top_p, axis=-1, keepdims=True)
    expert_id = g_sel[:, None] * EXPERTS_PER_GROUP + top_i.astype(jnp.int32)

    n_assign = n_tok * TOP_K
    flat_e = expert_id.reshape(-1)
    flat_w = gate.reshape(-1)
    flat_tok = jnp.repeat(jnp.arange(n_tok, dtype=jnp.int32), TOP_K)
    order = jnp.argsort(flat_e)
    se, stok, sw = flat_e[order], flat_tok[order], flat_w[order]
    counts = jnp.bincount(flat_e, length=N_EXPERTS).astype(jnp.int32)
    padded = (counts + EXPERT_BLOCK - 1) // EXPERT_BLOCK * EXPERT_BLOCK
    starts = jnp.cumsum(counts) - counts
    pad_ends = jnp.cumsum(padded)
    pad_starts = pad_ends - padded
    dest = pad_starts[se] + (jnp.arange(n_assign, dtype=jnp.int32) - starts[se])
    n_blocks = (n_assign + N_EXPERTS * (EXPERT_BLOCK - 1) + EXPERT_BLOCK - 1) // EXPERT_BLOCK
    n_slots = n_blocks * EXPERT_BLOCK
    slot_tok = jnp.zeros((n_slots,), jnp.int32).at[dest].set(stok)
    slot_w = jnp.zeros((n_slots,), f32).at[dest].set(sw)
    block_e = jnp.minimum(
        jnp.searchsorted(pad_ends, jnp.arange(n_blocks, dtype=jnp.int32) * EXPERT_BLOCK, side='right'),
        N_EXPERTS - 1)

    def block(y, args):
        tok, wt, e = args
        xb = xt[tok]
        a = jax.nn.silu(xb @ w_gate[e]) * (xb @ w_up[e])
        out = (a @ w_down[e]) * wt[:, None].astype(xt.dtype)
        return y.at[tok].add(out), None

    y, _ = lax.scan(block, jnp.zeros_like(xt),
                    (slot_tok.reshape(n_blocks, EXPERT_BLOCK),
                     slot_w.reshape(n_blocks, EXPERT_BLOCK), block_e))
    return y.reshape(b_, s, d)


def setup_inputs(seed: int = 0) -> dict:
    key = jax.random.key(seed)
    ks = iter(jax.random.split(key, 48))
    nrm = lambda shape, scale: jax.random.normal(next(ks), shape, jnp.float32) * scale
    dsc = D_MODEL ** -0.5
    beta = DEEPNORM_BETA
    col_scales = (dsc, dsc, dsc * beta, 0.1 * dsc,
                  dsc, dsc, dsc * beta, 0.1 * dsc, 0.1 * dsc, dsc,
                  dsc, dsc)
    w_in = jnp.concatenate([nrm((DEPTH, D_MODEL, w), sc)
                            for w, sc in zip(IN_SPLIT_WIDTHS, col_scales)], axis=-1)
    return {
        "x": nrm((BATCH, SEQ, D_MODEL), 1.0),
        "ln_in_g": 1.0 + nrm((D_MODEL,), 0.02),
        "ln_in_b": nrm((D_MODEL,), 0.02),
        "w_in": w_in,
        "b_fox_f": jax.random.uniform(next(ks), (DEPTH, FOX_HEADS), jnp.float32, 3.0, 6.0),
        "b_ml_i": nrm((DEPTH, ML_HEADS), 0.5) - 1.0,
        "b_ml_f": jax.random.uniform(next(ks), (DEPTH, ML_HEADS), jnp.float32, 3.0, 6.0),
        "conv_w": nrm((DEPTH, CONV_WIDTH, 2 * ML_QK_WIDTH), CONV_WIDTH ** -0.5),
        "conv_b": nrm((DEPTH, 2 * ML_QK_WIDTH), 0.02),
        "ml_norm_g": 1.0 + nrm((DEPTH, ML_V_WIDTH), 0.02),
        "w_proj_fox": nrm((DEPTH, FOX_WIDTH, D_MODEL), FOX_WIDTH ** -0.5 * beta),
        "w_proj_ml": nrm((DEPTH, ML_V_WIDTH, D_MODEL), ML_V_WIDTH ** -0.5 * beta),
        "w_out": nrm((DEPTH, D_MODEL, D_MODEL), dsc * beta),
        "ln_mix_g": 1.0 + nrm((DEPTH, D_MODEL), 0.02),
        "ln_mix_b": nrm((DEPTH, D_MODEL), 0.02),
        "w_group": nrm((DEPTH, D_MODEL, N_GROUPS), dsc),
        "b_group": nrm((DEPTH, N_GROUPS), 0.01),
        "w_router": nrm((DEPTH, D_MODEL, N_EXPERTS), dsc),
        "b_router": nrm((DEPTH, N_EXPERTS), 0.01),
        "w_gate": nrm((DEPTH, N_EXPERTS, D_MODEL, D_EXPERT), dsc),
        "w_up": nrm((DEPTH, N_EXPERTS, D_MODEL, D_EXPERT), dsc * beta),
        "w_down": nrm((DEPTH, N_EXPERTS, D_EXPERT, D_MODEL), D_EXPERT ** -0.5 * beta),
        "ln_moe_g": 1.0 + nrm((DEPTH, D_MODEL), 0.02),
        "ln_moe_b": nrm((DEPTH, D_MODEL), 0.02),
    }


def reference(x, ln_in_g, ln_in_b, w_in, b_fox_f, b_ml_i, b_ml_f, conv_w, conv_b, ml_norm_g,
              w_proj_fox, w_proj_ml, w_out, ln_mix_g, ln_mix_b, w_group, b_group, w_router,
              b_router, w_gate, w_up, w_down, ln_moe_g, ln_moe_b):
    h = layer_norm(x, ln_in_g, ln_in_b)
    for l in range(DEPTH):
        mix = hybrid_mixer(h, w_in[l], b_fox_f[l], b_ml_i[l], b_ml_f[l], conv_w[l], conv_b[l],
                           ml_norm_g[l], w_proj_fox[l], w_proj_ml[l], w_out[l])
        h = layer_norm(DEEPNORM_ALPHA * h + mix, ln_mix_g[l], ln_mix_b[l])
        ffn = hierarchical_moe(h, w_group[l], b_group[l], w_router[l], b_router[l],
                               w_gate[l], w_up[l], w_down[l])
        h = layer_norm(DEEPNORM_ALPHA * h + ffn, ln_moe_g[l], ln_moe_b[l])
    return h
```

```python
import functools

import jax
import jax.numpy as jnp
from jax import lax
from jax.experimental import pallas as pl
from jax.experimental.pallas import tpu as pltpu

F32 = jnp.float32
BF16 = jnp.bfloat16

LN_EPS = 1e-5
FOX_HEAD_DIM = 128
ML_HEADS = 4
ML_QK_DIM = 256
ML_V_DIM = 512
TOP_K = 2
GATE_LANES = 128

V7X_VMEM_BYTES = 64 * 1024 * 1024
VMEM_LIMIT_BYTES = V7X_VMEM_BYTES - 8 * 1024 * 1024


def _params(*sem):
    return pltpu.CompilerParams(dimension_semantics=sem, vmem_limit_bytes=VMEM_LIMIT_BYTES)


def _tile(dim, pref):
    t = min(dim, pref)
    while dim % t:
        t //= 2
    return t


def _layer_norm(x, g, b):
    mu = jnp.mean(x, axis=-1, keepdims=True)
    xc = x - mu
    var = jnp.mean(xc * xc, axis=-1, keepdims=True)
    return xc * lax.rsqrt(var + LN_EPS) * g + b


def _log_sigmoid(x):
    return jnp.minimum(x, 0.0) - jnp.log1p(jnp.exp(-jnp.abs(x)))


def _ln_in_kernel(x_ref, g_ref, b_ref, o32_ref, o16_ref):
    y = _layer_norm(x_ref[...], g_ref[...], b_ref[...])
    o32_ref[...] = y
    o16_ref[...] = y.astype(BF16)


def _ln_in(x, g, b):
    n, d = x.shape
    tr = _tile(n, 256)
    row = pl.BlockSpec((tr, d), lambda i: (i, 0))
    vec = pl.BlockSpec((1, d), lambda i: (0, 0))
    return pl.pallas_call(
        _ln_in_kernel,
        grid=(n // tr,),
        in_specs=[row, vec, vec],
        out_specs=[row, row],
        out_shape=[jax.ShapeDtypeStruct((n, d), F32), jax.ShapeDtypeStruct((n, d), BF16)],
        compiler_params=_params("parallel"),
        name="ln_in",
    )(x, g.reshape(1, d), b.reshape(1, d))


def _ln_res_kernel(h_ref, y_ref, g_ref, b_ref, o_ref, *, alpha):
    o_ref[...] = _layer_norm(alpha * h_ref[...] + y_ref[...], g_ref[...], b_ref[...])


def _ln_res(h, y, g, b, alpha):
    n, d = h.shape
    tr = _tile(n, 256)
    row = pl.BlockSpec((tr, d), lambda i: (i, 0))
    vec = pl.BlockSpec((1, d), lambda i: (0, 0))
    return pl.pallas_call(
        functools.partial(_ln_res_kernel, alpha=alpha),
        grid=(n // tr,),
        in_specs=[row, row, vec, vec],
        out_specs=row,
        out_shape=jax.ShapeDtypeStruct((n, d), F32),
        compiler_params=_params("parallel"),
        name="ln_mix",
    )(h, y, g.reshape(1, d), b.reshape(1, d))


def _mm_kernel(x_ref, w_ref, o_ref):
    o_ref[...] = jnp.dot(x_ref[...], w_ref[...], preferred_element_type=F32).astype(o_ref.dtype)


def _matmul(x, w, out_dtype, name, bm=1024, bn=1024):
    m, k = x.shape
    _, n = w.shape
    bm, bn = _tile(m, bm), _tile(n, bn)
    return pl.pallas_call(
        _mm_kernel,
        grid=(m // bm, n // bn),
        in_specs=[pl.BlockSpec((bm, k), lambda i, j: (i, 0)),
                  pl.BlockSpec((k, bn), lambda i, j: (0, j))],
        out_specs=pl.BlockSpec((bm, bn), lambda i, j: (i, j)),
        out_shape=jax.ShapeDtypeStruct((m, n), out_dtype),
        compiler_params=_params("parallel", "parallel"),
        name=name,
    )(x, w)


def _gates_kernel(g_ref, bias_ref, o_ref, carry_ref, *, n_fox):
    @pl.when(pl.program_id(1) == 0)
    def _():
        carry_ref[...] = jnp.zeros_like(carry_ref)

    ts = g_ref.shape[0]
    x = g_ref[...] + bias_ref[...]
    lane = lax.broadcasted_iota(jnp.int32, x.shape, 1)
    is_input_gate = (lane >= n_fox) & (lane < n_fox + ML_HEADS)
    val = jnp.where(is_input_gate, 0.0, _log_sigmoid(x))
    tril = (lax.broadcasted_iota(jnp.int32, (ts, ts), 0)
            >= lax.broadcasted_iota(jnp.int32, (ts, ts), 1)).astype(BF16)
    hi = val.astype(BF16)
    rem = val - hi.astype(F32)
    mid = rem.astype(BF16)
    lo = (rem - mid.astype(F32)).astype(BF16)
    cs = (jnp.dot(tril, hi, preferred_element_type=F32)
          + jnp.dot(tril, mid, preferred_element_type=F32)
          + jnp.dot(tril, lo, preferred_element_type=F32))
    total = cs + jnp.where(lane < n_fox, carry_ref[...], 0.0)
    carry_ref[...] = total[ts - 1:ts, :]
    o_ref[...] = jnp.where(is_input_gate, x, total)


def _gates(g, bias, batch, ts, n_fox):
    n, lanes = g.shape
    tiles = n // batch // ts
    blk = pl.BlockSpec((ts, lanes), lambda b, j: (b * tiles + j, 0))
    return pl.pallas_call(
        functools.partial(_gates_kernel, n_fox=n_fox),
        grid=(batch, tiles),
        in_specs=[blk, pl.BlockSpec((1, lanes), lambda b, j: (0, 0))],
        out_specs=blk,
        out_shape=jax.ShapeDtypeStruct((n, lanes), F32),
        scratch_shapes=[pltpu.VMEM((1, lanes), F32)],
        compiler_params=_params("parallel", "arbitrary"),
        name="gates",
    )(g, bias)


def _conv_kernel(x_ref, prev_ref, w_ref, b_ref, o_ref, *, tiles_per_seq, k_blocks_from, k_scale):
    ts = x_ref.shape[0]
    kw = w_ref.shape[0]
    halo = prev_ref.shape[0]
    first = pl.program_id(0) % tiles_per_seq == 0
    prev = jnp.where(first, 0.0, prev_ref[...])
    ext = jnp.concatenate([prev, x_ref[...]], axis=0)
    w = w_ref[...]
    acc = None
    for j in range(kw):
        off = halo - (kw - 1) + j
        term = w[j:j + 1, :] * ext[off:off + ts, :]
        acc = term if acc is None else acc + term
    acc = acc + b_ref[...]
    y = acc * jax.nn.sigmoid(acc)
    scale = jnp.where(pl.program_id(1) >= k_blocks_from, k_scale, 1.0)
    o_ref[...] = y * scale


def _conv_silu(proj, conv_w, conv_b, seq, width):
    n = proj.shape[0]
    kw = conv_w.shape[0]
    ts, tc, halo = _tile(seq, 512), _tile(width // 2, 512), 8
    assert kw - 1 <= halo
    tiles_per_seq = seq // ts
    return pl.pallas_call(
        functools.partial(_conv_kernel, tiles_per_seq=tiles_per_seq,
                          k_blocks_from=(width // 2) // tc, k_scale=ML_QK_DIM ** -0.5),
        grid=(n // ts, width // tc),
        in_specs=[pl.BlockSpec((ts, tc), lambda i, j: (i, j)),
                  pl.BlockSpec((halo, tc), lambda i, j: (jnp.maximum(i * (ts // halo) - 1, 0), j)),
                  pl.BlockSpec((kw, tc), lambda i, j: (0, j)),
                  pl.BlockSpec((1, tc), lambda i, j: (0, j))],
        out_specs=pl.BlockSpec((ts, tc), lambda i, j: (i, j)),
        out_shape=jax.ShapeDtypeStruct((n, width), F32),
        compiler_params=_params("parallel", "parallel"),
        name="conv_silu",
    )(proj, proj, conv_w, conv_b.reshape(1, width))


def _fox_kernel(q_ref, k_ref, v_ref, gq_ref, ck_ref, o_ref, *, tk, scale):
    tq, dh = q_ref.shape
    h = pl.program_id(1)
    q_start = pl.program_id(2) * tq
    q = q_ref[...]
    lane = lax.broadcasted_iota(jnp.int32, gq_ref.shape, 1)
    cq = jnp.sum(jnp.where(lane == h, gq_ref[...], 0.0), axis=-1, keepdims=True)

    def step(kb, carry, masked):
        m, l, acc = carry
        ks = pl.multiple_of(kb * tk, tk)
        k = k_ref[pl.ds(ks, tk), :]
        v = v_ref[pl.ds(ks, tk), :]
        s = lax.dot_general(q, k, (((1,), (1,)), ((), ())), preferred_element_type=F32) * scale
        s = s + (cq - ck_ref[kb])
        if masked:
            q_pos = q_start + lax.broadcasted_iota(jnp.int32, (tq, tk), 0)
            k_pos = ks + lax.broadcasted_iota(jnp.int32, (tq, tk), 1)
            s = jnp.where(k_pos <= q_pos, s, -jnp.inf)
        m_new = jnp.maximum(m, jnp.max(s, axis=-1, keepdims=True))
        a = jnp.exp(m - m_new)
        p = jnp.exp(s - m_new)
        l = a * l + jnp.sum(p, axis=-1, keepdims=True)
        acc = a * acc + jnp.dot(p.astype(BF16), v, preferred_element_type=F32)
        return m_new, l, acc

    init = (jnp.full((tq, 1), -jnp.inf, F32), jnp.zeros((tq, 1), F32), jnp.zeros((tq, dh), F32))
    n_full = q_start // tk
    n_any = (q_start + tq + tk - 1) // tk
    carry = lax.fori_loop(0, n_full, functools.partial(step, masked=False), init)
    _, l, acc = lax.fori_loop(n_full, n_any, functools.partial(step, masked=True), carry)
    o_ref[...] = (acc / l).astype(o_ref.dtype)


def _fox_attention(qkv, gates, ck, batch, seq, heads, col_q, col_k, col_v):
    n = qkv.shape[0]
    dh = FOX_HEAD_DIM
    tq = _tile(seq, 512)
    tk = ck.shape[-1]
    nq, nk = seq // tq, seq // tk
    kv_spec = lambda col: pl.BlockSpec((seq, dh), lambda b, h, i: (b, col + h))
    return pl.pallas_call(
        functools.partial(_fox_kernel, tk=tk, scale=dh ** -0.5),
        grid=(batch, heads, nq),
        in_specs=[pl.BlockSpec((tq, dh), lambda b, h, i: (b * nq + i, col_q + h)),
                  kv_spec(col_k), kv_spec(col_v),
                  pl.BlockSpec((tq, gates.shape[1]), lambda b, h, i: (b * nq + i, 0)),
                  pl.BlockSpec((None, None, nk, 1, tk), lambda b, h, i: (b, h, 0, 0, 0))],
        out_specs=pl.BlockSpec((tq, dh), lambda b, h, i: (b * nq + i, h)),
        out_shape=jax.ShapeDtypeStruct((n, heads * dh), BF16),
        compiler_params=_params("parallel", "parallel", "arbitrary"),
        name="fox_attention",
    )(qkv, qkv, qkv, gates, ck)


def _mlstm_kernel(q_ref, k_ref, v_ref, gate_ref, brow_ref, irow_ref, mo_ref, g_ref, o_ref,
                  c_ref, n_ref, m_ref, *, lane_i, lane_b):
    L = q_ref.shape[0]
    h = pl.program_id(1)

    @pl.when(pl.program_id(2) == 0)
    def _():
        c_ref[...] = jnp.zeros_like(c_ref)
        n_ref[...] = jnp.zeros_like(n_ref)
        m_ref[...] = jnp.zeros_like(m_ref)

    gates = gate_ref[...]
    lane = lax.broadcasted_iota(jnp.int32, gates.shape, 1)
    pick = lambda col: jnp.sum(jnp.where(lane == col + h, gates, 0.0), axis=-1, keepdims=True)
    icol = pick(lane_i)
    bcol = pick(lane_b)
    brow = brow_ref[...]
    irow = irow_ref[...]
    m_prev = m_ref[0:1, 0:1]

    q = q_ref[...].astype(BF16)
    k32 = k_ref[...]
    v = v_ref[...]

    tri = lax.broadcasted_iota(jnp.int32, (L, L), 0) >= lax.broadcasted_iota(jnp.int32, (L, L), 1)
    d = jnp.where(tri, bcol - brow + irow, -jnp.inf)
    inter = bcol + m_prev
    m = jnp.maximum(inter, jnp.max(d, axis=-1, keepdims=True))
    w_intra = jnp.exp(d - m)
    w_inter = jnp.exp(inter - m)
    s = lax.dot_general(q, k32.astype(BF16), (((1,), (1,)), ((), ())), preferred_element_type=F32)
    sc = s * w_intra
    num = (jnp.dot(sc.astype(BF16), v, preferred_element_type=F32)
           + w_inter * jnp.dot(q, c_ref[...].astype(BF16), preferred_element_type=F32))
    qn = jnp.sum(q_ref[...] * n_ref[...], axis=-1, keepdims=True)
    den = jnp.sum(sc, axis=-1, keepdims=True) + w_inter * qn
    hid = num / jnp.maximum(jnp.abs(den), jnp.exp(-m))

    b_last = bcol[L - 1:L, :]
    gcol = b_last - bcol + icol
    m_new = jnp.maximum(b_last + m_prev, jnp.max(gcol, axis=0, keepdims=True))
    w_k = jnp.exp(gcol - m_new)
    decay = jnp.exp(b_last + m_prev - m_new)
    kw = k32 * w_k
    c_ref[...] = decay * c_ref[...] + lax.dot_general(
        kw.astype(BF16), v, (((0,), (0,)), ((), ())), preferred_element_type=F32)
    n_ref[...] = decay * n_ref[...] + jnp.sum(kw, axis=0, keepdims=True)
    m_ref[...] = jnp.broadcast_to(m_new, m_ref.shape)

    mu = jnp.mean(hid, axis=-1, keepdims=True)
    hc = hid - mu
    var = jnp.mean(hc * hc, axis=-1, keepdims=True)
    hn = hc * lax.rsqrt(var + LN_EPS)
    o_ref[...] = (hn * g_ref[...] * jax.nn.sigmoid(mo_ref[...])).astype(o_ref.dtype)


def _mlstm(qk, qkv, proj, gates, brow, irow, norm_g, batch, seq, chunk, col_v, col_mo, lane_i, lane_b):
    n = qk.shape[0]
    dk, dv = ML_QK_DIM, ML_V_DIM
    nc = seq // chunk
    row = lambda b, h, c: b * nc + c
    vec = pl.BlockSpec((None, None, None, 1, chunk), lambda b, h, c: (b, h, c, 0, 0))
    return pl.pallas_call(
        functools.partial(_mlstm_kernel, lane_i=lane_i, lane_b=lane_b),
        grid=(batch, ML_HEADS, nc),
        in_specs=[pl.BlockSpec((chunk, dk), lambda b, h, c: (row(b, h, c), h)),
                  pl.BlockSpec((chunk, dk), lambda b, h, c: (row(b, h, c), ML_HEADS + h)),
                  pl.BlockSpec((chunk, dv), lambda b, h, c: (row(b, h, c), col_v + h)),
                  pl.BlockSpec((chunk, gates.shape[1]), lambda b, h, c: (row(b, h, c), 0)),
                  vec, vec,
                  pl.BlockSpec((chunk, dv), lambda b, h, c: (row(b, h, c), col_mo + h)),
                  pl.BlockSpec((1, dv), lambda b, h, c: (0, h))],
        out_specs=pl.BlockSpec((chunk, dv), lambda b, h, c: (row(b, h, c), h)),
        out_shape=jax.ShapeDtypeStruct((n, ML_HEADS * dv), BF16),
        scratch_shapes=[pltpu.VMEM((dk, dv), F32), pltpu.VMEM((1, dk), F32), pltpu.VMEM((8, 128), F32)],
        compiler_params=_params("parallel", "parallel", "arbitrary"),
        name="mlstm",
    )(qk, qk, qkv, gates, brow, irow, proj, norm_g.reshape(1, ML_HEADS * dv))


def _merge_kernel(yf_ref, ym_ref, wf_ref, wm_ref, ga_ref, gb_ref, o_ref):
    a = jnp.dot(yf_ref[...], wf_ref[...], preferred_element_type=F32)
    b = jnp.dot(ym_ref[...], wm_ref[...], preferred_element_type=F32)
    o_ref[...] = (jax.nn.sigmoid(ga_ref[...]) * a + jax.nn.sigmoid(gb_ref[...]) * b).astype(o_ref.dtype)


def _merge(y_fox, y_ml, w_fox, w_ml, proj, col_ga, col_gb):
    n, kf = y_fox.shape
    km = y_ml.shape[1]
    d = w_fox.shape[1]
    bm, bn = _tile(n, 512), _tile(d, 1024)
    ga0, gb0 = col_ga // bn, col_gb // bn
    return pl.pallas_call(
        _merge_kernel,
        grid=(n // bm, d // bn),
        in_specs=[pl.BlockSpec((bm, kf), lambda i, j: (i, 0)),
                  pl.BlockSpec((bm, km), lambda i, j: (i, 0)),
                  pl.BlockSpec((kf, bn), lambda i, j: (0, j)),
                  pl.BlockSpec((km, bn), lambda i, j: (0, j)),
                  pl.BlockSpec((bm, bn), lambda i, j: (i, ga0 + j)),
                  pl.BlockSpec((bm, bn), lambda i, j: (i, gb0 + j))],
        out_specs=pl.BlockSpec((bm, bn), lambda i, j: (i, j)),
        out_shape=jax.ShapeDtypeStruct((n, d), BF16),
        compiler_params=_params("parallel", "parallel"),
        name="merge_proj",
    )(y_fox, y_ml, w_fox, w_ml, proj, proj)


def _router_kernel(x_ref, w_ref, b_ref, oi_ref, of_ref, cnt_ref, carry_ref, *, n_groups, per_group):
    @pl.when(pl.program_id(0) == 0)
    def _():
        carry_ref[...] = jnp.zeros_like(carry_ref)

    tr = x_ref.shape[0]
    n_exp = n_groups * per_group
    logits = jnp.dot(x_ref[...], w_ref[...], preferred_element_type=F32,
                     precision=lax.Precision.HIGHEST) + b_ref[...]
    lane = lax.broadcasted_iota(jnp.int32, logits.shape, 1)
    big = jnp.int32(logits.shape[1])
    first_lane = lambda cond: jnp.min(jnp.where(cond, lane, big), axis=-1, keepdims=True)

    is_group = (lane >= n_exp) & (lane < n_exp + n_groups)
    gl = jnp.where(is_group, logits, -jnp.inf)
    g_max = jnp.max(gl, axis=-1, keepdims=True)
    g_sel = first_lane(gl == g_max) - n_exp
    p_g_sel = 1.0 / jnp.sum(jnp.exp(gl - g_max), axis=-1, keepdims=True)

    in_group = (lane >= g_sel * per_group) & (lane < (g_sel + 1) * per_group)
    el = jnp.where(in_group, logits, -jnp.inf)
    e_max = jnp.max(el, axis=-1, keepdims=True)
    ee = jnp.exp(el - e_max)
    pe = jnp.where(in_group, ee / jnp.sum(ee, axis=-1, keepdims=True), -1.0)
    p1 = jnp.max(pe, axis=-1, keepdims=True)
    e1 = first_lane(pe == p1)
    pe2 = jnp.where(lane == e1, -1.0, pe)
    p2 = jnp.max(pe2, axis=-1, keepdims=True)
    e2 = first_lane(pe2 == p2)
    p_sum = p1 + p2
    w1 = p_g_sel * p1 / p_sum
    w2 = p_g_sel * p2 / p_sum

    onehot = (lane == e1) | (lane == e2)
    strict = (lax.broadcasted_iota(jnp.int32, (tr, tr), 0)
              > lax.broadcasted_iota(jnp.int32, (tr, tr), 1)).astype(BF16)
    before = jnp.dot(strict, onehot.astype(BF16), preferred_element_type=F32) + carry_ref[...]
    r1 = jnp.sum(jnp.where(lane == e1, before, 0.0), axis=-1, keepdims=True).astype(jnp.int32)
    r2 = jnp.sum(jnp.where(lane == e2, before, 0.0), axis=-1, keepdims=True).astype(jnp.int32)
    carry_ref[...] += jnp.sum(onehot.astype(F32), axis=0, keepdims=True)
    cnt_ref[...] = carry_ref[...].astype(jnp.int32)

    oi_ref[...] = jnp.where(lane == 0, e1, jnp.where(lane == 1, e2, jnp.where(lane == 2, r1, r2)))
    of_ref[...] = jnp.where(lane == 0, w1, w2)


def _router(x, w_route, b_route, n_groups, per_group):
    n, d = x.shape
    lanes = w_route.shape[1]
    tr = _tile(n, 256)
    row = pl.BlockSpec((tr, lanes), lambda i: (i, 0))
    vec = pl.BlockSpec((1, lanes), lambda i: (0, 0))
    return pl.pallas_call(
        functools.partial(_router_kernel, n_groups=n_groups, per_group=per_group),
        grid=(n // tr,),
        in_specs=[pl.BlockSpec((tr, d), lambda i: (i, 0)),
                  pl.BlockSpec((d, lanes), lambda i: (0, 0)), vec],
        out_specs=[row, row, vec],
        out_shape=[jax.ShapeDtypeStruct((n, lanes), jnp.int32),
                   jax.ShapeDtypeStruct((n, lanes), F32),
                   jax.ShapeDtypeStruct((1, lanes), jnp.int32)],
        scratch_shapes=[pltpu.VMEM((1, lanes), F32)],
        compiler_params=_params("arbitrary"),
        name="router",
    )(x, w_route, b_route)


def _expert_kernel(blk_e_ref, n_used_ref, tok_ref, x_hbm, wt_ref, wg_ref, wu_ref, wd_ref, o_ref,
                   xbuf, sem):
    b = pl.program_id(0)
    tb = xbuf.shape[0]

    @pl.when(b < n_used_ref[0])
    def _():
        def row_copy(r):
            return pltpu.make_async_copy(x_hbm.at[pl.ds(tok_ref[r], 1), :],
                                         xbuf.at[pl.ds(r, 1), :], sem)

        def start(r, c):
            row_copy(r).start()
            return c

        def wait(r, c):
            row_copy(r).wait()
            return c

        lax.fori_loop(0, tb, start, 0)
        lax.fori_loop(0, tb, wait, 0)
        xb = xbuf[...].astype(BF16)
        gate = jnp.dot(xb, wg_ref[...], preferred_element_type=F32)
        up = jnp.dot(xb, wu_ref[...], preferred_element_type=F32)
        act = (gate * jax.nn.sigmoid(gate) * up).astype(BF16)
        o_ref[...] = jnp.dot(act, wd_ref[...], preferred_element_type=F32) * wt_ref[...]

    @pl.when(b >= n_used_ref[0])
    def _():
        o_ref[...] = jnp.zeros_like(o_ref)


def _experts(x, slot_tok, slot_w, block_e, n_used, w_gate, w_up, w_down, tb):
    n, d = x.shape
    _, _, f = w_gate.shape
    n_blocks = block_e.shape[0]
    grid_spec = pltpu.PrefetchScalarGridSpec(
        num_scalar_prefetch=2,
        grid=(n_blocks,),
        in_specs=[pl.BlockSpec((tb,), lambda b, be, nu: (b,), memory_space=pltpu.SMEM),
                  pl.BlockSpec(memory_space=pl.ANY),
                  pl.BlockSpec((tb, 1), lambda b, be, nu: (b, 0)),
                  pl.BlockSpec((None, d, f), lambda b, be, nu: (be[b], 0, 0)),
                  pl.BlockSpec((None, d, f), lambda b, be, nu: (be[b], 0, 0)),
                  pl.BlockSpec((None, f, d), lambda b, be, nu: (be[b], 0, 0))],
        out_specs=pl.BlockSpec((tb, d), lambda b, be, nu: (b, 0)),
        scratch_shapes=[pltpu.VMEM((tb, d), F32), pltpu.SemaphoreType.DMA(())],
    )
    return pl.pallas_call(
        _expert_kernel,
        grid_spec=grid_spec,
        out_shape=jax.ShapeDtypeStruct((n_blocks * tb, d), F32),
        compiler_params=_params("arbitrary"),
        name="experts",
    )(block_e, n_used, slot_tok, x, slot_w.reshape(-1, 1), w_gate, w_up, w_down)


def _combine_kernel(dest_ref, y_hbm, h_ref, g_ref, b_ref, o_ref, buf, sem, *, alpha):
    tm = h_ref.shape[0]

    def row_copy(r, k):
        return pltpu.make_async_copy(y_hbm.at[pl.ds(dest_ref[r * TOP_K + k], 1), :],
                                     buf.at[k, pl.ds(r, 1), :], sem)

    def start(r, c):
        for k in range(TOP_K):
            row_copy(r, k).start()
        return c

    def wait(r, c):
        for k in range(TOP_K):
            row_copy(r, k).wait()
        return c

    lax.fori_loop(0, tm, start, 0)
    lax.fori_loop(0, tm, wait, 0)
    y = buf[0]
    for k in range(1, TOP_K):
        y = y + buf[k]
    o_ref[...] = _layer_norm(alpha * h_ref[...] + y, g_ref[...], b_ref[...])


def _combine_ln(h, y_slots, dest, g, b, alpha):
    n, d = h.shape
    tm = _tile(n, 256)
    return pl.pallas_call(
        functools.partial(_combine_kernel, alpha=alpha),
        grid=(n // tm,),
        in_specs=[pl.BlockSpec((tm * TOP_K,), lambda i: (i,), memory_space=pltpu.SMEM),
                  pl.BlockSpec(memory_space=pl.ANY),
                  pl.BlockSpec((tm, d), lambda i: (i, 0)),
                  pl.BlockSpec((1, d), lambda i: (0, 0)),
                  pl.BlockSpec((1, d), lambda i: (0, 0))],
        out_specs=pl.BlockSpec((tm, d), lambda i: (i, 0)),
        out_shape=jax.ShapeDtypeStruct((n, d), F32),
        scratch_shapes=[pltpu.VMEM((TOP_K, tm, d), F32), pltpu.SemaphoreType.DMA(())],
        compiler_params=_params("arbitrary"),
        name="combine_ln",
    )(dest, y_slots, h, g.reshape(1, d), b.reshape(1, d))


EXPERT_SLOT_BLOCK = 256
ML_CHUNK = 256
FOX_KEY_BLOCK = 512


def _mixer(h32, h16, batch, seq, w_in, b_fox_f, b_ml_i, b_ml_f, conv_w, conv_b, ml_norm_g,
           w_proj_fox, w_proj_ml, w_out):
    n, d = h32.shape
    fox_w = d // 2
    fox_heads = fox_w // FOX_HEAD_DIM
    qk_w = ML_HEADS * ML_QK_DIM
    v_w = ML_HEADS * ML_V_DIM
    widths = (fox_w, fox_w, fox_w, fox_heads, qk_w, qk_w, v_w, ML_HEADS, ML_HEADS, v_w, d, d)
    offs = [0]
    for w in widths:
        offs.append(offs[-1] + w)
    seg = lambda i: w_in[:, offs[i]:offs[i + 1]]
    fq, fk, fv, ff, mq, mk, mv, mi, mf, mo, ga, gb = (seg(i) for i in range(12))

    w_a = jnp.concatenate([fq, fk, fv, mv], axis=1).astype(BF16)
    w_b = jnp.concatenate([mq, mk, mo, ga, gb], axis=1).astype(BF16)
    n_gate = fox_heads + 2 * ML_HEADS
    assert n_gate <= GATE_LANES
    w_c = jnp.pad(jnp.concatenate([ff, mi, mf], axis=1), ((0, 0), (0, GATE_LANES - n_gate))).astype(BF16)
    gate_bias = jnp.pad(jnp.concatenate([b_fox_f, b_ml_i, b_ml_f]), (0, GATE_LANES - n_gate)).reshape(1, -1)

    proj_a = _matmul(h16, w_a, BF16, "in_proj_a")
    proj_b = _matmul(h16, w_b, F32, "in_proj_b")
    gate_pre = _matmul(h16, w_c, F32, "in_proj_gates")

    chunk = _tile(seq, ML_CHUNK)
    gates = _gates(gate_pre, gate_bias, batch, chunk, fox_heads)

    g3 = gates.reshape(batch, seq, GATE_LANES)
    tk = _tile(seq, FOX_KEY_BLOCK)
    ck = jnp.transpose(g3[:, :, :fox_heads], (0, 2, 1)).reshape(batch, fox_heads, seq // tk, 1, tk)
    lane_i, lane_b = fox_heads, fox_heads + ML_HEADS
    rows = lambda lo: jnp.transpose(g3[:, :, lo:lo + ML_HEADS], (0, 2, 1)).reshape(
        batch, ML_HEADS, seq // chunk, 1, chunk)
    irow, brow = rows(lane_i), rows(lane_b)

    y_fox = _fox_attention(proj_a, gates, ck, batch, seq, fox_heads,
                           col_q=0, col_k=fox_heads, col_v=2 * fox_heads)
    qk = _conv_silu(proj_b, conv_w, conv_b, seq, 2 * qk_w)
    y_ml = _mlstm(qk, proj_a, proj_b, gates, brow, irow, ml_norm_g, batch, seq, chunk,
                  col_v=3 * fox_w // ML_V_DIM, col_mo=2 * qk_w // ML_V_DIM,
                  lane_i=lane_i, lane_b=lane_b)
    merged = _merge(y_fox, y_ml, w_proj_fox.astype(BF16), w_proj_ml.astype(BF16), proj_b,
                    col_ga=2 * qk_w + v_w, col_gb=2 * qk_w + v_w + d)
    return _matmul(merged, w_out.astype(BF16), F32, "out_proj")


def _moe(h, w_group, b_group, w_router, b_router, w_gate, w_up, w_down, ln_g, ln_b, alpha):
    n, d = h.shape
    n_groups = w_group.shape[1]
    n_exp = w_router.shape[1]
    lanes = GATE_LANES
    assert n_exp + n_groups <= lanes
    pad = lanes - n_exp - n_groups
    w_route = jnp.pad(jnp.concatenate([w_router, w_group], axis=1), ((0, 0), (0, pad)))
    b_route = jnp.pad(jnp.concatenate([b_router, b_group]), (0, pad)).reshape(1, lanes)
    oi, of, cnt = _router(h, w_route, b_route, n_groups, n_exp // n_groups)
    e_idx, rank, gate_w = oi[:, 0:TOP_K], oi[:, TOP_K:2 * TOP_K], of[:, 0:TOP_K]
    counts = cnt[0, :n_exp]

    tb = EXPERT_SLOT_BLOCK
    n_assign = n * TOP_K
    n_blocks = (n_assign + n_exp * (tb - 1) + tb - 1) // tb
    padded = (counts + tb - 1) // tb * tb
    pad_ends = jnp.cumsum(padded)
    dest = ((pad_ends - padded)[e_idx] + rank).reshape(-1)
    tok = jnp.repeat(jnp.arange(n, dtype=jnp.int32), TOP_K)
    slot_tok = jnp.zeros((n_blocks * tb,), jnp.int32).at[dest].set(tok)
    slot_w = jnp.zeros((n_blocks * tb,), F32).at[dest].set(gate_w.reshape(-1))
    n_used = pad_ends[-1] // tb
    blk = jnp.arange(n_blocks, dtype=jnp.int32)
    block_e = jnp.minimum(jnp.searchsorted(pad_ends, blk * tb, side="right"), n_exp - 1).astype(jnp.int32)
    block_e = jnp.where(blk < n_used, block_e, block_e[jnp.maximum(n_used - 1, 0)])

    y_slots = _experts(h, slot_tok, slot_w, block_e, n_used.reshape(1).astype(jnp.int32),
                       w_gate.astype(BF16), w_up.astype(BF16), w_down.astype(BF16), tb)
    return _combine_ln(h, y_slots, dest.astype(jnp.int32), ln_g, ln_b, alpha)


def kernel(x, ln_in_g, ln_in_b, w_in, b_fox_f, b_ml_i, b_ml_f, conv_w, conv_b, ml_norm_g, w_proj_fox, w_proj_ml, w_out, ln_mix_g, ln_mix_b, w_group, b_group, w_router, b_router, w_gate, w_up, w_down, ln_moe_g, ln_moe_b):
    batch, seq, d = x.shape
    depth = w_in.shape[0]
    alpha = (2 * depth) ** 0.25
    h32, h16 = _ln_in(x.reshape(batch * seq, d), ln_in_g, ln_in_b)
    for l in range(depth):
        mix = _mixer(h32, h16, batch, seq, w_in[l], b_fox_f[l], b_ml_i[l], b_ml_f[l], conv_w[l],
                     conv_b[l], ml_norm_g[l], w_proj_fox[l], w_proj_ml[l], w_out[l])
        h32 = _ln_res(h32, mix, ln_mix_g[l], ln_mix_b[l], alpha)
        h32 = _moe(h32, w_group[l], b_group[l], w_router[l], b_router[l], w_gate[l], w_up[l],
                   w_down[l], ln_moe_g[l], ln_moe_b[l], alpha)
        if l + 1 < depth:
            h16 = h32.astype(BF16)
    return h32.reshape(batch, seq, d)
```

```python
import functools

import jax
import jax.numpy as jnp
from jax import lax
from jax.experimental import pallas as pl
from jax.experimental.pallas import tpu as pltpu

F32 = jnp.float32
BF16 = jnp.bfloat16

LN_EPS = 1e-5
FOX_HEAD_DIM = 128
ML_HEADS = 4
ML_QK_DIM = 256
ML_V_DIM = 512
TOP_K = 2
GATE_LANES = 128
LOG2_E = 1.4426950408889634

V7X_VMEM_BYTES = 64 * 1024 * 1024
VMEM_LIMIT_BYTES = V7X_VMEM_BYTES - 8 * 1024 * 1024


def _params(*sem):
    return pltpu.CompilerParams(dimension_semantics=sem, vmem_limit_bytes=VMEM_LIMIT_BYTES)


def _tile(dim, pref):
    t = min(dim, pref)
    while dim % t:
        t //= 2
    return t


def _layer_norm(x, g, b):
    mu = jnp.mean(x, axis=-1, keepdims=True)
    xc = x - mu
    var = jnp.mean(xc * xc, axis=-1, keepdims=True)
    return xc * lax.rsqrt(var + LN_EPS) * g + b


def _log_sigmoid(x):
    return jnp.minimum(x, 0.0) - jnp.log1p(jnp.exp(-jnp.abs(x)))


def _ln_in_kernel(x_ref, g_ref, b_ref, o32_ref, o16_ref):
    y = _layer_norm(x_ref[...], g_ref[...], b_ref[...])
    o32_ref[...] = y
    o16_ref[...] = y.astype(BF16)


def _ln_in(x, g, b):
    n, d = x.shape
    tr = _tile(n, 256)
    row = pl.BlockSpec((tr, d), lambda i: (i, 0))
    vec = pl.BlockSpec((1, d), lambda i: (0, 0))
    return pl.pallas_call(
        _ln_in_kernel,
        grid=(n // tr,),
        in_specs=[row, vec, vec],
        out_specs=[row, row],
        out_shape=[jax.ShapeDtypeStruct((n, d), F32), jax.ShapeDtypeStruct((n, d), BF16)],
        compiler_params=_params("parallel"),
        name="ln_in",
    )(x, g.reshape(1, d), b.reshape(1, d))


def _ln_res_kernel(h_ref, y_ref, g_ref, b_ref, o_ref, *, alpha):
    o_ref[...] = _layer_norm(alpha * h_ref[...] + y_ref[...], g_ref[...], b_ref[...])


def _ln_res(h, y, g, b, alpha):
    n, d = h.shape
    tr = _tile(n, 256)
    row = pl.BlockSpec((tr, d), lambda i: (i, 0))
    vec = pl.BlockSpec((1, d), lambda i: (0, 0))
    return pl.pallas_call(
        functools.partial(_ln_res_kernel, alpha=alpha),
        grid=(n // tr,),
        in_specs=[row, row, vec, vec],
        out_specs=row,
        out_shape=jax.ShapeDtypeStruct((n, d), F32),
        compiler_params=_params("parallel"),
        name="ln_mix",
    )(h, y, g.reshape(1, d), b.reshape(1, d))


def _mm_kernel(x_ref, w_ref, o_ref):
    o_ref[...] = jnp.dot(x_ref[...], w_ref[...], preferred_element_type=F32).astype(o_ref.dtype)


def _matmul(x, w, out_dtype, name, bm=1024, bn=1024):
    m, k = x.shape
    _, n = w.shape
    bm, bn = _tile(m, bm), _tile(n, bn)
    return pl.pallas_call(
        _mm_kernel,
        grid=(m // bm, n // bn),
        in_specs=[pl.BlockSpec((bm, k), lambda i, j: (i, 0)),
                  pl.BlockSpec((k, bn), lambda i, j: (0, j))],
        out_specs=pl.BlockSpec((bm, bn), lambda i, j: (i, j)),
        out_shape=jax.ShapeDtypeStruct((m, n), out_dtype),
        compiler_params=_params("parallel", "parallel"),
        name=name,
    )(x, w)


def _gates_kernel(g_ref, bias_ref, o_ref, carry_ref, *, n_fox, fox_inv_scale):
    @pl.when(pl.program_id(1) == 0)
    def _():
        carry_ref[...] = jnp.zeros_like(carry_ref)

    ts = g_ref.shape[0]
    x = g_ref[...] + bias_ref[...]
    lane = lax.broadcasted_iota(jnp.int32, x.shape, 1)
    is_input_gate = (lane >= n_fox) & (lane < n_fox + ML_HEADS)
    val = jnp.where(is_input_gate, 0.0, _log_sigmoid(x))
    tril = (lax.broadcasted_iota(jnp.int32, (ts, ts), 0)
            >= lax.broadcasted_iota(jnp.int32, (ts, ts), 1)).astype(BF16)
    hi = val.astype(BF16)
    rem = val - hi.astype(F32)
    mid = rem.astype(BF16)
    lo = (rem - mid.astype(F32)).astype(BF16)
    cs = (jnp.dot(tril, hi, preferred_element_type=F32)
          + jnp.dot(tril, mid, preferred_element_type=F32)
          + jnp.dot(tril, lo, preferred_element_type=F32))
    total = cs + jnp.where(lane < n_fox, carry_ref[...], 0.0)
    carry_ref[...] = total[ts - 1:ts, :]
    total = jnp.where(lane < n_fox, total * fox_inv_scale, total)
    o_ref[...] = jnp.where(is_input_gate, x, total)


def _gates(g, bias, batch, ts, n_fox):
    n, lanes = g.shape
    tiles = n // batch // ts
    blk = pl.BlockSpec((ts, lanes), lambda b, j: (b * tiles + j, 0))
    return pl.pallas_call(
        functools.partial(_gates_kernel, n_fox=n_fox, fox_inv_scale=FOX_HEAD_DIM ** 0.5),
        grid=(batch, tiles),
        in_specs=[blk, pl.BlockSpec((1, lanes), lambda b, j: (0, 0))],
        out_specs=blk,
        out_shape=jax.ShapeDtypeStruct((n, lanes), F32),
        scratch_shapes=[pltpu.VMEM((1, lanes), F32)],
        compiler_params=_params("parallel", "arbitrary"),
        name="gates",
    )(g, bias)


def _conv_kernel(x_ref, prev_ref, w_ref, b_ref, o_ref, *, tiles_per_seq, k_blocks_from, k_scale):
    ts = x_ref.shape[0]
    kw = w_ref.shape[0]
    halo = prev_ref.shape[0]
    first = pl.program_id(0) % tiles_per_seq == 0
    prev = jnp.where(first, 0.0, prev_ref[...])
    ext = jnp.concatenate([prev, x_ref[...]], axis=0)
    w = w_ref[...]
    acc = None
    for j in range(kw):
        off = halo - (kw - 1) + j
        term = w[j:j + 1, :] * ext[off:off + ts, :]
        acc = term if acc is None else acc + term
    acc = acc + b_ref[...]
    y = acc * jax.nn.sigmoid(acc)
    scale = jnp.where(pl.program_id(1) >= k_blocks_from, k_scale, 1.0)
    o_ref[...] = y * scale


def _conv_silu(proj, conv_w, conv_b, seq, width):
    n = proj.shape[0]
    kw = conv_w.shape[0]
    ts, tc, halo = _tile(seq, 512), _tile(width // 2, 512), 8
    assert kw - 1 <= halo
    tiles_per_seq = seq // ts
    return pl.pallas_call(
        functools.partial(_conv_kernel, tiles_per_seq=tiles_per_seq,
                          k_blocks_from=(width // 2) // tc, k_scale=ML_QK_DIM ** -0.5),
        grid=(n // ts, width // tc),
        in_specs=[pl.BlockSpec((ts, tc), lambda i, j: (i, j)),
                  pl.BlockSpec((halo, tc), lambda i, j: (jnp.maximum(i * (ts // halo) - 1, 0), j)),
                  pl.BlockSpec((kw, tc), lambda i, j: (0, j)),
                  pl.BlockSpec((1, tc), lambda i, j: (0, j))],
        out_specs=pl.BlockSpec((ts, tc), lambda i, j: (i, j)),
        out_shape=jax.ShapeDtypeStruct((n, width), F32),
        compiler_params=_params("parallel", "parallel"),
        name="conv_silu",
    )(proj, proj, conv_w, conv_b.reshape(1, width))


def _fox_kernel(q_ref, k_ref, vt_ref, cq_ref, ck_ref, o_ref,
                ua_ref, ub_ref, mxa_ref, mxb_ref, m_ref, l_ref, acc_ref, *, scale):
    tq, dh = q_ref.shape
    tk = tq
    qi = pl.program_id(2)
    c_exp = scale * LOG2_E
    q = q_ref[...]
    cq = cq_ref[...]
    buf_a, buf_b = (ua_ref, mxa_ref), (ub_ref, mxb_ref)

    def score(kb, buf, masked):
        u_ref, mx_ref = buf
        ks = pl.multiple_of(kb * tk, tk)
        u = lax.dot_general(k_ref[pl.ds(ks, tk), :], q, (((1,), (1,)), ((), ())),
                            preferred_element_type=F32) - ck_ref[pl.ds(ks, tk), :]
        if masked:
            u = jnp.where(lax.broadcasted_iota(jnp.int32, (tk, tq), 0)
                          <= lax.broadcasted_iota(jnp.int32, (tk, tq), 1), u, -jnp.inf)
        u_ref[...] = u
        mx_ref[...] = jnp.max(u, axis=0, keepdims=True)

    def absorb(kb, buf):
        u_ref, mx_ref = buf
        ks = pl.multiple_of(kb * tk, tk)
        m = m_ref[...]
        m_new = jnp.maximum(m, mx_ref[...] + cq)
        p = jnp.exp2((u_ref[...] + (cq - m_new)) * c_exp)
        a = jnp.exp2((m - m_new) * c_exp)
        l_ref[...] = a * l_ref[...] + jnp.sum(p, axis=0, keepdims=True)
        acc_ref[...] = a * acc_ref[...] + jnp.dot(vt_ref[:, pl.ds(ks, tk)], p.astype(BF16),
                                                  preferred_element_type=F32)
        m_ref[...] = m_new

    m_ref[...] = jnp.full_like(m_ref, -jnp.inf)
    l_ref[...] = jnp.zeros_like(l_ref)
    acc_ref[...] = jnp.zeros_like(acc_ref)

    @pl.when(qi == 0)
    def _():
        score(0, buf_a, masked=True)
        absorb(0, buf_a)

    @pl.when(qi > 0)
    def _():
        score(0, buf_a, masked=False)
        n_pairs = (qi - 1) // 2

        def pair(g, c):
            score(2 * g + 1, buf_b, masked=False)
            absorb(2 * g, buf_a)
            score(2 * g + 2, buf_a, masked=False)
            absorb(2 * g + 1, buf_b)
            return c

        lax.fori_loop(0, n_pairs, pair, 0)
        base = 2 * n_pairs

        @pl.when(qi % 2 == 1)
        def _():
            score(qi, buf_b, masked=True)
            absorb(base, buf_a)
            absorb(qi, buf_b)

        @pl.when(qi % 2 == 0)
        def _():
            score(base + 1, buf_b, masked=False)
            absorb(base, buf_a)
            score(qi, buf_a, masked=True)
            absorb(base + 1, buf_b)
            absorb(qi, buf_a)

    o_ref[...] = jnp.transpose(acc_ref[...] / l_ref[...]).astype(o_ref.dtype)


def _fox_attention(qk, vt, c_rows, c_cols, batch, seq, heads, col_q, col_k):
    n = qk.shape[0]
    dh = FOX_HEAD_DIM
    tq = c_rows.shape[-1]
    nq = seq // tq
    return pl.pallas_call(
        functools.partial(_fox_kernel, scale=dh ** -0.5),
        grid=(batch, heads, nq),
        in_specs=[pl.BlockSpec((tq, dh), lambda b, h, i: (b * nq + i, col_q + h)),
                  pl.BlockSpec((seq, dh), lambda b, h, i: (b, col_k + h)),
                  pl.BlockSpec((None, None, dh, seq), lambda b, h, i: (b, h, 0, 0)),
                  pl.BlockSpec((None, None, None, 1, tq), lambda b, h, i: (b, h, i, 0, 0)),
                  pl.BlockSpec((None, None, seq, 1), lambda b, h, i: (b, h, 0, 0))],
        out_specs=pl.BlockSpec((tq, dh), lambda b, h, i: (b * nq + i, h)),
        out_shape=jax.ShapeDtypeStruct((n, heads * dh), BF16),
        scratch_shapes=[pltpu.VMEM((tq, tq), F32), pltpu.VMEM((tq, tq), F32),
                        pltpu.VMEM((1, tq), F32), pltpu.VMEM((1, tq), F32),
                        pltpu.VMEM((1, tq), F32), pltpu.VMEM((1, tq), F32),
                        pltpu.VMEM((dh, tq), F32)],
        compiler_params=_params("parallel", "parallel", "arbitrary"),
        name="fox_attention",
    )(qk, qk, vt, c_rows, c_cols)


def _mlstm_kernel(q_ref, k_ref, v_ref, gate_ref, brow_ref, irow_ref, mo_ref, g_ref, o_ref,
                  c_ref, n_ref, m_ref, *, lane_i, lane_b):
    L = q_ref.shape[0]
    h = pl.program_id(1)

    @pl.when(pl.program_id(2) == 0)
    def _():
        c_ref[...] = jnp.zeros_like(c_ref)
        n_ref[...] = jnp.zeros_like(n_ref)
        m_ref[...] = jnp.zeros_like(m_ref)

    gates = gate_ref[...]
    lane = lax.broadcasted_iota(jnp.int32, gates.shape, 1)
    pick = lambda col: jnp.sum(jnp.where(lane == col + h, gates, 0.0), axis=-1, keepdims=True)
    icol = pick(lane_i)
    bcol = pick(lane_b)
    brow = brow_ref[...]
    irow = irow_ref[...]
    m_prev = m_ref[0:1, 0:1]

    q = q_ref[...].astype(BF16)
    k32 = k_ref[...]
    v = v_ref[...]

    tri = lax.broadcasted_iota(jnp.int32, (L, L), 0) >= lax.broadcasted_iota(jnp.int32, (L, L), 1)
    d = jnp.where(tri, bcol - brow + irow, -jnp.inf)
    inter = bcol + m_prev
    m = jnp.maximum(inter, jnp.max(d, axis=-1, keepdims=True))
    w_intra = jnp.exp(d - m)
    w_inter = jnp.exp(inter - m)
    s = lax.dot_general(q, k32.astype(BF16), (((1,), (1,)), ((), ())), preferred_element_type=F32)
    sc = s * w_intra
    num = (jnp.dot(sc.astype(BF16), v, preferred_element_type=F32)
           + w_inter * jnp.dot(q, c_ref[...].astype(BF16), preferred_element_type=F32))
    qn = jnp.sum(q_ref[...] * n_ref[...], axis=-1, keepdims=True)
    den = jnp.sum(sc, axis=-1, keepdims=True) + w_inter * qn
    hid = num / jnp.maximum(jnp.abs(den), jnp.exp(-m))

    b_last = bcol[L - 1:L, :]
    gcol = b_last - bcol + icol
    m_new = jnp.maximum(b_last + m_prev, jnp.max(gcol, axis=0, keepdims=True))
    w_k = jnp.exp(gcol - m_new)
    decay = jnp.exp(b_last + m_prev - m_new)
    kw = k32 * w_k
    c_ref[...] = decay * c_ref[...] + lax.dot_general(
        kw.astype(BF16), v, (((0,), (0,)), ((), ())), preferred_element_type=F32)
    n_ref[...] = decay * n_ref[...] + jnp.sum(kw, axis=0, keepdims=True)
    m_ref[...] = jnp.broadcast_to(m_new, m_ref.shape)

    mu = jnp.mean(hid, axis=-1, keepdims=True)
    hc = hid - mu
    var = jnp.mean(hc * hc, axis=-1, keepdims=True)
    hn = hc * lax.rsqrt(var + LN_EPS)
    o_ref[...] = (hn * g_ref[...] * jax.nn.sigmoid(mo_ref[...])).astype(o_ref.dtype)


def _mlstm(qk, qkv, proj, gates, brow, irow, norm_g, batch, seq, chunk, col_v, col_mo, lane_i, lane_b):
    n = qk.shape[0]
    dk, dv = ML_QK_DIM, ML_V_DIM
    nc = seq // chunk
    row = lambda b, h, c: b * nc + c
    vec = pl.BlockSpec((None, None, None, 1, chunk), lambda b, h, c: (b, h, c, 0, 0))
    return pl.pallas_call(
        functools.partial(_mlstm_kernel, lane_i=lane_i, lane_b=lane_b),
        grid=(batch, ML_HEADS, nc),
        in_specs=[pl.BlockSpec((chunk, dk), lambda b, h, c: (row(b, h, c), h)),
                  pl.BlockSpec((chunk, dk), lambda b, h, c: (row(b, h, c), ML_HEADS + h)),
                  pl.BlockSpec((chunk, dv), lambda b, h, c: (row(b, h, c), col_v + h)),
                  pl.BlockSpec((chunk, gates.shape[1]), lambda b, h, c: (row(b, h, c), 0)),
                  vec, vec,
                  pl.BlockSpec((chunk, dv), lambda b, h, c: (row(b, h, c), col_mo + h)),
                  pl.BlockSpec((1, dv), lambda b, h, c: (0, h))],
        out_specs=pl.BlockSpec((chunk, dv), lambda b, h, c: (row(b, h, c), h)),
        out_shape=jax.ShapeDtypeStruct((n, ML_HEADS * dv), BF16),
        scratch_shapes=[pltpu.VMEM((dk, dv), F32), pltpu.VMEM((1, dk), F32), pltpu.VMEM((8, 128), F32)],
        compiler_params=_params("parallel", "parallel", "arbitrary"),
        name="mlstm",
    )(qk, qk, qkv, gates, brow, irow, proj, norm_g.reshape(1, ML_HEADS * dv))


def _merge_kernel(yf_ref, ym_ref, wf_ref, wm_ref, ga_ref, gb_ref, o_ref):
    a = jnp.dot(yf_ref[...], wf_ref[...], preferred_element_type=F32)
    b = jnp.dot(ym_ref[...], wm_ref[...], preferred_element_type=F32)
    o_ref[...] = (jax.nn.sigmoid(ga_ref[...]) * a + jax.nn.sigmoid(gb_ref[...]) * b).astype(o_ref.dtype)


def _merge(y_fox, y_ml, w_fox, w_ml, proj, col_ga, col_gb):
    n, kf = y_fox.shape
    km = y_ml.shape[1]
    d = w_fox.shape[1]
    bm, bn = _tile(n, 512), _tile(d, 1024)
    ga0, gb0 = col_ga // bn, col_gb // bn
    return pl.pallas_call(
        _merge_kernel,
        grid=(n // bm, d // bn),
        in_specs=[pl.BlockSpec((bm, kf), lambda i, j: (i, 0)),
                  pl.BlockSpec((bm, km), lambda i, j: (i, 0)),
                  pl.BlockSpec((kf, bn), lambda i, j: (0, j)),
                  pl.BlockSpec((km, bn), lambda i, j: (0, j)),
                  pl.BlockSpec((bm, bn), lambda i, j: (i, ga0 + j)),
                  pl.BlockSpec((bm, bn), lambda i, j: (i, gb0 + j))],
        out_specs=pl.BlockSpec((bm, bn), lambda i, j: (i, j)),
        out_shape=jax.ShapeDtypeStruct((n, d), BF16),
        compiler_params=_params("parallel", "parallel"),
        name="merge_proj",
    )(y_fox, y_ml, w_fox, w_ml, proj, proj)


def _router_kernel(x_ref, w_ref, b_ref, oi_ref, of_ref, cnt_ref, carry_ref, *, n_groups, per_group):
    @pl.when(pl.program_id(0) == 0)
    def _():
        carry_ref[...] = jnp.zeros_like(carry_ref)

    tr = x_ref.shape[0]
    n_exp = n_groups * per_group
    logits = jnp.dot(x_ref[...], w_ref[...], preferred_element_type=F32,
                     precision=lax.Precision.HIGHEST) + b_ref[...]
    lane = lax.broadcasted_iota(jnp.int32, logits.shape, 1)
    big = jnp.int32(logits.shape[1])
    first_lane = lambda cond: jnp.min(jnp.where(cond, lane, big), axis=-1, keepdims=True)

    is_group = (lane >= n_exp) & (lane < n_exp + n_groups)
    gl = jnp.where(is_group, logits, -jnp.inf)
    g_max = jnp.max(gl, axis=-1, keepdims=True)
    g_sel = first_lane(gl == g_max) - n_exp
    p_g_sel = 1.0 / jnp.sum(jnp.exp(gl - g_max), axis=-1, keepdims=True)

    in_group = (lane >= g_sel * per_group) & (lane < (g_sel + 1) * per_group)
    el = jnp.where(in_group, logits, -jnp.inf)
    e_max = jnp.max(el, axis=-1, keepdims=True)
    ee = jnp.exp(el - e_max)
    pe = jnp.where(in_group, ee / jnp.sum(ee, axis=-1, keepdims=True), -1.0)
    p1 = jnp.max(pe, axis=-1, keepdims=True)
    e1 = first_lane(pe == p1)
    pe2 = jnp.where(lane == e1, -1.0, pe)
    p2 = jnp.max(pe2, axis=-1, keepdims=True)
    e2 = first_lane(pe2 == p2)
    p_sum = p1 + p2
    w1 = p_g_sel * p1 / p_sum
    w2 = p_g_sel * p2 / p_sum

    onehot = (lane == e1) | (lane == e2)
    strict = (lax.broadcasted_iota(jnp.int32, (tr, tr), 0)
              > lax.broadcasted_iota(jnp.int32, (tr, tr), 1)).astype(BF16)
    before = jnp.dot(strict, onehot.astype(BF16), preferred_element_type=F32) + carry_ref[...]
    r1 = jnp.sum(jnp.where(lane == e1, before, 0.0), axis=-1, keepdims=True).astype(jnp.int32)
    r2 = jnp.sum(jnp.where(lane == e2, before, 0.0), axis=-1, keepdims=True).astype(jnp.int32)
    carry_ref[...] += jnp.sum(onehot.astype(F32), axis=0, keepdims=True)
    cnt_ref[...] = carry_ref[...].astype(jnp.int32)

    oi_ref[...] = jnp.where(lane == 0, e1, jnp.where(lane == 1, e2, jnp.where(lane == 2, r1, r2)))
    of_ref[...] = jnp.where(lane == 0, w1, w2)


def _router(x, w_route, b_route, n_groups, per_group):
    n, d = x.shape
    lanes = w_route.shape[1]
    tr = _tile(n, 256)
    row = pl.BlockSpec((tr, lanes), lambda i: (i, 0))
    vec = pl.BlockSpec((1, lanes), lambda i: (0, 0))
    return pl.pallas_call(
        functools.partial(_router_kernel, n_groups=n_groups, per_group=per_group),
        grid=(n // tr,),
        in_specs=[pl.BlockSpec((tr, d), lambda i: (i, 0)),
                  pl.BlockSpec((d, lanes), lambda i: (0, 0)), vec],
        out_specs=[row, row, vec],
        out_shape=[jax.ShapeDtypeStruct((n, lanes), jnp.int32),
                   jax.ShapeDtypeStruct((n, lanes), F32),
                   jax.ShapeDtypeStruct((1, lanes), jnp.int32)],
        scratch_shapes=[pltpu.VMEM((1, lanes), F32)],
        compiler_params=_params("arbitrary"),
        name="router",
    )(x, w_route, b_route)


def _expert_kernel(blk_e_ref, n_used_ref, tok_ref, tok_next_ref, x_hbm, wt_ref, wg_ref, wu_ref, wd_ref,
                   o_ref, xbuf, sem):
    b = pl.program_id(0)
    tb = xbuf.shape[1]
    n_used = n_used_ref[0]
    slot = b % 2

    def row_copy(ids_ref, r, s):
        return pltpu.make_async_copy(x_hbm.at[pl.ds(ids_ref[r], 1), :],
                                     xbuf.at[s, pl.ds(r, 1), :], sem.at[s])

    def wait_rows():
        def wait(r, c):
            row_copy(tok_ref, r, slot).wait()
            return c
        lax.fori_loop(0, tb, wait, 0)

    def compute():
        xb = xbuf[slot].astype(BF16)
        gate = jnp.dot(xb, wg_ref[...], preferred_element_type=F32)
        up = jnp.dot(xb, wu_ref[...], preferred_element_type=F32)
        act = (gate * jax.nn.sigmoid(gate) * up).astype(BF16)
        o_ref[...] = jnp.dot(act, wd_ref[...], preferred_element_type=F32) * wt_ref[...]

    @pl.when((b == 0) & (n_used > 0))
    def _():
        def start(r, c):
            row_copy(tok_ref, r, slot).start()
            return c
        lax.fori_loop(0, tb, start, 0)

    @pl.when(b + 1 < n_used)
    def _():
        wait_rows()
        for r in range(tb):
            row_copy(tok_next_ref, r, 1 - slot).start()
        compute()

    @pl.when(b + 1 == n_used)
    def _():
        wait_rows()
        compute()

    @pl.when(b >= n_used)
    def _():
        o_ref[...] = jnp.zeros_like(o_ref)


def _experts(x, slot_tok, slot_w, block_e, n_used, w_gate, w_up, w_down, tb):
    n, d = x.shape
    _, _, f = w_gate.shape
    n_blocks = block_e.shape[0]
    grid_spec = pltpu.PrefetchScalarGridSpec(
        num_scalar_prefetch=2,
        grid=(n_blocks,),
        in_specs=[pl.BlockSpec((tb,), lambda b, be, nu: (b,), memory_space=pltpu.SMEM),
                  pl.BlockSpec((tb,), lambda b, be, nu: (jnp.minimum(b + 1, n_blocks - 1),),
                               memory_space=pltpu.SMEM),
                  pl.BlockSpec(memory_space=pl.ANY),
                  pl.BlockSpec((tb, 1), lambda b, be, nu: (b, 0)),
                  pl.BlockSpec((None, d, f), lambda b, be, nu: (be[b], 0, 0)),
                  pl.BlockSpec((None, d, f), lambda b, be, nu: (be[b], 0, 0)),
                  pl.BlockSpec((None, f, d), lambda b, be, nu: (be[b], 0, 0))],
        out_specs=pl.BlockSpec((tb, d), lambda b, be, nu: (b, 0)),
        scratch_shapes=[pltpu.VMEM((2, tb, d), F32), pltpu.SemaphoreType.DMA((2,))],
    )
    return pl.pallas_call(
        _expert_kernel,
        grid_spec=grid_spec,
        out_shape=jax.ShapeDtypeStruct((n_blocks * tb, d), F32),
        compiler_params=_params("arbitrary"),
        name="experts",
    )(block_e, n_used, slot_tok, slot_tok, x, slot_w.reshape(-1, 1), w_gate, w_up, w_down)


def _combine_kernel(dest_ref, dest_next_ref, y_hbm, h_ref, g_ref, b_ref, o_ref, buf, sem, *, alpha):
    i = pl.program_id(0)
    last = pl.num_programs(0) - 1
    tm = h_ref.shape[0]
    slot = i % 2

    def row_copy(ids_ref, r, k, s):
        return pltpu.make_async_copy(y_hbm.at[pl.ds(ids_ref[r * TOP_K + k], 1), :],
                                     buf.at[s, k, pl.ds(r, 1), :], sem.at[s])

    def start_rows(ids_ref, s):
        def start(r, c):
            for k in range(TOP_K):
                row_copy(ids_ref, r, k, s).start()
            return c
        lax.fori_loop(0, tm, start, 0)

    def wait(r, c):
        for k in range(TOP_K):
            row_copy(dest_ref, r, k, slot).wait()
        return c

    @pl.when(i == 0)
    def _():
        start_rows(dest_ref, slot)

    @pl.when(i < last)
    def _():
        start_rows(dest_next_ref, 1 - slot)

    lax.fori_loop(0, tm, wait, 0)
    y = buf[slot, 0]
    for k in range(1, TOP_K):
        y = y + buf[slot, k]
    o_ref[...] = _layer_norm(alpha * h_ref[...] + y, g_ref[...], b_ref[...])


def _combine_ln(h, y_slots, dest, g, b, alpha):
    n, d = h.shape
    tm = _tile(n, 256)
    tiles = n // tm
    return pl.pallas_call(
        functools.partial(_combine_kernel, alpha=alpha),
        grid=(tiles,),
        in_specs=[pl.BlockSpec((tm * TOP_K,), lambda i: (i,), memory_space=pltpu.SMEM),
                  pl.BlockSpec((tm * TOP_K,), lambda i: (jnp.minimum(i + 1, tiles - 1),),
                               memory_space=pltpu.SMEM),
                  pl.BlockSpec(memory_space=pl.ANY),
                  pl.BlockSpec((tm, d), lambda i: (i, 0)),
                  pl.BlockSpec((1, d), lambda i: (0, 0)),
                  pl.BlockSpec((1, d), lambda i: (0, 0))],
        out_specs=pl.BlockSpec((tm, d), lambda i: (i, 0)),
        out_shape=jax.ShapeDtypeStruct((n, d), F32),
        scratch_shapes=[pltpu.VMEM((2, TOP_K, tm, d), F32), pltpu.SemaphoreType.DMA((2,))],
        compiler_params=_params("arbitrary"),
        name="combine_ln",
    )(dest, dest, y_slots, h, g.reshape(1, d), b.reshape(1, d))


EXPERT_SLOT_BLOCK = 256
ML_CHUNK = 256
FOX_KEY_BLOCK = 512
FOX_UNROLL = 2


def _mixer(h32, h16, batch, seq, w_in, b_fox_f, b_ml_i, b_ml_f, conv_w, conv_b, ml_norm_g,
           w_proj_fox, w_proj_ml, w_out):
    n, d = h32.shape
    fox_w = d // 2
    fox_heads = fox_w // FOX_HEAD_DIM
    qk_w = ML_HEADS * ML_QK_DIM
    v_w = ML_HEADS * ML_V_DIM
    widths = (fox_w, fox_w, fox_w, fox_heads, qk_w, qk_w, v_w, ML_HEADS, ML_HEADS, v_w, d, d)
    offs = [0]
    for w in widths:
        offs.append(offs[-1] + w)
    seg = lambda i: w_in[:, offs[i]:offs[i + 1]]
    fq, fk, fv, ff, mq, mk, mv, mi, mf, mo, ga, gb = (seg(i) for i in range(12))

    w_a = jnp.concatenate([fq, fk, fv, mv], axis=1).astype(BF16)
    w_b = jnp.concatenate([mq, mk, mo, ga, gb], axis=1).astype(BF16)
    n_gate = fox_heads + 2 * ML_HEADS
    assert n_gate <= GATE_LANES
    w_c = jnp.pad(jnp.concatenate([ff, mi, mf], axis=1), ((0, 0), (0, GATE_LANES - n_gate))).astype(BF16)
    gate_bias = jnp.pad(jnp.concatenate([b_fox_f, b_ml_i, b_ml_f]), (0, GATE_LANES - n_gate)).reshape(1, -1)

    proj_a = _matmul(h16, w_a, BF16, "in_proj_a")
    proj_b = _matmul(h16, w_b, F32, "in_proj_b")
    gate_pre = _matmul(h16, w_c, F32, "in_proj_gates")

    chunk = _tile(seq, ML_CHUNK)
    gates = _gates(gate_pre, gate_bias, batch, chunk, fox_heads)

    g3 = gates.reshape(batch, seq, GATE_LANES)
    tk = _tile(seq, FOX_KEY_BLOCK)
    c_heads = jnp.transpose(g3[:, :, :fox_heads], (0, 2, 1))
    c_rows = c_heads.reshape(batch, fox_heads, seq // tk, 1, tk)
    c_cols = c_heads.reshape(batch, fox_heads, seq, 1)
    lane_i, lane_b = fox_heads, fox_heads + ML_HEADS
    rows = lambda lo: jnp.transpose(g3[:, :, lo:lo + ML_HEADS], (0, 2, 1)).reshape(
        batch, ML_HEADS, seq // chunk, 1, chunk)
    irow, brow = rows(lane_i), rows(lane_b)

    v_t = jnp.transpose(proj_a[:, 2 * fox_w:3 * fox_w].reshape(batch, seq, fox_heads, FOX_HEAD_DIM),
                        (0, 2, 3, 1))
    y_fox = _fox_attention(proj_a, v_t, c_rows, c_cols, batch, seq, fox_heads,
                           col_q=0, col_k=fox_heads)
    qk = _conv_silu(proj_b, conv_w, conv_b, seq, 2 * qk_w)
    y_ml = _mlstm(qk, proj_a, proj_b, gates, brow, irow, ml_norm_g, batch, seq, chunk,
                  col_v=3 * fox_w // ML_V_DIM, col_mo=2 * qk_w // ML_V_DIM,
                  lane_i=lane_i, lane_b=lane_b)
    merged = _merge(y_fox, y_ml, w_proj_fox.astype(BF16), w_proj_ml.astype(BF16), proj_b,
                    col_ga=2 * qk_w + v_w, col_gb=2 * qk_w + v_w + d)
    return _matmul(merged, w_out.astype(BF16), F32, "out_proj")


def _moe(h, w_group, b_group, w_router, b_router, w_gate, w_up, w_down, ln_g, ln_b, alpha):
    n, d = h.shape
    n_groups = w_group.shape[1]
    n_exp = w_router.shape[1]
    lanes = GATE_LANES
    assert n_exp + n_groups <= lanes
    pad = lanes - n_exp - n_groups
    w_route = jnp.pad(jnp.concatenate([w_router, w_group], axis=1), ((0, 0), (0, pad)))
    b_route = jnp.pad(jnp.concatenate([b_router, b_group]), (0, pad)).reshape(1, lanes)
    oi, of, cnt = _router(h, w_route, b_route, n_groups, n_exp // n_groups)
    e_idx, rank, gate_w = oi[:, 0:TOP_K], oi[:, TOP_K:2 * TOP_K], of[:, 0:TOP_K]
    counts = cnt[0, :n_exp]

    tb = EXPERT_SLOT_BLOCK
    n_assign = n * TOP_K
    n_blocks = (n_assign + n_exp * (tb - 1) + tb - 1) // tb
    padded = (counts + tb - 1) // tb * tb
    pad_ends = jnp.cumsum(padded)
    dest = ((pad_ends - padded)[e_idx] + rank).reshape(-1)
    tok = jnp.repeat(jnp.arange(n, dtype=jnp.int32), TOP_K)
    slot_tok = jnp.zeros((n_blocks * tb,), jnp.int32).at[dest].set(tok)
    slot_w = jnp.zeros((n_blocks * tb,), F32).at[dest].set(gate_w.reshape(-1))
    n_used = pad_ends[-1] // tb
    blk = jnp.arange(n_blocks, dtype=jnp.int32)
    block_e = jnp.minimum(jnp.searchsorted(pad_ends, blk * tb, side="right"), n_exp - 1).astype(jnp.int32)
    block_e = jnp.where(blk < n_used, block_e, block_e[jnp.maximum(n_used - 1, 0)])

    y_slots = _experts(h, slot_tok, slot_w, block_e, n_used.reshape(1).astype(jnp.int32),
                       w_gate.astype(BF16), w_up.astype(BF16), w_down.astype(BF16), tb)
    return _combine_ln(h, y_slots, dest.astype(jnp.int32), ln_g, ln_b, alpha)


def kernel(x, ln_in_g, ln_in_b, w_in, b_fox_f, b_ml_i, b_ml_f, conv_w, conv_b, ml_norm_g, w_proj_fox, w_proj_ml, w_out, ln_mix_g, ln_mix_b, w_group, b_group, w_router, b_router, w_gate, w_up, w_down, ln_moe_g, ln_moe_b):
    batch, seq, d = x.shape
    depth = w_in.shape[0]
    alpha = (2 * depth) ** 0.25
    h32, h16 = _ln_in(x.reshape(batch * seq, d), ln_in_g, ln_in_b)
    for l in range(depth):
        mix = _mixer(h32, h16, batch, seq, w_in[l], b_fox_f[l], b_ml_i[l], b_ml_f[l], conv_w[l],
                     conv_b[l], ml_norm_g[l], w_proj_fox[l], w_proj_ml[l], w_out[l])
        h32 = _ln_res(h32, mix, ln_mix_g[l], ln_mix_b[l], alpha)
        h32 = _moe(h32, w_group[l], b_group[l], w_router[l], b_router[l], w_gate[l], w_up[l],
                   w_down[l], ln_moe_g[l], ln_moe_b[l], alpha)
        if l + 1 < depth:
            h16 = h32.astype(BF16)
    return h32.reshape(batch, seq, d)
```

```python
import functools

import jax
import jax.numpy as jnp
from jax import lax
from jax.experimental import pallas as pl
from jax.experimental.pallas import tpu as pltpu

F32 = jnp.float32
BF16 = jnp.bfloat16

LN_EPS = 1e-5
FOX_HEAD_DIM = 128
ML_HEADS = 4
ML_QK_DIM = 256
ML_V_DIM = 512
TOP_K = 2
LANES = 128
BF16_SUBLANES = 16
GATE_LANES = LANES
LOG2_E = 1.4426950408889634

V7X_VMEM_BYTES = 64 * 1024 * 1024
VMEM_LIMIT_BYTES = V7X_VMEM_BYTES - 8 * 1024 * 1024


def _params(*sem):
    return pltpu.CompilerParams(dimension_semantics=sem, vmem_limit_bytes=VMEM_LIMIT_BYTES)


def _tile(dim, pref):
    t = min(dim, pref)
    while dim % t:
        t //= 2
    return t


def _layer_norm(x, g, b):
    mu = jnp.mean(x, axis=-1, keepdims=True)
    xc = x - mu
    var = jnp.mean(xc * xc, axis=-1, keepdims=True)
    return xc * lax.rsqrt(var + LN_EPS) * g + b


def _log_sigmoid(x):
    return jnp.minimum(x, 0.0) - jnp.log1p(jnp.exp(-jnp.abs(x)))


def _ln_in_kernel(x_ref, g_ref, b_ref, o32_ref, o16_ref):
    y = _layer_norm(x_ref[...], g_ref[...], b_ref[...])
    o32_ref[...] = y
    o16_ref[...] = y.astype(BF16)


def _ln_in(x, g, b):
    n, d = x.shape
    tr = _tile(n, 256)
    row = pl.BlockSpec((tr, d), lambda i: (i, 0))
    vec = pl.BlockSpec((1, d), lambda i: (0, 0))
    return pl.pallas_call(
        _ln_in_kernel,
        grid=(n // tr,),
        in_specs=[row, vec, vec],
        out_specs=[row, row],
        out_shape=[jax.ShapeDtypeStruct((n, d), F32), jax.ShapeDtypeStruct((n, d), BF16)],
        compiler_params=_params("parallel"),
        name="ln_in",
    )(x, g.reshape(1, d), b.reshape(1, d))


def _ln_res_kernel(h_ref, y_ref, g_ref, b_ref, o_ref, *, alpha):
    o_ref[...] = _layer_norm(alpha * h_ref[...] + y_ref[...], g_ref[...], b_ref[...])


def _ln_res(h, y, g, b, alpha):
    n, d = h.shape
    tr = _tile(n, 256)
    row = pl.BlockSpec((tr, d), lambda i: (i, 0))
    vec = pl.BlockSpec((1, d), lambda i: (0, 0))
    return pl.pallas_call(
        functools.partial(_ln_res_kernel, alpha=alpha),
        grid=(n // tr,),
        in_specs=[row, row, vec, vec],
        out_specs=row,
        out_shape=jax.ShapeDtypeStruct((n, d), F32),
        compiler_params=_params("parallel"),
        name="ln_mix",
    )(h, y, g.reshape(1, d), b.reshape(1, d))


def _mm_kernel(x_ref, w_ref, o_ref):
    o_ref[...] = jnp.dot(x_ref[...], w_ref[...], preferred_element_type=F32).astype(o_ref.dtype)


def _matmul(x, w, out_dtype, name, bm=1024, bn=1024):
    m, k = x.shape
    _, n = w.shape
    bm, bn = _tile(m, bm), _tile(n, bn)
    return pl.pallas_call(
        _mm_kernel,
        grid=(m // bm, n // bn),
        in_specs=[pl.BlockSpec((bm, k), lambda i, j: (i, 0)),
                  pl.BlockSpec((k, bn), lambda i, j: (0, j))],
        out_specs=pl.BlockSpec((bm, bn), lambda i, j: (i, j)),
        out_shape=jax.ShapeDtypeStruct((m, n), out_dtype),
        compiler_params=_params("parallel", "parallel"),
        name=name,
    )(x, w)


IN_PROJ_BN = 512
IN_PROJ_ROWS = 512


def _in_proj_kernel(*refs, off):
    if off:
        x_ref, w_ref, wx_ref, o_ref, w16_ref = refs
    else:
        x_ref, w_ref, o_ref, w16_ref = refs
    k, bn = w16_ref.shape

    @pl.when(pl.program_id(1) == 0)
    def _():
        def convert(r, c):
            rows = pl.ds(pl.multiple_of(r * IN_PROJ_ROWS, IN_PROJ_ROWS), IN_PROJ_ROWS)
            w = w_ref[rows, :]
            if off:
                w = jnp.concatenate([w, wx_ref[rows, :]], axis=1)[:, off:off + bn]
            w16_ref[rows, :] = w.astype(BF16)
            return c
        lax.fori_loop(0, k // IN_PROJ_ROWS, convert, 0)

    o_ref[...] = jnp.dot(x_ref[...], w16_ref[...], preferred_element_type=F32).astype(o_ref.dtype)


def _in_proj(x, w, col_start, n_cols, out_dtype, name):
    m, k = x.shape
    bm, bn = _tile(m, 1024), IN_PROJ_BN
    assert n_cols % bn == 0 and k % IN_PROJ_ROWS == 0
    aligned = col_start // LANES * LANES
    off = col_start - aligned
    assert aligned % bn == 0
    j0 = aligned // bn
    in_specs = [pl.BlockSpec((bm, k), lambda j, i: (i, 0)),
                pl.BlockSpec((k, bn), lambda j, i: (0, j0 + j))]
    args = [x, w]
    if off:
        in_specs.append(pl.BlockSpec((k, LANES), lambda j, i: (0, (j0 + j + 1) * (bn // LANES))))
        args.append(w)
    return pl.pallas_call(
        functools.partial(_in_proj_kernel, off=off),
        grid=(n_cols // bn, m // bm),
        in_specs=in_specs,
        out_specs=pl.BlockSpec((bm, bn), lambda j, i: (i, j)),
        out_shape=jax.ShapeDtypeStruct((m, n_cols), out_dtype),
        scratch_shapes=[pltpu.VMEM((k, bn), BF16)],
        compiler_params=_params("parallel", "arbitrary"),
        name=name,
    )(*args)


def _gates_kernel(g_ref, bias_ref, o_ref, carry_ref, *, n_fox, fox_inv_scale):
    @pl.when(pl.program_id(1) == 0)
    def _():
        carry_ref[...] = jnp.zeros_like(carry_ref)

    ts = g_ref.shape[0]
    x = g_ref[...] + bias_ref[...]
    lane = lax.broadcasted_iota(jnp.int32, x.shape, 1)
    is_input_gate = (lane >= n_fox) & (lane < n_fox + ML_HEADS)
    val = jnp.where(is_input_gate, 0.0, _log_sigmoid(x))
    tril = (lax.broadcasted_iota(jnp.int32, (ts, ts), 0)
            >= lax.broadcasted_iota(jnp.int32, (ts, ts), 1)).astype(BF16)
    hi = val.astype(BF16)
    rem = val - hi.astype(F32)
    mid = rem.astype(BF16)
    lo = (rem - mid.astype(F32)).astype(BF16)
    cs = (jnp.dot(tril, hi, preferred_element_type=F32)
          + jnp.dot(tril, mid, preferred_element_type=F32)
          + jnp.dot(tril, lo, preferred_element_type=F32))
    total = cs + jnp.where(lane < n_fox, carry_ref[...], 0.0)
    carry_ref[...] = total[ts - 1:ts, :]
    total = jnp.where(lane < n_fox, total * fox_inv_scale, total)
    o_ref[...] = jnp.where(is_input_gate, x, total)


def _gates(g, bias, batch, ts, n_fox):
    n, lanes = g.shape
    tiles = n // batch // ts
    blk = pl.BlockSpec((ts, lanes), lambda b, j: (b * tiles + j, 0))
    return pl.pallas_call(
        functools.partial(_gates_kernel, n_fox=n_fox, fox_inv_scale=FOX_HEAD_DIM ** 0.5),
        grid=(batch, tiles),
        in_specs=[blk, pl.BlockSpec((1, lanes), lambda b, j: (0, 0))],
        out_specs=blk,
        out_shape=jax.ShapeDtypeStruct((n, lanes), F32),
        scratch_shapes=[pltpu.VMEM((1, lanes), F32)],
        compiler_params=_params("parallel", "arbitrary"),
        name="gates",
    )(g, bias)


def _conv_kernel(x_ref, prev_ref, w_ref, b_ref, o_ref, *, tiles_per_seq, k_blocks_from, k_scale):
    ts = x_ref.shape[0]
    kw = w_ref.shape[0]
    halo = prev_ref.shape[0]
    first = pl.program_id(0) % tiles_per_seq == 0
    prev = jnp.where(first, 0.0, prev_ref[...])
    ext = jnp.concatenate([prev, x_ref[...]], axis=0)
    w = w_ref[...]
    acc = None
    for j in range(kw):
        off = halo - (kw - 1) + j
        term = w[j:j + 1, :] * ext[off:off + ts, :]
        acc = term if acc is None else acc + term
    acc = acc + b_ref[...]
    y = acc * jax.nn.sigmoid(acc)
    scale = jnp.where(pl.program_id(1) >= k_blocks_from, k_scale, 1.0)
    o_ref[...] = y * scale


def _conv_silu(proj, conv_w, conv_b, seq, width):
    n = proj.shape[0]
    kw = conv_w.shape[0]
    ts, tc, halo = _tile(seq, 512), _tile(width // 2, 512), 8
    assert kw - 1 <= halo
    tiles_per_seq = seq // ts
    return pl.pallas_call(
        functools.partial(_conv_kernel, tiles_per_seq=tiles_per_seq,
                          k_blocks_from=(width // 2) // tc, k_scale=ML_QK_DIM ** -0.5),
        grid=(n // ts, width // tc),
        in_specs=[pl.BlockSpec((ts, tc), lambda i, j: (i, j)),
                  pl.BlockSpec((halo, tc), lambda i, j: (jnp.maximum(i * (ts // halo) - 1, 0), j)),
                  pl.BlockSpec((kw, tc), lambda i, j: (0, j)),
                  pl.BlockSpec((1, tc), lambda i, j: (0, j))],
        out_specs=pl.BlockSpec((ts, tc), lambda i, j: (i, j)),
        out_shape=jax.ShapeDtypeStruct((n, width), F32),
        compiler_params=_params("parallel", "parallel"),
        name="conv_silu",
    )(proj, proj, conv_w, conv_b.reshape(1, width))


def _fox_kernel(*refs, scale, n_cast):
    q_ref, k_ref, vt_ref, cq_ref, ck_ref = refs[:5]
    cast_in = refs[5:5 + n_cast]
    o_ref = refs[5 + n_cast]
    cast_out = refs[6 + n_cast:6 + 2 * n_cast]
    ua_ref, ub_ref, mxa_ref, mxb_ref, m_ref, l_ref, acc_ref = refs[6 + 2 * n_cast:]
    for src, dst in zip(cast_in, cast_out):
        dst[...] = src[...].astype(dst.dtype)

    tq, dh = q_ref.shape
    tk = tq
    qi = pl.program_id(2)
    c_exp = scale * LOG2_E
    q = q_ref[...]
    cq = cq_ref[...]
    buf_a, buf_b = (ua_ref, mxa_ref), (ub_ref, mxb_ref)

    def score(kb, buf, masked):
        u_ref, mx_ref = buf
        ks = pl.multiple_of(kb * tk, tk)
        u = lax.dot_general(k_ref[pl.ds(ks, tk), :], q, (((1,), (1,)), ((), ())),
                            preferred_element_type=F32) - ck_ref[pl.ds(ks, tk), :]
        if masked:
            u = jnp.where(lax.broadcasted_iota(jnp.int32, (tk, tq), 0)
                          <= lax.broadcasted_iota(jnp.int32, (tk, tq), 1), u, -jnp.inf)
        u_ref[...] = u
        mx_ref[...] = jnp.max(u, axis=0, keepdims=True)

    def absorb(kb, buf):
        u_ref, mx_ref = buf
        ks = pl.multiple_of(kb * tk, tk)
        m = m_ref[...]
        m_new = jnp.maximum(m, mx_ref[...] + cq)
        p = jnp.exp2((u_ref[...] + (cq - m_new)) * c_exp)
        a = jnp.exp2((m - m_new) * c_exp)
        l_ref[...] = a * l_ref[...] + jnp.sum(p, axis=0, keepdims=True)
        acc_ref[...] = a * acc_ref[...] + jnp.dot(vt_ref[:, pl.ds(ks, tk)], p.astype(BF16),
                                                  preferred_element_type=F32)
        m_ref[...] = m_new

    m_ref[...] = jnp.full_like(m_ref, -jnp.inf)
    l_ref[...] = jnp.zeros_like(l_ref)
    acc_ref[...] = jnp.zeros_like(acc_ref)

    @pl.when(qi == 0)
    def _():
        score(0, buf_a, masked=True)
        absorb(0, buf_a)

    @pl.when(qi > 0)
    def _():
        score(0, buf_a, masked=False)
        n_pairs = (qi - 1) // 2

        def pair(g, c):
            score(2 * g + 1, buf_b, masked=False)
            absorb(2 * g, buf_a)
            score(2 * g + 2, buf_a, masked=False)
            absorb(2 * g + 1, buf_b)
            return c

        lax.fori_loop(0, n_pairs, pair, 0)
        base = 2 * n_pairs

        @pl.when(qi % 2 == 1)
        def _():
            score(qi, buf_b, masked=True)
            absorb(base, buf_a)
            absorb(qi, buf_b)

        @pl.when(qi % 2 == 0)
        def _():
            score(base + 1, buf_b, masked=False)
            absorb(base, buf_a)
            score(qi, buf_a, masked=True)
            absorb(base + 1, buf_b)
            absorb(qi, buf_a)

    o_ref[...] = jnp.transpose(acc_ref[...] / l_ref[...]).astype(o_ref.dtype)


def _fox_attention(qk, vt, c_rows, c_cols, batch, seq, heads, col_q, col_k, to_cast):
    n = qk.shape[0]
    dh = FOX_HEAD_DIM
    tq = c_rows.shape[-1]
    nq = seq // tq
    steps = batch * heads * nq
    slab = lambda a: pl.BlockSpec((a.shape[0] // steps, a.shape[1]),
                                  lambda b, h, i: ((b * heads + h) * nq + i, 0))
    outs = pl.pallas_call(
        functools.partial(_fox_kernel, scale=dh ** -0.5, n_cast=len(to_cast)),
        grid=(batch, heads, nq),
        in_specs=[pl.BlockSpec((tq, dh), lambda b, h, i: (b * nq + i, col_q + h)),
                  pl.BlockSpec((seq, dh), lambda b, h, i: (b, col_k + h)),
                  pl.BlockSpec((None, None, dh, seq), lambda b, h, i: (b, h, 0, 0)),
                  pl.BlockSpec((None, None, None, 1, tq), lambda b, h, i: (b, h, i, 0, 0)),
                  pl.BlockSpec((None, None, seq, 1), lambda b, h, i: (b, h, 0, 0))]
                 + [slab(a) for a in to_cast],
        out_specs=[pl.BlockSpec((tq, dh), lambda b, h, i: (b * nq + i, h))] + [slab(a) for a in to_cast],
        out_shape=[jax.ShapeDtypeStruct((n, heads * dh), BF16)]
                  + [jax.ShapeDtypeStruct(a.shape, BF16) for a in to_cast],
        scratch_shapes=[pltpu.VMEM((tq, tq), F32), pltpu.VMEM((tq, tq), F32),
                        pltpu.VMEM((1, tq), F32), pltpu.VMEM((1, tq), F32),
                        pltpu.VMEM((1, tq), F32), pltpu.VMEM((1, tq), F32),
                        pltpu.VMEM((dh, tq), F32)],
        compiler_params=_params("parallel", "parallel", "arbitrary"),
        name="fox_attention",
    )(qk, qk, vt, c_rows, c_cols, *to_cast)
    return outs[0], outs[1:]


def _cast_plan(arrays, steps):
    views = [a.reshape(-1, a.shape[-1]) for a in arrays]
    ok = all(v.shape[0] % steps == 0 and (v.shape[0] // steps) % BF16_SUBLANES == 0 for v in views)
    return views if ok else None


def _mlstm_kernel(q_ref, k_ref, v_ref, gate_ref, brow_ref, irow_ref, mo_ref, g_ref, o_ref,
                  c_ref, n_ref, m_ref, *, lane_i, lane_b):
    L = q_ref.shape[0]
    h = pl.program_id(1)

    @pl.when(pl.program_id(2) == 0)
    def _():
        c_ref[...] = jnp.zeros_like(c_ref)
        n_ref[...] = jnp.zeros_like(n_ref)
        m_ref[...] = jnp.zeros_like(m_ref)

    gates = gate_ref[...]
    lane = lax.broadcasted_iota(jnp.int32, gates.shape, 1)
    pick = lambda col: jnp.sum(jnp.where(lane == col + h, gates, 0.0), axis=-1, keepdims=True)
    icol = pick(lane_i)
    bcol = pick(lane_b)
    brow = brow_ref[...]
    irow = irow_ref[...]
    m_prev = m_ref[0:1, 0:1]

    q = q_ref[...].astype(BF16)
    k32 = k_ref[...]
    v = v_ref[...]

    tri = lax.broadcasted_iota(jnp.int32, (L, L), 0) >= lax.broadcasted_iota(jnp.int32, (L, L), 1)
    d = jnp.where(tri, bcol - brow + irow, -jnp.inf)
    inter = bcol + m_prev
    m = jnp.maximum(inter, jnp.max(d, axis=-1, keepdims=True))
    w_intra = jnp.exp(d - m)
    w_inter = jnp.exp(inter - m)
    s = lax.dot_general(q, k32.astype(BF16), (((1,), (1,)), ((), ())), preferred_element_type=F32)
    sc = s * w_intra
    num = (jnp.dot(sc.astype(BF16), v, preferred_element_type=F32)
           + w_inter * jnp.dot(q, c_ref[...].astype(BF16), preferred_element_type=F32))
    qn = jnp.sum(q_ref[...] * n_ref[...], axis=-1, keepdims=True)
    den = jnp.sum(sc, axis=-1, keepdims=True) + w_inter * qn
    hid = num / jnp.maximum(jnp.abs(den), jnp.exp(-m))

    b_last = bcol[L - 1:L, :]
    gcol = b_last - bcol + icol
    m_new = jnp.maximum(b_last + m_prev, jnp.max(gcol, axis=0, keepdims=True))
    w_k = jnp.exp(gcol - m_new)
    decay = jnp.exp(b_last + m_prev - m_new)
    kw = k32 * w_k
    c_ref[...] = decay * c_ref[...] + lax.dot_general(
        kw.astype(BF16), v, (((0,), (0,)), ((), ())), preferred_element_type=F32)
    n_ref[...] = decay * n_ref[...] + jnp.sum(kw, axis=0, keepdims=True)
    m_ref[...] = jnp.broadcast_to(m_new, m_ref.shape)

    mu = jnp.mean(hid, axis=-1, keepdims=True)
    hc = hid - mu
    var = jnp.mean(hc * hc, axis=-1, keepdims=True)
    hn = hc * lax.rsqrt(var + LN_EPS)
    o_ref[...] = (hn * g_ref[...] * jax.nn.sigmoid(mo_ref[...])).astype(o_ref.dtype)


def _mlstm(qk, qkv, proj, gates, brow, irow, norm_g, batch, seq, chunk, col_v, col_mo, lane_i, lane_b):
    n = qk.shape[0]
    dk, dv = ML_QK_DIM, ML_V_DIM
    nc = seq // chunk
    row = lambda b, h, c: b * nc + c
    vec = pl.BlockSpec((None, None, None, 1, chunk), lambda b, h, c: (b, h, c, 0, 0))
    return pl.pallas_call(
        functools.partial(_mlstm_kernel, lane_i=lane_i, lane_b=lane_b),
        grid=(batch, ML_HEADS, nc),
        in_specs=[pl.BlockSpec((chunk, dk), lambda b, h, c: (row(b, h, c), h)),
                  pl.BlockSpec((chunk, dk), lambda b, h, c: (row(b, h, c), ML_HEADS + h)),
                  pl.BlockSpec((chunk, dv), lambda b, h, c: (row(b, h, c), col_v + h)),
                  pl.BlockSpec((chunk, gates.shape[1]), lambda b, h, c: (row(b, h, c), 0)),
                  vec, vec,
                  pl.BlockSpec((chunk, dv), lambda b, h, c: (row(b, h, c), col_mo + h)),
                  pl.BlockSpec((1, dv), lambda b, h, c: (0, h))],
        out_specs=pl.BlockSpec((chunk, dv), lambda b, h, c: (row(b, h, c), h)),
        out_shape=jax.ShapeDtypeStruct((n, ML_HEADS * dv), BF16),
        scratch_shapes=[pltpu.VMEM((dk, dv), F32), pltpu.VMEM((1, dk), F32), pltpu.VMEM((8, 128), F32)],
        compiler_params=_params("parallel", "parallel", "arbitrary"),
        name="mlstm",
    )(qk, qk, qkv, gates, brow, irow, proj, norm_g.reshape(1, ML_HEADS * dv))


def _merge_kernel(yf_ref, ym_ref, wf_ref, wm_ref, ga_ref, gb_ref, o_ref):
    a = jnp.dot(yf_ref[...], wf_ref[...], preferred_element_type=F32)
    b = jnp.dot(ym_ref[...], wm_ref[...], preferred_element_type=F32)
    o_ref[...] = (jax.nn.sigmoid(ga_ref[...]) * a + jax.nn.sigmoid(gb_ref[...]) * b).astype(o_ref.dtype)


def _merge(y_fox, y_ml, w_fox, w_ml, proj, col_ga, col_gb):
    n, kf = y_fox.shape
    km = y_ml.shape[1]
    d = w_fox.shape[1]
    bm, bn = _tile(n, 512), _tile(d, 1024)
    ga0, gb0 = col_ga // bn, col_gb // bn
    return pl.pallas_call(
        _merge_kernel,
        grid=(n // bm, d // bn),
        in_specs=[pl.BlockSpec((bm, kf), lambda i, j: (i, 0)),
                  pl.BlockSpec((bm, km), lambda i, j: (i, 0)),
                  pl.BlockSpec((kf, bn), lambda i, j: (0, j)),
                  pl.BlockSpec((km, bn), lambda i, j: (0, j)),
                  pl.BlockSpec((bm, bn), lambda i, j: (i, ga0 + j)),
                  pl.BlockSpec((bm, bn), lambda i, j: (i, gb0 + j))],
        out_specs=pl.BlockSpec((bm, bn), lambda i, j: (i, j)),
        out_shape=jax.ShapeDtypeStruct((n, d), BF16),
        compiler_params=_params("parallel", "parallel"),
        name="merge_proj",
    )(y_fox, y_ml, w_fox, w_ml, proj, proj)


def _router_kernel(x_ref, w_ref, b_ref, oi_ref, of_ref, cnt_ref, carry_ref, *, n_groups, per_group):
    @pl.when(pl.program_id(0) == 0)
    def _():
        carry_ref[...] = jnp.zeros_like(carry_ref)

    tr = x_ref.shape[0]
    n_exp = n_groups * per_group
    logits = jnp.dot(x_ref[...], w_ref[...], preferred_element_type=F32,
                     precision=lax.Precision.HIGHEST) + b_ref[...]
    lane = lax.broadcasted_iota(jnp.int32, logits.shape, 1)
    big = jnp.int32(logits.shape[1])
    first_lane = lambda cond: jnp.min(jnp.where(cond, lane, big), axis=-1, keepdims=True)

    is_group = (lane >= n_exp) & (lane < n_exp + n_groups)
    gl = jnp.where(is_group, logits, -jnp.inf)
    g_max = jnp.max(gl, axis=-1, keepdims=True)
    g_sel = first_lane(gl == g_max) - n_exp
    p_g_sel = 1.0 / jnp.sum(jnp.exp(gl - g_max), axis=-1, keepdims=True)

    in_group = (lane >= g_sel * per_group) & (lane < (g_sel + 1) * per_group)
    el = jnp.where(in_group, logits, -jnp.inf)
    e_max = jnp.max(el, axis=-1, keepdims=True)
    ee = jnp.exp(el - e_max)
    pe = jnp.where(in_group, ee / jnp.sum(ee, axis=-1, keepdims=True), -1.0)
    p1 = jnp.max(pe, axis=-1, keepdims=True)
    e1 = first_lane(pe == p1)
    pe2 = jnp.where(lane == e1, -1.0, pe)
    p2 = jnp.max(pe2, axis=-1, keepdims=True)
    e2 = first_lane(pe2 == p2)
    p_sum = p1 + p2
    w1 = p_g_sel * p1 / p_sum
    w2 = p_g_sel * p2 / p_sum

    onehot = (lane == e1) | (lane == e2)
    strict = (lax.broadcasted_iota(jnp.int32, (tr, tr), 0)
              > lax.broadcasted_iota(jnp.int32, (tr, tr), 1)).astype(BF16)
    before = jnp.dot(strict, onehot.astype(BF16), preferred_element_type=F32) + carry_ref[...]
    r1 = jnp.sum(jnp.where(lane == e1, before, 0.0), axis=-1, keepdims=True).astype(jnp.int32)
    r2 = jnp.sum(jnp.where(lane == e2, before, 0.0), axis=-1, keepdims=True).astype(jnp.int32)
    carry_ref[...] += jnp.sum(onehot.astype(F32), axis=0, keepdims=True)
    cnt_ref[...] = carry_ref[...].astype(jnp.int32)

    oi_ref[...] = jnp.where(lane == 0, e1, jnp.where(lane == 1, e2, jnp.where(lane == 2, r1, r2)))
    of_ref[...] = jnp.where(lane == 0, w1, w2)


def _router(x, w_route, b_route, n_groups, per_group):
    n, d = x.shape
    lanes = w_route.shape[1]
    tr = _tile(n, 256)
    row = pl.BlockSpec((tr, lanes), lambda i: (i, 0))
    vec = pl.BlockSpec((1, lanes), lambda i: (0, 0))
    return pl.pallas_call(
        functools.partial(_router_kernel, n_groups=n_groups, per_group=per_group),
        grid=(n // tr,),
        in_specs=[pl.BlockSpec((tr, d), lambda i: (i, 0)),
                  pl.BlockSpec((d, lanes), lambda i: (0, 0)), vec],
        out_specs=[row, row, vec],
        out_shape=[jax.ShapeDtypeStruct((n, lanes), jnp.int32),
                   jax.ShapeDtypeStruct((n, lanes), F32),
                   jax.ShapeDtypeStruct((1, lanes), jnp.int32)],
        scratch_shapes=[pltpu.VMEM((1, lanes), F32)],
        compiler_params=_params("arbitrary"),
        name="router",
    )(x, w_route, b_route)


def _expert_kernel(blk_e_ref, n_used_ref, tok_ref, tok_next_ref, x_hbm, wt_ref, wg_ref, wu_ref, wd_ref,
                   o_ref, xbuf, sem):
    b = pl.program_id(0)
    tb = xbuf.shape[1]
    n_used = n_used_ref[0]
    slot = b % 2

    def row_copy(ids_ref, r, s):
        return pltpu.make_async_copy(x_hbm.at[pl.ds(ids_ref[r], 1), :],
                                     xbuf.at[s, pl.ds(r, 1), :], sem.at[s])

    def wait_rows():
        pltpu.make_async_copy(x_hbm.at[pl.ds(0, tb), :], xbuf.at[slot], sem.at[slot]).wait()

    def compute():
        xb = xbuf[slot].astype(BF16)
        gate = jnp.dot(xb, wg_ref[...], preferred_element_type=F32)
        up = jnp.dot(xb, wu_ref[...], preferred_element_type=F32)
        act = (gate * jax.nn.sigmoid(gate) * up).astype(BF16)
        o_ref[...] = jnp.dot(act, wd_ref[...], preferred_element_type=F32) * wt_ref[...]

    @pl.when((b == 0) & (n_used > 0))
    def _():
        def start(r, c):
            row_copy(tok_ref, r, slot).start()
            return c
        lax.fori_loop(0, tb, start, 0)

    @pl.when(b + 1 < n_used)
    def _():
        wait_rows()
        for r in range(tb):
            row_copy(tok_next_ref, r, 1 - slot).start()
        compute()

    @pl.when(b + 1 == n_used)
    def _():
        wait_rows()
        compute()

    @pl.when(b >= n_used)
    def _():
        o_ref[...] = jnp.zeros_like(o_ref)


def _experts(x, slot_tok, slot_w, block_e, n_used, w_gate, w_up, w_down, tb):
    n, d = x.shape
    _, _, f = w_gate.shape
    n_blocks = block_e.shape[0]
    grid_spec = pltpu.PrefetchScalarGridSpec(
        num_scalar_prefetch=2,
        grid=(n_blocks,),
        in_specs=[pl.BlockSpec((tb,), lambda b, be, nu: (b,), memory_space=pltpu.SMEM),
                  pl.BlockSpec((tb,), lambda b, be, nu: (jnp.minimum(b + 1, n_blocks - 1),),
                               memory_space=pltpu.SMEM),
                  pl.BlockSpec(memory_space=pl.ANY),
                  pl.BlockSpec((tb, 1), lambda b, be, nu: (b, 0)),
                  pl.BlockSpec((None, d, f), lambda b, be, nu: (be[b], 0, 0)),
                  pl.BlockSpec((None, d, f), lambda b, be, nu: (be[b], 0, 0)),
                  pl.BlockSpec((None, f, d), lambda b, be, nu: (be[b], 0, 0))],
        out_specs=pl.BlockSpec((tb, d), lambda b, be, nu: (b, 0)),
        scratch_shapes=[pltpu.VMEM((2, tb, d), F32), pltpu.SemaphoreType.DMA((2,))],
    )
    return pl.pallas_call(
        _expert_kernel,
        grid_spec=grid_spec,
        out_shape=jax.ShapeDtypeStruct((n_blocks * tb, d), F32),
        compiler_params=_params("arbitrary"),
        name="experts",
    )(block_e, n_used, slot_tok, slot_tok, x, slot_w.reshape(-1, 1), w_gate, w_up, w_down)


def _combine_kernel(dest_ref, dest_next_ref, y_hbm, h_ref, g_ref, b_ref, o_ref, buf, sem, *, alpha):
    i = pl.program_id(0)
    last = pl.num_programs(0) - 1
    tm = h_ref.shape[0]
    slot = i % 2

    def row_copy(ids_ref, r, k, s):
        return pltpu.make_async_copy(y_hbm.at[pl.ds(ids_ref[r * TOP_K + k], 1), :],
                                     buf.at[s, k, pl.ds(r, 1), :], sem.at[s])

    def compute():
        y = buf[slot, 0]
        for k in range(1, TOP_K):
            y = y + buf[slot, k]
        o_ref[...] = _layer_norm(alpha * h_ref[...] + y, g_ref[...], b_ref[...])

    @pl.when(i == 0)
    def _():
        def start(r, c):
            for k in range(TOP_K):
                row_copy(dest_ref, r, k, slot).start()
            return c
        lax.fori_loop(0, tm, start, 0)

    for k in range(TOP_K):
        pltpu.make_async_copy(y_hbm.at[pl.ds(0, tm), :], buf.at[slot, k], sem.at[slot]).wait()

    @pl.when(i < last)
    def _():
        for r in range(tm):
            for k in range(TOP_K):
                row_copy(dest_next_ref, r, k, 1 - slot).start()
        compute()

    @pl.when(i == last)
    def _():
        compute()


def _combine_ln(h, y_slots, dest, g, b, alpha):
    n, d = h.shape
    tm = _tile(n, 256)
    tiles = n // tm
    return pl.pallas_call(
        functools.partial(_combine_kernel, alpha=alpha),
        grid=(tiles,),
        in_specs=[pl.BlockSpec((tm * TOP_K,), lambda i: (i,), memory_space=pltpu.SMEM),
                  pl.BlockSpec((tm * TOP_K,), lambda i: (jnp.minimum(i + 1, tiles - 1),),
                               memory_space=pltpu.SMEM),
                  pl.BlockSpec(memory_space=pl.ANY),
                  pl.BlockSpec((tm, d), lambda i: (i, 0)),
                  pl.BlockSpec((1, d), lambda i: (0, 0)),
                  pl.BlockSpec((1, d), lambda i: (0, 0))],
        out_specs=pl.BlockSpec((tm, d), lambda i: (i, 0)),
        out_shape=jax.ShapeDtypeStruct((n, d), F32),
        scratch_shapes=[pltpu.VMEM((2, TOP_K, tm, d), F32), pltpu.SemaphoreType.DMA((2,))],
        compiler_params=_params("arbitrary"),
        name="combine_ln",
    )(dest, dest, y_slots, h, g.reshape(1, d), b.reshape(1, d))


EXPERT_SLOT_BLOCK = 256
ML_CHUNK = 256
FOX_KEY_BLOCK = 512
FOX_UNROLL = 2


def _mixer(h32, h16, batch, seq, w_in, b_fox_f, b_ml_i, b_ml_f, conv_w, conv_b, ml_norm_g,
           w_proj_fox, w_proj_ml, w_out, expert_w):
    n, d = h32.shape
    fox_w = d // 2
    fox_heads = fox_w // FOX_HEAD_DIM
    qk_w = ML_HEADS * ML_QK_DIM
    v_w = ML_HEADS * ML_V_DIM
    widths = (fox_w, fox_w, fox_w, fox_heads, qk_w, qk_w, v_w, ML_HEADS, ML_HEADS, v_w, d, d)
    offs = [0]
    for w in widths:
        offs.append(offs[-1] + w)
    seg = lambda i: w_in[:, offs[i]:offs[i + 1]]
    ff, mi, mf = seg(3), seg(7), seg(8)

    n_gate = fox_heads + 2 * ML_HEADS
    assert n_gate <= GATE_LANES
    w_c = jnp.pad(jnp.concatenate([ff, mi, mf], axis=1), ((0, 0), (0, GATE_LANES - n_gate))).astype(BF16)
    gate_bias = jnp.pad(jnp.concatenate([b_fox_f, b_ml_i, b_ml_f]), (0, GATE_LANES - n_gate)).reshape(1, -1)

    proj_fox = _in_proj(h16, w_in, offs[0], 3 * fox_w, BF16, "in_proj_fox")
    proj_qk = _in_proj(h16, w_in, offs[4], 2 * qk_w, F32, "in_proj_qk")
    proj_mv = _in_proj(h16, w_in, offs[6], v_w, BF16, "in_proj_mv")
    proj_og = _in_proj(h16, w_in, offs[9], v_w + 2 * d, F32, "in_proj_og")
    gate_pre = _matmul(h16, w_c, F32, "in_proj_gates")

    chunk = _tile(seq, ML_CHUNK)
    gates = _gates(gate_pre, gate_bias, batch, chunk, fox_heads)

    g3 = gates.reshape(batch, seq, GATE_LANES)
    tk = _tile(seq, FOX_KEY_BLOCK)
    c_heads = jnp.transpose(g3[:, :, :fox_heads], (0, 2, 1))
    c_rows = c_heads.reshape(batch, fox_heads, seq // tk, 1, tk)
    c_cols = c_heads.reshape(batch, fox_heads, seq, 1)
    lane_i, lane_b = fox_heads, fox_heads + ML_HEADS
    rows = lambda lo: jnp.transpose(g3[:, :, lo:lo + ML_HEADS], (0, 2, 1)).reshape(
        batch, ML_HEADS, seq // chunk, 1, chunk)
    irow, brow = rows(lane_i), rows(lane_b)

    v_t = jnp.transpose(proj_fox[:, 2 * fox_w:3 * fox_w].reshape(batch, seq, fox_heads, FOX_HEAD_DIM),
                        (0, 2, 3, 1))
    views = _cast_plan(expert_w, batch * fox_heads * (seq // tk))
    y_fox, cast = _fox_attention(proj_fox, v_t, c_rows, c_cols, batch, seq, fox_heads,
                                 col_q=0, col_k=fox_heads, to_cast=views or [])
    if views:
        expert_w16 = [c.reshape(a.shape) for c, a in zip(cast, expert_w)]
    else:
        expert_w16 = [a.astype(BF16) for a in expert_w]
    qk = _conv_silu(proj_qk, conv_w, conv_b, seq, 2 * qk_w)
    y_ml = _mlstm(qk, proj_mv, proj_og, gates, brow, irow, ml_norm_g, batch, seq, chunk,
                  col_v=0, col_mo=0, lane_i=lane_i, lane_b=lane_b)
    merged = _merge(y_fox, y_ml, w_proj_fox.astype(BF16), w_proj_ml.astype(BF16), proj_og,
                    col_ga=v_w, col_gb=v_w + d)
    return _matmul(merged, w_out.astype(BF16), F32, "out_proj"), expert_w16


def _moe(h, w_group, b_group, w_router, b_router, w_gate, w_up, w_down, ln_g, ln_b, alpha):
    n, d = h.shape
    n_groups = w_group.shape[1]
    n_exp = w_router.shape[1]
    lanes = GATE_LANES
    assert n_exp + n_groups <= lanes
    pad = lanes - n_exp - n_groups
    w_route = jnp.pad(jnp.concatenate([w_router, w_group], axis=1), ((0, 0), (0, pad)))
    b_route = jnp.pad(jnp.concatenate([b_router, b_group]), (0, pad)).reshape(1, lanes)
    oi, of, cnt = _router(h, w_route, b_route, n_groups, n_exp // n_groups)
    e_idx, rank, gate_w = oi[:, 0:TOP_K], oi[:, TOP_K:2 * TOP_K], of[:, 0:TOP_K]
    counts = cnt[0, :n_exp]

    tb = EXPERT_SLOT_BLOCK
    n_assign = n * TOP_K
    n_blocks = (n_assign + n_exp * (tb - 1) + tb - 1) // tb
    padded = (counts + tb - 1) // tb * tb
    pad_ends = jnp.cumsum(padded)
    dest = ((pad_ends - padded)[e_idx] + rank).reshape(-1)
    tok = jnp.repeat(jnp.arange(n, dtype=jnp.int32), TOP_K)
    pairs = jnp.stack([tok, lax.bitcast_convert_type(gate_w.reshape(-1), jnp.int32)], axis=1)
    slots = jnp.zeros((n_blocks * tb, 2), jnp.int32).at[dest].set(pairs)
    slot_tok = slots[:, 0]
    slot_w = lax.bitcast_convert_type(slots[:, 1], F32)
    n_used = pad_ends[-1] // tb
    blk = jnp.arange(n_blocks, dtype=jnp.int32)
    block_e = jnp.minimum(jnp.searchsorted(pad_ends, blk * tb, side="right"), n_exp - 1).astype(jnp.int32)
    block_e = jnp.where(blk < n_used, block_e, block_e[jnp.maximum(n_used - 1, 0)])

    y_slots = _experts(h, slot_tok, slot_w, block_e, n_used.reshape(1).astype(jnp.int32),
                       w_gate, w_up, w_down, tb)
    return _combine_ln(h, y_slots, dest.astype(jnp.int32), ln_g, ln_b, alpha)


def kernel(x, ln_in_g, ln_in_b, w_in, b_fox_f, b_ml_i, b_ml_f, conv_w, conv_b, ml_norm_g, w_proj_fox, w_proj_ml, w_out, ln_mix_g, ln_mix_b, w_group, b_group, w_router, b_router, w_gate, w_up, w_down, ln_moe_g, ln_moe_b):
    batch, seq, d = x.shape
    depth = w_in.shape[0]
    alpha = (2 * depth) ** 0.25
    h32, h16 = _ln_in(x.reshape(batch * seq, d), ln_in_g, ln_in_b)
    for l in range(depth):
        mix, expert_w16 = _mixer(h32, h16, batch, seq, w_in[l], b_fox_f[l], b_ml_i[l], b_ml_f[l],
                                 conv_w[l], conv_b[l], ml_norm_g[l], w_proj_fox[l], w_proj_ml[l],
                                 w_out[l], [w_gate[l], w_up[l], w_down[l]])
        h32 = _ln_res(h32, mix, ln_mix_g[l], ln_mix_b[l], alpha)
        h32 = _moe(h32, w_group[l], b_group[l], w_router[l], b_router[l], *expert_w16,
                   ln_moe_g[l], ln_moe_b[l], alpha)
        if l + 1 < depth:
            h16 = h32.astype(BF16)
    return h32.reshape(batch, seq, d)
```

```python
import functools

import jax
import jax.numpy as jnp
from jax import lax
from jax.experimental import pallas as pl
from jax.experimental.pallas import tpu as pltpu

F32 = jnp.float32
BF16 = jnp.bfloat16

LN_EPS = 1e-5
FOX_HEAD_DIM = 128
ML_HEADS = 4
ML_QK_DIM = 256
ML_V_DIM = 512
TOP_K = 2
LANES = 128
F32_SUBLANES = 8
BF16_SUBLANES = 16
GATE_LANES = LANES
LOG2_E = 1.4426950408889634

V7X_VMEM_BYTES = 64 * 1024 * 1024
VMEM_LIMIT_BYTES = V7X_VMEM_BYTES - 8 * 1024 * 1024


def _params(*sem):
    return pltpu.CompilerParams(dimension_semantics=sem, vmem_limit_bytes=VMEM_LIMIT_BYTES)


def _tile(dim, pref):
    t = min(dim, pref)
    while dim % t:
        t //= 2
    return t


def _layer_norm(x, g, b):
    mu = jnp.mean(x, axis=-1, keepdims=True)
    xc = x - mu
    var = jnp.mean(xc * xc, axis=-1, keepdims=True)
    return xc * lax.rsqrt(var + LN_EPS) * g + b


def _log_sigmoid(x):
    return jnp.minimum(x, 0.0) - jnp.log1p(jnp.exp(-jnp.abs(x)))


def _ln_in_kernel(x_ref, g_ref, b_ref, o32_ref, o16_ref):
    y = _layer_norm(x_ref[...], g_ref[...], b_ref[...])
    o32_ref[...] = y
    o16_ref[...] = y.astype(BF16)


def _ln_in(x, g, b):
    n, d = x.shape
    tr = _tile(n, 256)
    row = pl.BlockSpec((tr, d), lambda i: (i, 0))
    vec = pl.BlockSpec((1, d), lambda i: (0, 0))
    return pl.pallas_call(
        _ln_in_kernel,
        grid=(n // tr,),
        in_specs=[row, vec, vec],
        out_specs=[row, row],
        out_shape=[jax.ShapeDtypeStruct((n, d), F32), jax.ShapeDtypeStruct((n, d), BF16)],
        compiler_params=_params("parallel"),
        name="ln_in",
    )(x, g.reshape(1, d), b.reshape(1, d))


def _ln_res_kernel(h_ref, y_ref, g_ref, b_ref, o_ref, *, alpha):
    o_ref[...] = _layer_norm(alpha * h_ref[...] + y_ref[...], g_ref[...], b_ref[...])


def _ln_res(h, y, g, b, alpha):
    n, d = h.shape
    tr = _tile(n, 256)
    row = pl.BlockSpec((tr, d), lambda i: (i, 0))
    vec = pl.BlockSpec((1, d), lambda i: (0, 0))
    return pl.pallas_call(
        functools.partial(_ln_res_kernel, alpha=alpha),
        grid=(n // tr,),
        in_specs=[row, row, vec, vec],
        out_specs=row,
        out_shape=jax.ShapeDtypeStruct((n, d), F32),
        compiler_params=_params("parallel"),
        name="ln_mix",
    )(h, y, g.reshape(1, d), b.reshape(1, d))


def _mm_kernel(x_ref, w_ref, o_ref):
    o_ref[...] = jnp.dot(x_ref[...], w_ref[...], preferred_element_type=F32).astype(o_ref.dtype)


def _matmul(x, w, out_dtype, name, bm=1024, bn=1024):
    m, k = x.shape
    _, n = w.shape
    bm, bn = _tile(m, bm), _tile(n, bn)
    return pl.pallas_call(
        _mm_kernel,
        grid=(m // bm, n // bn),
        in_specs=[pl.BlockSpec((bm, k), lambda i, j: (i, 0)),
                  pl.BlockSpec((k, bn), lambda i, j: (0, j))],
        out_specs=pl.BlockSpec((bm, bn), lambda i, j: (i, j)),
        out_shape=jax.ShapeDtypeStruct((m, n), out_dtype),
        compiler_params=_params("parallel", "parallel"),
        name=name,
    )(x, w)


_NT = (((1,), (1,)), ((), ()))


def _in_proj_kernel(*refs, off, n_col_blocks):
    if off:
        x_ref, w_ref, wx_ref, o_ref, w16_ref = refs
    else:
        x_ref, w_ref, o_ref, w16_ref = refs
    rc = w_ref.shape[0]
    j, i = pl.program_id(0), pl.program_id(1)

    @pl.when(j < n_col_blocks)
    def _():
        w = w_ref[...]
        if off:
            w = jnp.concatenate([w, wx_ref[...]], axis=0)[off:off + rc, :]
        w16_ref[j % 2, pl.ds(pl.multiple_of(i * rc, rc), rc), :] = w.astype(BF16)

    @pl.when(j > 0)
    def _():
        o_ref[...] = lax.dot_general(x_ref[...], w16_ref[(j - 1) % 2], _NT,
                                     preferred_element_type=F32).astype(o_ref.dtype)


def _in_proj(x, wt, col_start, n_cols, out_dtype, name):
    m, k = x.shape
    bm, bn = _tile(m, 1024), _tile(n_cols, 1024)
    ni, nj = m // bm, n_cols // bn
    rc = bn // ni
    assert bn % ni == 0 and rc % BF16_SUBLANES == 0
    off = col_start % rc
    assert off % F32_SUBLANES == 0
    blk0 = (col_start - off) // rc
    chunk = lambda j, i: blk0 + jnp.minimum(j, nj - 1) * ni + i
    row = lambda j, i: jnp.where(j == 0, 0, i)
    in_specs = [pl.BlockSpec((bm, k), lambda j, i: (row(j, i), 0)),
                pl.BlockSpec((rc, k), lambda j, i: (chunk(j, i), 0))]
    args = [x, wt]
    if off:
        in_specs.append(pl.BlockSpec((rc, k), lambda j, i: (chunk(j, i) + 1, 0)))
        args.append(wt)
    return pl.pallas_call(
        functools.partial(_in_proj_kernel, off=off, n_col_blocks=nj),
        grid=(nj + 1, ni),
        in_specs=in_specs,
        out_specs=pl.BlockSpec((bm, bn), lambda j, i: (row(j, i), jnp.maximum(j - 1, 0))),
        out_shape=jax.ShapeDtypeStruct((m, n_cols), out_dtype),
        scratch_shapes=[pltpu.VMEM((2, bn, k), BF16)],
        compiler_params=_params("arbitrary", "arbitrary"),
        name=name,
    )(*args)


def _gate_proj_kernel(x_ref, wa_ref, wb_ref, o_ref, w16_ref):
    @pl.when(pl.program_id(0) == 0)
    def _():
        pad = w16_ref.shape[0] - wa_ref.shape[0] - wb_ref.shape[0]
        zeros = jnp.zeros((pad, w16_ref.shape[1]), F32)
        w16_ref[...] = jnp.concatenate([wa_ref[...], wb_ref[...], zeros], axis=0).astype(BF16)

    o_ref[...] = lax.dot_general(x_ref[...], w16_ref[...], _NT, preferred_element_type=F32)


def _gate_proj(x, wt, col_ff, col_mi, n_fox):
    m, k = x.shape
    n_ml = 2 * ML_HEADS
    assert n_fox % F32_SUBLANES == 0 and n_ml % F32_SUBLANES == 0 and n_fox + n_ml <= LANES
    assert col_ff % n_fox == 0 and col_mi % n_ml == 0
    bm = _tile(m, 1024)
    return pl.pallas_call(
        _gate_proj_kernel,
        grid=(m // bm,),
        in_specs=[pl.BlockSpec((bm, k), lambda i: (i, 0)),
                  pl.BlockSpec((n_fox, k), lambda i: (col_ff // n_fox, 0)),
                  pl.BlockSpec((n_ml, k), lambda i: (col_mi // n_ml, 0))],
        out_specs=pl.BlockSpec((bm, LANES), lambda i: (i, 0)),
        out_shape=jax.ShapeDtypeStruct((m, LANES), F32),
        scratch_shapes=[pltpu.VMEM((LANES, k), BF16)],
        compiler_params=_params("arbitrary"),
        name="in_proj_gates",
    )(x, wt, wt)


def _gates_kernel(g_ref, bias_ref, o_ref, carry_ref, *, n_fox, fox_inv_scale):
    @pl.when(pl.program_id(1) == 0)
    def _():
        carry_ref[...] = jnp.zeros_like(carry_ref)

    ts = g_ref.shape[0]
    x = g_ref[...] + bias_ref[...]
    lane = lax.broadcasted_iota(jnp.int32, x.shape, 1)
    is_input_gate = (lane >= n_fox) & (lane < n_fox + ML_HEADS)
    val = jnp.where(is_input_gate, 0.0, _log_sigmoid(x))
    tril = (lax.broadcasted_iota(jnp.int32, (ts, ts), 0)
            >= lax.broadcasted_iota(jnp.int32, (ts, ts), 1)).astype(BF16)
    hi = val.astype(BF16)
    rem = val - hi.astype(F32)
    mid = rem.astype(BF16)
    lo = (rem - mid.astype(F32)).astype(BF16)
    cs = (jnp.dot(tril, hi, preferred_element_type=F32)
          + jnp.dot(tril, mid, preferred_element_type=F32)
          + jnp.dot(tril, lo, preferred_element_type=F32))
    total = cs + jnp.where(lane < n_fox, carry_ref[...], 0.0)
    carry_ref[...] = total[ts - 1:ts, :]
    total = jnp.where(lane < n_fox, total * fox_inv_scale, total)
    o_ref[...] = jnp.where(is_input_gate, x, total)


def _gates(g, bias, batch, ts, n_fox):
    n, lanes = g.shape
    tiles = n // batch // ts
    blk = pl.BlockSpec((ts, lanes), lambda b, j: (b * tiles + j, 0))
    return pl.pallas_call(
        functools.partial(_gates_kernel, n_fox=n_fox, fox_inv_scale=FOX_HEAD_DIM ** 0.5),
        grid=(batch, tiles),
        in_specs=[blk, pl.BlockSpec((1, lanes), lambda b, j: (0, 0))],
        out_specs=blk,
        out_shape=jax.ShapeDtypeStruct((n, lanes), F32),
        scratch_shapes=[pltpu.VMEM((1, lanes), F32)],
        compiler_params=_params("parallel", "arbitrary"),
        name="gates",
    )(g, bias)


def _conv_kernel(x_ref, prev_ref, w_ref, b_ref, o_ref, *, tiles_per_seq, k_blocks_from, k_scale):
    ts = x_ref.shape[0]
    kw = w_ref.shape[0]
    halo = prev_ref.shape[0]
    first = pl.program_id(0) % tiles_per_seq == 0
    prev = jnp.where(first, 0.0, prev_ref[...])
    ext = jnp.concatenate([prev, x_ref[...]], axis=0)
    w = w_ref[...]
    acc = None
    for j in range(kw):
        off = halo - (kw - 1) + j
        term = w[j:j + 1, :] * ext[off:off + ts, :]
        acc = term if acc is None else acc + term
    acc = acc + b_ref[...]
    y = acc * jax.nn.sigmoid(acc)
    scale = jnp.where(pl.program_id(1) >= k_blocks_from, k_scale, 1.0)
    o_ref[...] = y * scale


def _conv_silu(proj, conv_w, conv_b, seq, width):
    n = proj.shape[0]
    kw = conv_w.shape[0]
    ts, tc, halo = _tile(seq, 512), _tile(width // 2, 512), 8
    assert kw - 1 <= halo
    tiles_per_seq = seq // ts
    return pl.pallas_call(
        functools.partial(_conv_kernel, tiles_per_seq=tiles_per_seq,
                          k_blocks_from=(width // 2) // tc, k_scale=ML_QK_DIM ** -0.5),
        grid=(n // ts, width // tc),
        in_specs=[pl.BlockSpec((ts, tc), lambda i, j: (i, j)),
                  pl.BlockSpec((halo, tc), lambda i, j: (jnp.maximum(i * (ts // halo) - 1, 0), j)),
                  pl.BlockSpec((kw, tc), lambda i, j: (0, j)),
                  pl.BlockSpec((1, tc), lambda i, j: (0, j))],
        out_specs=pl.BlockSpec((ts, tc), lambda i, j: (i, j)),
        out_shape=jax.ShapeDtypeStruct((n, width), F32),
        compiler_params=_params("parallel", "parallel"),
        name="conv_silu",
    )(proj, proj, conv_w, conv_b.reshape(1, width))


def _fox_kernel(*refs, scale, n_cast):
    q_ref, k_ref, vt_ref, cq_ref, ck_ref = refs[:5]
    cast_in = refs[5:5 + n_cast]
    o_ref = refs[5 + n_cast]
    cast_out = refs[6 + n_cast:6 + 2 * n_cast]
    ua_ref, ub_ref, mxa_ref, mxb_ref, m_ref, l_ref, acc_ref = refs[6 + 2 * n_cast:]
    for src, dst in zip(cast_in, cast_out):
        dst[...] = src[...].astype(dst.dtype)

    tq, dh = q_ref.shape
    tk = tq
    qi = pl.program_id(2)
    c_exp = scale * LOG2_E
    q = q_ref[...]
    cq = cq_ref[...]
    buf_a, buf_b = (ua_ref, mxa_ref), (ub_ref, mxb_ref)

    def score(kb, buf, masked):
        u_ref, mx_ref = buf
        ks = pl.multiple_of(kb * tk, tk)
        u = lax.dot_general(k_ref[pl.ds(ks, tk), :], q, (((1,), (1,)), ((), ())),
                            preferred_element_type=F32) - ck_ref[pl.ds(ks, tk), :]
        if masked:
            u = jnp.where(lax.broadcasted_iota(jnp.int32, (tk, tq), 0)
                          <= lax.broadcasted_iota(jnp.int32, (tk, tq), 1), u, -jnp.inf)
        u_ref[...] = u
        mx_ref[...] = jnp.max(u, axis=0, keepdims=True)

    def absorb(kb, buf):
        u_ref, mx_ref = buf
        ks = pl.multiple_of(kb * tk, tk)
        m = m_ref[...]
        m_new = jnp.maximum(m, mx_ref[...] + cq)
        p = jnp.exp2((u_ref[...] + (cq - m_new)) * c_exp)
        a = jnp.exp2((m - m_new) * c_exp)
        l_ref[...] = a * l_ref[...] + jnp.sum(p, axis=0, keepdims=True)
        acc_ref[...] = a * acc_ref[...] + jnp.dot(vt_ref[:, pl.ds(ks, tk)], p.astype(BF16),
                                                  preferred_element_type=F32)
        m_ref[...] = m_new

    m_ref[...] = jnp.full_like(m_ref, -jnp.inf)
    l_ref[...] = jnp.zeros_like(l_ref)
    acc_ref[...] = jnp.zeros_like(acc_ref)

    @pl.when(qi == 0)
    def _():
        score(0, buf_a, masked=True)
        absorb(0, buf_a)

    @pl.when(qi > 0)
    def _():
        score(0, buf_a, masked=False)
        n_pairs = (qi - 1) // 2

        def pair(g, c):
            score(2 * g + 1, buf_b, masked=False)
            absorb(2 * g, buf_a)
            score(2 * g + 2, buf_a, masked=False)
            absorb(2 * g + 1, buf_b)
            return c

        lax.fori_loop(0, n_pairs, pair, 0)
        base = 2 * n_pairs

        @pl.when(qi % 2 == 1)
        def _():
            score(qi, buf_b, masked=True)
            absorb(base, buf_a)
            absorb(qi, buf_b)

        @pl.when(qi % 2 == 0)
        def _():
            score(base + 1, buf_b, masked=False)
            absorb(base, buf_a)
            score(qi, buf_a, masked=True)
            absorb(base + 1, buf_b)
            absorb(qi, buf_a)

    o_ref[...] = jnp.transpose(acc_ref[...] / l_ref[...]).astype(o_ref.dtype)


def _fox_attention(qk, vt, c_rows, c_cols, batch, seq, heads, col_q, col_k, to_cast):
    n = qk.shape[0]
    dh = FOX_HEAD_DIM
    tq = c_rows.shape[-1]
    nq = seq // tq
    steps = batch * heads * nq

    def slab(a):
        rb = _cast_rows(a.shape[0], steps)
        last = a.shape[0] // rb - 1
        return pl.BlockSpec((rb, a.shape[1]),
                            lambda b, h, i: (jnp.minimum((b * heads + h) * nq + i, last), 0))
    outs = pl.pallas_call(
        functools.partial(_fox_kernel, scale=dh ** -0.5, n_cast=len(to_cast)),
        grid=(batch, heads, nq),
        in_specs=[pl.BlockSpec((tq, dh), lambda b, h, i: (b * nq + i, col_q + h)),
                  pl.BlockSpec((seq, dh), lambda b, h, i: (b, col_k + h)),
                  pl.BlockSpec((None, None, dh, seq), lambda b, h, i: (b, h, 0, 0)),
                  pl.BlockSpec((None, None, None, 1, tq), lambda b, h, i: (b, h, i, 0, 0)),
                  pl.BlockSpec((None, None, seq, 1), lambda b, h, i: (b, h, 0, 0))]
                 + [slab(a) for a in to_cast],
        out_specs=[pl.BlockSpec((tq, dh), lambda b, h, i: (b * nq + i, h))] + [slab(a) for a in to_cast],
        out_shape=[jax.ShapeDtypeStruct((n, heads * dh), BF16)]
                  + [jax.ShapeDtypeStruct(a.shape, BF16) for a in to_cast],
        scratch_shapes=[pltpu.VMEM((tq, tq), F32), pltpu.VMEM((tq, tq), F32),
                        pltpu.VMEM((1, tq), F32), pltpu.VMEM((1, tq), F32),
                        pltpu.VMEM((1, tq), F32), pltpu.VMEM((1, tq), F32),
                        pltpu.VMEM((dh, tq), F32)],
        compiler_params=_params("arbitrary", "arbitrary", "arbitrary"),
        name="fox_attention",
    )(qk, qk, vt, c_rows, c_cols, *to_cast)
    return outs[0], outs[1:]


def _cast_rows(rows, steps):
    rb = max(BF16_SUBLANES, -(-rows // steps))
    return rb if rows % rb == 0 and rb % BF16_SUBLANES == 0 else None


def _cast_plan(arrays, steps):
    views = [a.reshape(-1, a.shape[-1]) for a in arrays]
    return views if all(_cast_rows(v.shape[0], steps) for v in views) else None


def _mlstm_kernel(q_ref, k_ref, v_ref, gate_ref, brow_ref, irow_ref, mo_ref, g_ref, o_ref,
                  c_ref, n_ref, m_ref, *, lane_i, lane_b):
    L = q_ref.shape[0]
    h = pl.program_id(1)

    @pl.when(pl.program_id(2) == 0)
    def _():
        c_ref[...] = jnp.zeros_like(c_ref)
        n_ref[...] = jnp.zeros_like(n_ref)
        m_ref[...] = jnp.zeros_like(m_ref)

    gates = gate_ref[...]
    lane = lax.broadcasted_iota(jnp.int32, gates.shape, 1)
    pick = lambda col: jnp.sum(jnp.where(lane == col + h, gates, 0.0), axis=-1, keepdims=True)
    icol = pick(lane_i)
    bcol = pick(lane_b)
    brow = brow_ref[...]
    irow = irow_ref[...]
    m_prev = m_ref[0:1, 0:1]

    q = q_ref[...].astype(BF16)
    k32 = k_ref[...]
    v = v_ref[...]

    tri = lax.broadcasted_iota(jnp.int32, (L, L), 0) >= lax.broadcasted_iota(jnp.int32, (L, L), 1)
    d = jnp.where(tri, bcol - brow + irow, -jnp.inf)
    inter = bcol + m_prev
    m = jnp.maximum(inter, jnp.max(d, axis=-1, keepdims=True))
    w_intra = jnp.exp(d - m)
    w_inter = jnp.exp(inter - m)
    s = lax.dot_general(q, k32.astype(BF16), (((1,), (1,)), ((), ())), preferred_element_type=F32)
    sc = s * w_intra
    num = (jnp.dot(sc.astype(BF16), v, preferred_element_type=F32)
           + w_inter * jnp.dot(q, c_ref[...].astype(BF16), preferred_element_type=F32))
    qn = jnp.sum(q_ref[...] * n_ref[...], axis=-1, keepdims=True)
    den = jnp.sum(sc, axis=-1, keepdims=True) + w_inter * qn
    hid = num / jnp.maximum(jnp.abs(den), jnp.exp(-m))

    b_last = bcol[L - 1:L, :]
    gcol = b_last - bcol + icol
    m_new = jnp.maximum(b_last + m_prev, jnp.max(gcol, axis=0, keepdims=True))
    w_k = jnp.exp(gcol - m_new)
    decay = jnp.exp(b_last + m_prev - m_new)
    kw = k32 * w_k
    c_ref[...] = decay * c_ref[...] + lax.dot_general(
        kw.astype(BF16), v, (((0,), (0,)), ((), ())), preferred_element_type=F32)
    n_ref[...] = decay * n_ref[...] + jnp.sum(kw, axis=0, keepdims=True)
    m_ref[...] = jnp.broadcast_to(m_new, m_ref.shape)

    mu = jnp.mean(hid, axis=-1, keepdims=True)
    hc = hid - mu
    var = jnp.mean(hc * hc, axis=-1, keepdims=True)
    hn = hc * lax.rsqrt(var + LN_EPS)
    o_ref[...] = (hn * g_ref[...] * jax.nn.sigmoid(mo_ref[...])).astype(o_ref.dtype)


def _mlstm(qk, qkv, proj, gates, brow, irow, norm_g, batch, seq, chunk, col_v, col_mo, lane_i, lane_b):
    n = qk.shape[0]
    dk, dv = ML_QK_DIM, ML_V_DIM
    nc = seq // chunk
    row = lambda b, h, c: b * nc + c
    vec = pl.BlockSpec((None, None, None, 1, chunk), lambda b, h, c: (b, h, c, 0, 0))
    return pl.pallas_call(
        functools.partial(_mlstm_kernel, lane_i=lane_i, lane_b=lane_b),
        grid=(batch, ML_HEADS, nc),
        in_specs=[pl.BlockSpec((chunk, dk), lambda b, h, c: (row(b, h, c), h)),
                  pl.BlockSpec((chunk, dk), lambda b, h, c: (row(b, h, c), ML_HEADS + h)),
                  pl.BlockSpec((chunk, dv), lambda b, h, c: (row(b, h, c), col_v + h)),
                  pl.BlockSpec((chunk, gates.shape[1]), lambda b, h, c: (row(b, h, c), 0)),
                  vec, vec,
                  pl.BlockSpec((chunk, dv), lambda b, h, c: (row(b, h, c), col_mo + h)),
                  pl.BlockSpec((1, dv), lambda b, h, c: (0, h))],
        out_specs=pl.BlockSpec((chunk, dv), lambda b, h, c: (row(b, h, c), h)),
        out_shape=jax.ShapeDtypeStruct((n, ML_HEADS * dv), BF16),
        scratch_shapes=[pltpu.VMEM((dk, dv), F32), pltpu.VMEM((1, dk), F32), pltpu.VMEM((8, 128), F32)],
        compiler_params=_params("parallel", "parallel", "arbitrary"),
        name="mlstm",
    )(qk, qk, qkv, gates, brow, irow, proj, norm_g.reshape(1, ML_HEADS * dv))


def _merge_kernel(yf_ref, ym_ref, wf_ref, wm_ref, ga_ref, gb_ref, o_ref):
    a = jnp.dot(yf_ref[...], wf_ref[...], preferred_element_type=F32)
    b = jnp.dot(ym_ref[...], wm_ref[...], preferred_element_type=F32)
    o_ref[...] = (jax.nn.sigmoid(ga_ref[...]) * a + jax.nn.sigmoid(gb_ref[...]) * b).astype(o_ref.dtype)


def _merge(y_fox, y_ml, w_fox, w_ml, proj, col_ga, col_gb):
    n, kf = y_fox.shape
    km = y_ml.shape[1]
    d = w_fox.shape[1]
    bm, bn = _tile(n, 512), _tile(d, 1024)
    ga0, gb0 = col_ga // bn, col_gb // bn
    return pl.pallas_call(
        _merge_kernel,
        grid=(n // bm, d // bn),
        in_specs=[pl.BlockSpec((bm, kf), lambda i, j: (i, 0)),
                  pl.BlockSpec((bm, km), lambda i, j: (i, 0)),
                  pl.BlockSpec((kf, bn), lambda i, j: (0, j)),
                  pl.BlockSpec((km, bn), lambda i, j: (0, j)),
                  pl.BlockSpec((bm, bn), lambda i, j: (i, ga0 + j)),
                  pl.BlockSpec((bm, bn), lambda i, j: (i, gb0 + j))],
        out_specs=pl.BlockSpec((bm, bn), lambda i, j: (i, j)),
        out_shape=jax.ShapeDtypeStruct((n, d), BF16),
        compiler_params=_params("parallel", "parallel"),
        name="merge_proj",
    )(y_fox, y_ml, w_fox, w_ml, proj, proj)


def _router_kernel(x_ref, w_ref, b_ref, oi_ref, of_ref, cnt_ref, carry_ref, *, n_groups, per_group):
    @pl.when(pl.program_id(0) == 0)
    def _():
        carry_ref[...] = jnp.zeros_like(carry_ref)

    tr = x_ref.shape[0]
    n_exp = n_groups * per_group
    logits = jnp.dot(x_ref[...], w_ref[...], preferred_element_type=F32,
                     precision=lax.Precision.HIGHEST) + b_ref[...]
    lane = lax.broadcasted_iota(jnp.int32, logits.shape, 1)
    big = jnp.int32(logits.shape[1])
    first_lane = lambda cond: jnp.min(jnp.where(cond, lane, big), axis=-1, keepdims=True)

    is_group = (lane >= n_exp) & (lane < n_exp + n_groups)
    gl = jnp.where(is_group, logits, -jnp.inf)
    g_max = jnp.max(gl, axis=-1, keepdims=True)
    g_sel = first_lane(gl == g_max) - n_exp
    p_g_sel = 1.0 / jnp.sum(jnp.exp(gl - g_max), axis=-1, keepdims=True)

    in_group = (lane >= g_sel * per_group) & (lane < (g_sel + 1) * per_group)
    el = jnp.where(in_group, logits, -jnp.inf)
    e_max = jnp.max(el, axis=-1, keepdims=True)
    ee = jnp.exp(el - e_max)
    pe = jnp.where(in_group, ee / jnp.sum(ee, axis=-1, keepdims=True), -1.0)
    p1 = jnp.max(pe, axis=-1, keepdims=True)
    e1 = first_lane(pe == p1)
    pe2 = jnp.where(lane == e1, -1.0, pe)
    p2 = jnp.max(pe2, axis=-1, keepdims=True)
    e2 = first_lane(pe2 == p2)
    p_sum = p1 + p2
    w1 = p_g_sel * p1 / p_sum
    w2 = p_g_sel * p2 / p_sum

    onehot = (lane == e1) | (lane == e2)
    strict = (lax.broadcasted_iota(jnp.int32, (tr, tr), 0)
              > lax.broadcasted_iota(jnp.int32, (tr, tr), 1)).astype(BF16)
    before = jnp.dot(strict, onehot.astype(BF16), preferred_element_type=F32) + carry_ref[...]
    r1 = jnp.sum(jnp.where(lane == e1, before, 0.0), axis=-1, keepdims=True).astype(jnp.int32)
    r2 = jnp.sum(jnp.where(lane == e2, before, 0.0), axis=-1, keepdims=True).astype(jnp.int32)
    carry_ref[...] += jnp.sum(onehot.astype(F32), axis=0, keepdims=True)
    cnt_ref[...] = carry_ref[...].astype(jnp.int32)

    oi_ref[...] = jnp.where(lane == 0, e1, jnp.where(lane == 1, e2, jnp.where(lane == 2, r1, r2)))
    of_ref[...] = jnp.where(lane == 0, w1, w2)


def _router(x, w_route, b_route, n_groups, per_group):
    n, d = x.shape
    lanes = w_route.shape[1]
    tr = _tile(n, 256)
    row = pl.BlockSpec((tr, lanes), lambda i: (i, 0))
    vec = pl.BlockSpec((1, lanes), lambda i: (0, 0))
    return pl.pallas_call(
        functools.partial(_router_kernel, n_groups=n_groups, per_group=per_group),
        grid=(n // tr,),
        in_specs=[pl.BlockSpec((tr, d), lambda i: (i, 0)),
                  pl.BlockSpec((d, lanes), lambda i: (0, 0)), vec],
        out_specs=[row, row, vec],
        out_shape=[jax.ShapeDtypeStruct((n, lanes), jnp.int32),
                   jax.ShapeDtypeStruct((n, lanes), F32),
                   jax.ShapeDtypeStruct((1, lanes), jnp.int32)],
        scratch_shapes=[pltpu.VMEM((1, lanes), F32)],
        compiler_params=_params("arbitrary"),
        name="router",
    )(x, w_route, b_route)


def _expert_kernel(blk_e_ref, first_ref, next_e_ref, wslot_ref, n_used_ref, tok_ref, tok_next_ref,
                   x_hbm, wt_ref, wg_hbm, wu_hbm, wd_hbm, o_ref, xbuf, wg_buf, wu_buf, wd_buf, sem, wsem):
    b = pl.program_id(0)
    tb = xbuf.shape[1]
    n_used = n_used_ref[0]
    slot = b % 2
    ws = wslot_ref[b]

    def row_copy(ids_ref, r, s):
        return pltpu.make_async_copy(x_hbm.at[pl.ds(ids_ref[r], 1), :],
                                     xbuf.at[s, pl.ds(r, 1), :], sem.at[s])

    def wait_rows():
        pltpu.make_async_copy(x_hbm.at[pl.ds(0, tb), :], xbuf.at[slot], sem.at[slot]).wait()

    def weight_copies(e, s):
        return [pltpu.make_async_copy(hbm.at[e], buf.at[s], wsem.at[s, k])
                for k, (hbm, buf) in enumerate(((wg_hbm, wg_buf), (wu_hbm, wu_buf), (wd_hbm, wd_buf)))]

    def compute():
        xb = xbuf[slot].astype(BF16)
        gate = jnp.dot(xb, wg_buf[ws], preferred_element_type=F32)
        up = jnp.dot(xb, wu_buf[ws], preferred_element_type=F32)
        act = (gate * jax.nn.sigmoid(gate) * up).astype(BF16)
        o_ref[...] = jnp.dot(act, wd_buf[ws], preferred_element_type=F32) * wt_ref[...]

    @pl.when((b == 0) & (n_used > 0))
    def _():
        for c in weight_copies(blk_e_ref[0], ws):
            c.start()

        def start(r, c):
            row_copy(tok_ref, r, slot).start()
            return c
        lax.fori_loop(0, tb, start, 0)

    @pl.when((b < n_used) & (first_ref[b] == 1))
    def _():
        for c in weight_copies(blk_e_ref[b], ws):
            c.wait()

        @pl.when(next_e_ref[b] >= 0)
        def _():
            for c in weight_copies(next_e_ref[b], 1 - ws):
                c.start()

    @pl.when(b + 1 < n_used)
    def _():
        wait_rows()
        for r in range(tb):
            row_copy(tok_next_ref, r, 1 - slot).start()
        compute()

    @pl.when(b + 1 == n_used)
    def _():
        wait_rows()
        compute()

    @pl.when(b >= n_used)
    def _():
        o_ref[...] = jnp.zeros_like(o_ref)


def _experts(x, slot_tok, slot_w, block_e, n_used, w_gate, w_up, w_down, tb):
    n, d = x.shape
    _, _, f = w_gate.shape
    n_blocks = block_e.shape[0]
    blk = jnp.arange(n_blocks, dtype=jnp.int32)
    first = (blk < n_used[0]) & ((blk == 0) | (block_e != jnp.roll(block_e, 1)))
    next_first = lax.cummin(jnp.where(first, blk, n_blocks), reverse=True)
    next_first = jnp.concatenate([next_first[1:], jnp.full((1,), n_blocks, jnp.int32)])
    next_e = jnp.where(next_first < n_blocks, block_e[jnp.minimum(next_first, n_blocks - 1)], -1)
    wslot = (jnp.cumsum(first.astype(jnp.int32)) - 1) % 2
    smem_tb = lambda shift: pl.BlockSpec(
        (tb,), lambda b, *_: (jnp.minimum(b + shift, n_blocks - 1),), memory_space=pltpu.SMEM)
    grid_spec = pltpu.PrefetchScalarGridSpec(
        num_scalar_prefetch=5,
        grid=(n_blocks,),
        in_specs=[smem_tb(0), smem_tb(1),
                  pl.BlockSpec(memory_space=pl.ANY),
                  pl.BlockSpec((tb, 1), lambda b, *_: (b, 0)),
                  pl.BlockSpec(memory_space=pl.ANY),
                  pl.BlockSpec(memory_space=pl.ANY),
                  pl.BlockSpec(memory_space=pl.ANY)],
        out_specs=pl.BlockSpec((tb, d), lambda b, *_: (b, 0)),
        scratch_shapes=[pltpu.VMEM((2, tb, d), F32),
                        pltpu.VMEM((2, d, f), BF16), pltpu.VMEM((2, d, f), BF16),
                        pltpu.VMEM((2, f, d), BF16),
                        pltpu.SemaphoreType.DMA((2,)), pltpu.SemaphoreType.DMA((2, 3))],
    )
    return pl.pallas_call(
        _expert_kernel,
        grid_spec=grid_spec,
        out_shape=jax.ShapeDtypeStruct((n_blocks * tb, d), F32),
        compiler_params=_params("arbitrary"),
        name="experts",
    )(block_e, first.astype(jnp.int32), next_e.astype(jnp.int32), wslot.astype(jnp.int32), n_used,
      slot_tok, slot_tok, x, slot_w.reshape(-1, 1), w_gate, w_up, w_down)


def _combine_kernel(dest_ref, dest_next_ref, y_hbm, h_ref, g_ref, b_ref, o_ref, buf, sem, *, alpha):
    i = pl.program_id(0)
    last = pl.num_programs(0) - 1
    tm = h_ref.shape[0]
    slot = i % 2

    def row_copy(ids_ref, r, k, s):
        return pltpu.make_async_copy(y_hbm.at[pl.ds(ids_ref[r * TOP_K + k], 1), :],
                                     buf.at[s, k, pl.ds(r, 1), :], sem.at[s])

    def compute():
        y = buf[slot, 0]
        for k in range(1, TOP_K):
            y = y + buf[slot, k]
        o_ref[...] = _layer_norm(alpha * h_ref[...] + y, g_ref[...], b_ref[...])

    @pl.when(i == 0)
    def _():
        def start(r, c):
            for k in range(TOP_K):
                row_copy(dest_ref, r, k, slot).start()
            return c
        lax.fori_loop(0, tm, start, 0)

    for k in range(TOP_K):
        pltpu.make_async_copy(y_hbm.at[pl.ds(0, tm), :], buf.at[slot, k], sem.at[slot]).wait()

    @pl.when(i < last)
    def _():
        for r in range(tm):
            for k in range(TOP_K):
                row_copy(dest_next_ref, r, k, 1 - slot).start()
        compute()

    @pl.when(i == last)
    def _():
        compute()


def _combine_ln(h, y_slots, dest, g, b, alpha):
    n, d = h.shape
    tm = _tile(n, 256)
    tiles = n // tm
    return pl.pallas_call(
        functools.partial(_combine_kernel, alpha=alpha),
        grid=(tiles,),
        in_specs=[pl.BlockSpec((tm * TOP_K,), lambda i: (i,), memory_space=pltpu.SMEM),
                  pl.BlockSpec((tm * TOP_K,), lambda i: (jnp.minimum(i + 1, tiles - 1),),
                               memory_space=pltpu.SMEM),
                  pl.BlockSpec(memory_space=pl.ANY),
                  pl.BlockSpec((tm, d), lambda i: (i, 0)),
                  pl.BlockSpec((1, d), lambda i: (0, 0)),
                  pl.BlockSpec((1, d), lambda i: (0, 0))],
        out_specs=pl.BlockSpec((tm, d), lambda i: (i, 0)),
        out_shape=jax.ShapeDtypeStruct((n, d), F32),
        scratch_shapes=[pltpu.VMEM((2, TOP_K, tm, d), F32), pltpu.SemaphoreType.DMA((2,))],
        compiler_params=_params("arbitrary"),
        name="combine_ln",
    )(dest, dest, y_slots, h, g.reshape(1, d), b.reshape(1, d))


EXPERT_SLOT_BLOCK = 256
ML_CHUNK = 256
FOX_KEY_BLOCK = 512
FOX_UNROLL = 2


def _mixer(h32, h16, batch, seq, w_in, b_fox_f, b_ml_i, b_ml_f, conv_w, conv_b, ml_norm_g,
           w_proj_fox, w_proj_ml, w_out, expert_w):
    n, d = h32.shape
    fox_w = d // 2
    fox_heads = fox_w // FOX_HEAD_DIM
    qk_w = ML_HEADS * ML_QK_DIM
    v_w = ML_HEADS * ML_V_DIM
    widths = (fox_w, fox_w, fox_w, fox_heads, qk_w, qk_w, v_w, ML_HEADS, ML_HEADS, v_w, d, d)
    offs = [0]
    for w in widths:
        offs.append(offs[-1] + w)
    n_gate = fox_heads + 2 * ML_HEADS
    gate_bias = jnp.pad(jnp.concatenate([b_fox_f, b_ml_i, b_ml_f]), (0, GATE_LANES - n_gate)).reshape(1, -1)

    wt = jnp.transpose(w_in)
    proj_fox = _in_proj(h16, wt, offs[0], 3 * fox_w, BF16, "in_proj_fox")
    proj_qk = _in_proj(h16, wt, offs[4], 2 * qk_w, F32, "in_proj_qk")
    proj_mv = _in_proj(h16, wt, offs[6], v_w, BF16, "in_proj_mv")
    proj_og = _in_proj(h16, wt, offs[9], v_w + 2 * d, F32, "in_proj_og")
    gate_pre = _gate_proj(h16, wt, offs[3], offs[7], fox_heads)

    chunk = _tile(seq, ML_CHUNK)
    gates = _gates(gate_pre, gate_bias, batch, chunk, fox_heads)

    g3 = gates.reshape(batch, seq, GATE_LANES)
    tk = _tile(seq, FOX_KEY_BLOCK)
    c_heads = jnp.transpose(g3[:, :, :fox_heads], (0, 2, 1))
    c_rows = c_heads.reshape(batch, fox_heads, seq // tk, 1, tk)
    c_cols = c_heads.reshape(batch, fox_heads, seq, 1)
    lane_i, lane_b = fox_heads, fox_heads + ML_HEADS
    rows = lambda lo: jnp.transpose(g3[:, :, lo:lo + ML_HEADS], (0, 2, 1)).reshape(
        batch, ML_HEADS, seq // chunk, 1, chunk)
    irow, brow = rows(lane_i), rows(lane_b)

    v_t = jnp.transpose(proj_fox[:, 2 * fox_w:3 * fox_w].reshape(batch, seq, fox_heads, FOX_HEAD_DIM),
                        (0, 2, 3, 1))
    later_w = [w_proj_fox, w_proj_ml, w_out] + list(expert_w)
    views = _cast_plan(later_w, batch * fox_heads * (seq // tk))
    y_fox, cast = _fox_attention(proj_fox, v_t, c_rows, c_cols, batch, seq, fox_heads,
                                 col_q=0, col_k=fox_heads, to_cast=views or [])
    if views:
        later_w16 = [c.reshape(a.shape) for c, a in zip(cast, later_w)]
    else:
        later_w16 = [a.astype(BF16) for a in later_w]
    w_proj_fox16, w_proj_ml16, w_out16 = later_w16[:3]
    qk = _conv_silu(proj_qk, conv_w, conv_b, seq, 2 * qk_w)
    y_ml = _mlstm(qk, proj_mv, proj_og, gates, brow, irow, ml_norm_g, batch, seq, chunk,
                  col_v=0, col_mo=0, lane_i=lane_i, lane_b=lane_b)
    merged = _merge(y_fox, y_ml, w_proj_fox16, w_proj_ml16, proj_og, col_ga=v_w, col_gb=v_w + d)
    return _matmul(merged, w_out16, F32, "out_proj"), later_w16[3:]


def _moe(h, w_group, b_group, w_router, b_router, w_gate, w_up, w_down, ln_g, ln_b, alpha):
    n, d = h.shape
    n_groups = w_group.shape[1]
    n_exp = w_router.shape[1]
    lanes = GATE_LANES
    assert n_exp + n_groups <= lanes
    pad = lanes - n_exp - n_groups
    w_route = jnp.pad(jnp.concatenate([w_router, w_group], axis=1), ((0, 0), (0, pad)))
    b_route = jnp.pad(jnp.concatenate([b_router, b_group]), (0, pad)).reshape(1, lanes)
    oi, of, cnt = _router(h, w_route, b_route, n_groups, n_exp // n_groups)
    e_idx, rank, gate_w = oi[:, 0:TOP_K], oi[:, TOP_K:2 * TOP_K], of[:, 0:TOP_K]
    counts = cnt[0, :n_exp]

    tb = EXPERT_SLOT_BLOCK
    n_assign = n * TOP_K
    n_blocks = (n_assign + n_exp * (tb - 1) + tb - 1) // tb
    padded = (counts + tb - 1) // tb * tb
    pad_ends = jnp.cumsum(padded)
    dest = ((pad_ends - padded)[e_idx] + rank).reshape(-1)
    tok = jnp.repeat(jnp.arange(n, dtype=jnp.int32), TOP_K)
    pairs = jnp.stack([tok, lax.bitcast_convert_type(gate_w.reshape(-1), jnp.int32)], axis=1)
    slots = jnp.zeros((n_blocks * tb, 2), jnp.int32).at[dest].set(pairs)
    slot_tok = slots[:, 0]
    slot_w = lax.bitcast_convert_type(slots[:, 1], F32)
    n_used = pad_ends[-1] // tb
    blk = jnp.arange(n_blocks, dtype=jnp.int32)
    block_e = jnp.minimum(jnp.searchsorted(pad_ends, blk * tb, side="right"), n_exp - 1).astype(jnp.int32)
    block_e = jnp.where(blk < n_used, block_e, block_e[jnp.maximum(n_used - 1, 0)])

    y_slots = _experts(h, slot_tok, slot_w, block_e, n_used.reshape(1).astype(jnp.int32),
                       w_gate, w_up, w_down, tb)
    return _combine_ln(h, y_slots, dest.astype(jnp.int32), ln_g, ln_b, alpha)


def kernel(x, ln_in_g, ln_in_b, w_in, b_fox_f, b_ml_i, b_ml_f, conv_w, conv_b, ml_norm_g, w_proj_fox, w_proj_ml, w_out, ln_mix_g, ln_mix_b, w_group, b_group, w_router, b_router, w_gate, w_up, w_down, ln_moe_g, ln_moe_b):
    batch, seq, d = x.shape
    depth = w_in.shape[0]
    alpha = (2 * depth) ** 0.25
    h32, h16 = _ln_in(x.reshape(batch * seq, d), ln_in_g, ln_in_b)
    for l in range(depth):
        mix, expert_w16 = _mixer(h32, h16, batch, seq, w_in[l], b_fox_f[l], b_ml_i[l], b_ml_f[l],
                                 conv_w[l], conv_b[l], ml_norm_g[l], w_proj_fox[l], w_proj_ml[l],
                                 w_out[l], [w_gate[l], w_up[l], w_down[l]])
        h32 = _ln_res(h32, mix, ln_mix_g[l], ln_mix_b[l], alpha)
        h32 = _moe(h32, w_group[l], b_group[l], w_router[l], b_router[l], *expert_w16,
                   ln_moe_g[l], ln_moe_b[l], alpha)
        if l + 1 < depth:
            h16 = h32.astype(BF16)
    return h32.reshape(batch, seq, d)
```

```python
import functools

import jax
import jax.numpy as jnp
from jax import lax
from jax.experimental import pallas as pl
from jax.experimental.pallas import tpu as pltpu

F32 = jnp.float32
BF16 = jnp.bfloat16

LN_EPS = 1e-5
FOX_HEAD_DIM = 128
ML_HEADS = 4
ML_QK_DIM = 256
ML_V_DIM = 512
TOP_K = 2
LANES = 128
F32_SUBLANES = 8
BF16_SUBLANES = 16
GATE_LANES = LANES
LOG2_E = 1.4426950408889634

V7X_VMEM_BYTES = 64 * 1024 * 1024
VMEM_LIMIT_BYTES = V7X_VMEM_BYTES - 8 * 1024 * 1024


def _params(*sem):
    return pltpu.CompilerParams(dimension_semantics=sem, vmem_limit_bytes=VMEM_LIMIT_BYTES)


def _tile(dim, pref):
    t = min(dim, pref)
    while dim % t:
        t //= 2
    return t


def _layer_norm(x, g, b):
    mu = jnp.mean(x, axis=-1, keepdims=True)
    xc = x - mu
    var = jnp.mean(xc * xc, axis=-1, keepdims=True)
    return xc * lax.rsqrt(var + LN_EPS) * g + b


def _log_sigmoid(x):
    return jnp.minimum(x, 0.0) - jnp.log1p(jnp.exp(-jnp.abs(x)))


def _ln_in_kernel(x_ref, g_ref, b_ref, o32_ref, o16_ref):
    y = _layer_norm(x_ref[...], g_ref[...], b_ref[...])
    o32_ref[...] = y
    o16_ref[...] = y.astype(BF16)


def _ln_in(x, g, b):
    n, d = x.shape
    tr = _tile(n, 256)
    row = pl.BlockSpec((tr, d), lambda i: (i, 0))
    vec = pl.BlockSpec((1, d), lambda i: (0, 0))
    return pl.pallas_call(
        _ln_in_kernel,
        grid=(n // tr,),
        in_specs=[row, vec, vec],
        out_specs=[row, row],
        out_shape=[jax.ShapeDtypeStruct((n, d), F32), jax.ShapeDtypeStruct((n, d), BF16)],
        compiler_params=_params("parallel"),
        name="ln_in",
    )(x, g.reshape(1, d), b.reshape(1, d))


def _pack_bf16_pairs(x):
    half = x.shape[1] // 2
    bits = lambda v: lax.bitcast_convert_type(v.astype(BF16).astype(F32), jnp.uint32)
    word = lax.shift_right_logical(bits(x[:, :half]), jnp.uint32(16)) | bits(x[:, half:])
    return lax.bitcast_convert_type(word, jnp.int32)


def _unpack_bf16_pairs(w):
    u = lax.bitcast_convert_type(w, jnp.uint32)
    lo = lax.bitcast_convert_type(lax.shift_left(u, jnp.uint32(16)), F32)
    hi = lax.bitcast_convert_type(u & jnp.uint32(0xFFFF0000), F32)
    return lo, hi


def _ln_res_kernel(h_ref, y_ref, g_ref, b_ref, o_ref, op_ref, *, alpha):
    out = _layer_norm(alpha * h_ref[...] + y_ref[...], g_ref[...], b_ref[...])
    o_ref[...] = out
    op_ref[...] = _pack_bf16_pairs(out)


def _ln_res(h, y, g, b, alpha):
    n, d = h.shape
    tr = _tile(n, 256)
    row = pl.BlockSpec((tr, d), lambda i: (i, 0))
    vec = pl.BlockSpec((1, d), lambda i: (0, 0))
    return pl.pallas_call(
        functools.partial(_ln_res_kernel, alpha=alpha),
        grid=(n // tr,),
        in_specs=[row, row, vec, vec],
        out_specs=[row, pl.BlockSpec((tr, d // 2), lambda i: (i, 0))],
        out_shape=[jax.ShapeDtypeStruct((n, d), F32), jax.ShapeDtypeStruct((n, d // 2), jnp.int32)],
        compiler_params=_params("parallel"),
        name="ln_mix",
    )(h, y, g.reshape(1, d), b.reshape(1, d))


def _mm_kernel(x_ref, w_ref, o_ref):
    o_ref[...] = jnp.dot(x_ref[...], w_ref[...], preferred_element_type=F32).astype(o_ref.dtype)


def _matmul(x, w, out_dtype, name, bm=1024, bn=1024):
    m, k = x.shape
    _, n = w.shape
    bm, bn = _tile(m, bm), _tile(n, bn)
    return pl.pallas_call(
        _mm_kernel,
        grid=(m // bm, n // bn),
        in_specs=[pl.BlockSpec((bm, k), lambda i, j: (i, 0)),
                  pl.BlockSpec((k, bn), lambda i, j: (0, j))],
        out_specs=pl.BlockSpec((bm, bn), lambda i, j: (i, j)),
        out_shape=jax.ShapeDtypeStruct((m, n), out_dtype),
        compiler_params=_params("parallel", "parallel"),
        name=name,
    )(x, w)


_NT = (((1,), (1,)), ((), ()))


def _in_proj_kernel(*refs, off, n_col_blocks):
    if off:
        x_ref, w_ref, wx_ref, o_ref, w16_ref = refs
    else:
        x_ref, w_ref, o_ref, w16_ref = refs
    rc = w_ref.shape[0]
    j, i = pl.program_id(0), pl.program_id(1)

    @pl.when(j < n_col_blocks)
    def _():
        w = w_ref[...]
        if off:
            w = jnp.concatenate([w, wx_ref[...]], axis=0)[off:off + rc, :]
        w16_ref[j % 2, pl.ds(pl.multiple_of(i * rc, rc), rc), :] = w.astype(BF16)

    @pl.when(j > 0)
    def _():
        o_ref[...] = lax.dot_general(x_ref[...], w16_ref[(j - 1) % 2], _NT,
                                     preferred_element_type=F32).astype(o_ref.dtype)


def _in_proj(x, wt, col_start, n_cols, out_dtype, name):
    m, k = x.shape
    bm, bn = _tile(m, 1024), _tile(n_cols, 1024)
    ni, nj = m // bm, n_cols // bn
    rc = bn // ni
    assert bn % ni == 0 and rc % BF16_SUBLANES == 0
    off = col_start % rc
    assert off % F32_SUBLANES == 0
    blk0 = (col_start - off) // rc
    chunk = lambda j, i: blk0 + jnp.minimum(j, nj - 1) * ni + i
    row = lambda j, i: jnp.where(j == 0, 0, i)
    in_specs = [pl.BlockSpec((bm, k), lambda j, i: (row(j, i), 0)),
                pl.BlockSpec((rc, k), lambda j, i: (chunk(j, i), 0))]
    args = [x, wt]
    if off:
        in_specs.append(pl.BlockSpec((rc, k), lambda j, i: (chunk(j, i) + 1, 0)))
        args.append(wt)
    return pl.pallas_call(
        functools.partial(_in_proj_kernel, off=off, n_col_blocks=nj),
        grid=(nj + 1, ni),
        in_specs=in_specs,
        out_specs=pl.BlockSpec((bm, bn), lambda j, i: (row(j, i), jnp.maximum(j - 1, 0))),
        out_shape=jax.ShapeDtypeStruct((m, n_cols), out_dtype),
        scratch_shapes=[pltpu.VMEM((2, bn, k), BF16)],
        compiler_params=_params("arbitrary", "arbitrary"),
        name=name,
    )(*args)


def _gate_proj_kernel(x_ref, wa_ref, wb_ref, o_ref, w16_ref):
    @pl.when(pl.program_id(0) == 0)
    def _():
        pad = w16_ref.shape[0] - wa_ref.shape[0] - wb_ref.shape[0]
        zeros = jnp.zeros((pad, w16_ref.shape[1]), F32)
        w16_ref[...] = jnp.concatenate([wa_ref[...], wb_ref[...], zeros], axis=0).astype(BF16)

    o_ref[...] = lax.dot_general(x_ref[...], w16_ref[...], _NT, preferred_element_type=F32)


def _gate_proj(x, wt, col_ff, col_mi, n_fox):
    m, k = x.shape
    n_ml = 2 * ML_HEADS
    assert n_fox % F32_SUBLANES == 0 and n_ml % F32_SUBLANES == 0 and n_fox + n_ml <= LANES
    assert col_ff % n_fox == 0 and col_mi % n_ml == 0
    bm = _tile(m, 1024)
    return pl.pallas_call(
        _gate_proj_kernel,
        grid=(m // bm,),
        in_specs=[pl.BlockSpec((bm, k), lambda i: (i, 0)),
                  pl.BlockSpec((n_fox, k), lambda i: (col_ff // n_fox, 0)),
                  pl.BlockSpec((n_ml, k), lambda i: (col_mi // n_ml, 0))],
        out_specs=pl.BlockSpec((bm, LANES), lambda i: (i, 0)),
        out_shape=jax.ShapeDtypeStruct((m, LANES), F32),
        scratch_shapes=[pltpu.VMEM((LANES, k), BF16)],
        compiler_params=_params("arbitrary"),
        name="in_proj_gates",
    )(x, wt, wt)


def _gates_kernel(g_ref, bias_ref, o_ref, carry_ref, *, n_fox, fox_inv_scale):
    @pl.when(pl.program_id(1) == 0)
    def _():
        carry_ref[...] = jnp.zeros_like(carry_ref)

    ts = g_ref.shape[0]
    x = g_ref[...] + bias_ref[...]
    lane = lax.broadcasted_iota(jnp.int32, x.shape, 1)
    is_input_gate = (lane >= n_fox) & (lane < n_fox + ML_HEADS)
    val = jnp.where(is_input_gate, 0.0, _log_sigmoid(x))
    tril = (lax.broadcasted_iota(jnp.int32, (ts, ts), 0)
            >= lax.broadcasted_iota(jnp.int32, (ts, ts), 1)).astype(BF16)
    hi = val.astype(BF16)
    rem = val - hi.astype(F32)
    mid = rem.astype(BF16)
    lo = (rem - mid.astype(F32)).astype(BF16)
    cs = (jnp.dot(tril, hi, preferred_element_type=F32)
          + jnp.dot(tril, mid, preferred_element_type=F32)
          + jnp.dot(tril, lo, preferred_element_type=F32))
    total = cs + jnp.where(lane < n_fox, carry_ref[...], 0.0)
    carry_ref[...] = total[ts - 1:ts, :]
    total = jnp.where(lane < n_fox, total * fox_inv_scale, total)
    o_ref[...] = jnp.where(is_input_gate, x, total)


def _gates(g, bias, batch, ts, n_fox):
    n, lanes = g.shape
    tiles = n // batch // ts
    blk = pl.BlockSpec((ts, lanes), lambda b, j: (b * tiles + j, 0))
    return pl.pallas_call(
        functools.partial(_gates_kernel, n_fox=n_fox, fox_inv_scale=FOX_HEAD_DIM ** 0.5),
        grid=(batch, tiles),
        in_specs=[blk, pl.BlockSpec((1, lanes), lambda b, j: (0, 0))],
        out_specs=blk,
        out_shape=jax.ShapeDtypeStruct((n, lanes), F32),
        scratch_shapes=[pltpu.VMEM((1, lanes), F32)],
        compiler_params=_params("parallel", "arbitrary"),
        name="gates",
    )(g, bias)


def _conv_kernel(x_ref, prev_ref, w_ref, b_ref, o_ref, *, tiles_per_seq, k_blocks_from, k_scale):
    ts = x_ref.shape[0]
    kw = w_ref.shape[0]
    halo = prev_ref.shape[0]
    first = pl.program_id(0) % tiles_per_seq == 0
    prev = jnp.where(first, 0.0, prev_ref[...])
    ext = jnp.concatenate([prev, x_ref[...]], axis=0)
    w = w_ref[...]
    acc = None
    for j in range(kw):
        off = halo - (kw - 1) + j
        term = w[j:j + 1, :] * ext[off:off + ts, :]
        acc = term if acc is None else acc + term
    acc = acc + b_ref[...]
    y = acc * jax.nn.sigmoid(acc)
    scale = jnp.where(pl.program_id(1) >= k_blocks_from, k_scale, 1.0)
    o_ref[...] = y * scale


def _conv_silu(proj, conv_w, conv_b, seq, width):
    n = proj.shape[0]
    kw = conv_w.shape[0]
    ts, tc, halo = _tile(seq, 512), _tile(width // 2, 512), 8
    assert kw - 1 <= halo
    tiles_per_seq = seq // ts
    return pl.pallas_call(
        functools.partial(_conv_kernel, tiles_per_seq=tiles_per_seq,
                          k_blocks_from=(width // 2) // tc, k_scale=ML_QK_DIM ** -0.5),
        grid=(n // ts, width // tc),
        in_specs=[pl.BlockSpec((ts, tc), lambda i, j: (i, j)),
                  pl.BlockSpec((halo, tc), lambda i, j: (jnp.maximum(i * (ts // halo) - 1, 0), j)),
                  pl.BlockSpec((kw, tc), lambda i, j: (0, j)),
                  pl.BlockSpec((1, tc), lambda i, j: (0, j))],
        out_specs=pl.BlockSpec((ts, tc), lambda i, j: (i, j)),
        out_shape=jax.ShapeDtypeStruct((n, width), F32),
        compiler_params=_params("parallel", "parallel"),
        name="conv_silu",
    )(proj, proj, conv_w, conv_b.reshape(1, width))


def _fox_kernel(*refs, scale, n_cast):
    q_ref, k_ref, vt_ref, cq_ref, ck_ref = refs[:5]
    cast_in = refs[5:5 + n_cast]
    o_ref = refs[5 + n_cast]
    cast_out = refs[6 + n_cast:6 + 2 * n_cast]
    ua_ref, ub_ref, mxa_ref, mxb_ref, m_ref, acc_ref = refs[6 + 2 * n_cast:]
    for src, dst in zip(cast_in, cast_out):
        dst[...] = src[...].astype(dst.dtype)

    tq, dh = q_ref.shape
    tk = tq
    qi = pl.program_id(2)
    c_exp = scale * LOG2_E
    q = q_ref[...]
    cq = cq_ref[...]
    buf_a, buf_b = (ua_ref, mxa_ref), (ub_ref, mxb_ref)

    def score(kb, buf, masked):
        u_ref, mx_ref = buf
        ks = pl.multiple_of(kb * tk, tk)
        u = lax.dot_general(k_ref[pl.ds(ks, tk), :], q, (((1,), (1,)), ((), ())),
                            preferred_element_type=F32) - ck_ref[pl.ds(ks, tk), :]
        if masked:
            u = jnp.where(lax.broadcasted_iota(jnp.int32, (tk, tq), 0)
                          <= lax.broadcasted_iota(jnp.int32, (tk, tq), 1), u, -jnp.inf)
        u_ref[...] = u
        mx_ref[...] = jnp.max(u, axis=0, keepdims=True)

    def absorb(kb, buf):
        u_ref, mx_ref = buf
        ks = pl.multiple_of(kb * tk, tk)
        m = m_ref[...]
        m_new = jnp.maximum(m, mx_ref[...] + cq)
        p = jnp.exp2((u_ref[...] + (cq - m_new)) * c_exp)
        a = jnp.exp2((m - m_new) * c_exp)
        acc_ref[...] = a * acc_ref[...] + jnp.dot(vt_ref[:, pl.ds(ks, tk)], p.astype(BF16),
                                                  preferred_element_type=F32)
        m_ref[...] = m_new

    m_ref[...] = jnp.full_like(m_ref, -jnp.inf)
    acc_ref[...] = jnp.zeros_like(acc_ref)

    @pl.when(qi == 0)
    def _():
        score(0, buf_a, masked=True)
        absorb(0, buf_a)

    @pl.when(qi > 0)
    def _():
        score(0, buf_a, masked=False)
        n_pairs = (qi - 1) // 2

        def pair(g, c):
            score(2 * g + 1, buf_b, masked=False)
            absorb(2 * g, buf_a)
            score(2 * g + 2, buf_a, masked=False)
            absorb(2 * g + 1, buf_b)
            return c

        lax.fori_loop(0, n_pairs, pair, 0)
        base = 2 * n_pairs

        @pl.when(qi % 2 == 1)
        def _():
            score(qi, buf_b, masked=True)
            absorb(base, buf_a)
            absorb(qi, buf_b)

        @pl.when(qi % 2 == 0)
        def _():
            score(base + 1, buf_b, masked=False)
            absorb(base, buf_a)
            score(qi, buf_a, masked=True)
            absorb(base + 1, buf_b)
            absorb(qi, buf_a)

    o_ref[...] = jnp.transpose(acc_ref[0:dh, :] / acc_ref[dh:dh + 1, :]).astype(o_ref.dtype)


def _fox_attention(qk, vt, c_rows, c_cols, batch, seq, heads, col_q, col_k, to_cast):
    n = qk.shape[0]
    dh = FOX_HEAD_DIM
    tq = c_rows.shape[-1]
    nq = seq // tq
    steps = batch * heads * nq

    def slab(a):
        rb = _cast_rows(a.shape[0], steps)
        last = a.shape[0] // rb - 1
        return pl.BlockSpec((rb, a.shape[1]),
                            lambda b, h, i: (jnp.minimum((b * heads + h) * nq + i, last), 0))
    outs = pl.pallas_call(
        functools.partial(_fox_kernel, scale=dh ** -0.5, n_cast=len(to_cast)),
        grid=(batch, heads, nq),
        in_specs=[pl.BlockSpec((tq, dh), lambda b, h, i: (b * nq + i, col_q + h)),
                  pl.BlockSpec((seq, dh), lambda b, h, i: (b, col_k + h)),
                  pl.BlockSpec((None, None, vt.shape[2], seq), lambda b, h, i: (b, h, 0, 0)),
                  pl.BlockSpec((None, None, None, 1, tq), lambda b, h, i: (b, h, i, 0, 0)),
                  pl.BlockSpec((None, None, seq, 1), lambda b, h, i: (b, h, 0, 0))]
                 + [slab(a) for a in to_cast],
        out_specs=[pl.BlockSpec((tq, dh), lambda b, h, i: (b * nq + i, h))] + [slab(a) for a in to_cast],
        out_shape=[jax.ShapeDtypeStruct((n, heads * dh), BF16)]
                  + [jax.ShapeDtypeStruct(a.shape, BF16) for a in to_cast],
        scratch_shapes=[pltpu.VMEM((tq, tq), F32), pltpu.VMEM((tq, tq), F32),
                        pltpu.VMEM((1, tq), F32), pltpu.VMEM((1, tq), F32),
                        pltpu.VMEM((1, tq), F32), pltpu.VMEM((vt.shape[2], tq), F32)],
        compiler_params=_params("arbitrary", "arbitrary", "arbitrary"),
        name="fox_attention",
    )(qk, qk, vt, c_rows, c_cols, *to_cast)
    return outs[0], outs[1:]


def _cast_rows(rows, steps):
    rb = max(BF16_SUBLANES, -(-rows // steps))
    return rb if rows % rb == 0 and rb % BF16_SUBLANES == 0 else None


def _cast_plan(arrays, steps):
    views = [a.reshape(-1, a.shape[-1]) for a in arrays]
    return views if all(_cast_rows(v.shape[0], steps) for v in views) else None


def _mlstm_kernel(q_ref, k_ref, v_ref, gate_ref, brow_ref, irow_ref, mo_ref, g_ref, o_ref,
                  c_ref, n_ref, m_ref, *, lane_i, lane_b):
    L = q_ref.shape[0]
    h = pl.program_id(1)

    @pl.when(pl.program_id(2) == 0)
    def _():
        c_ref[...] = jnp.zeros_like(c_ref)
        n_ref[...] = jnp.zeros_like(n_ref)
        m_ref[...] = jnp.zeros_like(m_ref)

    gates = gate_ref[...]
    lane = lax.broadcasted_iota(jnp.int32, gates.shape, 1)
    pick = lambda col: jnp.sum(jnp.where(lane == col + h, gates, 0.0), axis=-1, keepdims=True)
    icol = pick(lane_i)
    bcol = pick(lane_b)
    brow = brow_ref[...]
    irow = irow_ref[...]
    m_prev = m_ref[0:1, 0:1]

    q = q_ref[...].astype(BF16)
    k32 = k_ref[...]
    v = v_ref[...]

    tri = lax.broadcasted_iota(jnp.int32, (L, L), 0) >= lax.broadcasted_iota(jnp.int32, (L, L), 1)
    d = jnp.where(tri, bcol - brow + irow, -jnp.inf)
    inter = bcol + m_prev
    m = jnp.maximum(inter, jnp.max(d, axis=-1, keepdims=True))
    w_intra = jnp.exp(d - m)
    w_inter = jnp.exp(inter - m)
    s = lax.dot_general(q, k32.astype(BF16), (((1,), (1,)), ((), ())), preferred_element_type=F32)
    sc = s * w_intra
    num = (jnp.dot(sc.astype(BF16), v, preferred_element_type=F32)
           + w_inter * jnp.dot(q, c_ref[...].astype(BF16), preferred_element_type=F32))
    qn = jnp.sum(q_ref[...] * n_ref[...], axis=-1, keepdims=True)
    den = jnp.sum(sc, axis=-1, keepdims=True) + w_inter * qn
    hid = num / jnp.maximum(jnp.abs(den), jnp.exp(-m))

    b_last = bcol[L - 1:L, :]
    gcol = b_last - bcol + icol
    m_new = jnp.maximum(b_last + m_prev, jnp.max(gcol, axis=0, keepdims=True))
    w_k = jnp.exp(gcol - m_new)
    decay = jnp.exp(b_last + m_prev - m_new)
    kw = k32 * w_k
    c_ref[...] = decay * c_ref[...] + lax.dot_general(
        kw.astype(BF16), v, (((0,), (0,)), ((), ())), preferred_element_type=F32)
    n_ref[...] = decay * n_ref[...] + jnp.sum(kw, axis=0, keepdims=True)
    m_ref[...] = jnp.broadcast_to(m_new, m_ref.shape)

    mu = jnp.mean(hid, axis=-1, keepdims=True)
    hc = hid - mu
    var = jnp.mean(hc * hc, axis=-1, keepdims=True)
    hn = hc * lax.rsqrt(var + LN_EPS)
    o_ref[...] = (hn * g_ref[...] * jax.nn.sigmoid(mo_ref[...])).astype(o_ref.dtype)


def _mlstm(qk, qkv, proj, gates, brow, irow, norm_g, batch, seq, chunk, col_v, col_mo, lane_i, lane_b):
    n = qk.shape[0]
    dk, dv = ML_QK_DIM, ML_V_DIM
    nc = seq // chunk
    row = lambda b, h, c: b * nc + c
    vec = pl.BlockSpec((None, None, None, 1, chunk), lambda b, h, c: (b, h, c, 0, 0))
    return pl.pallas_call(
        functools.partial(_mlstm_kernel, lane_i=lane_i, lane_b=lane_b),
        grid=(batch, ML_HEADS, nc),
        in_specs=[pl.BlockSpec((chunk, dk), lambda b, h, c: (row(b, h, c), h)),
                  pl.BlockSpec((chunk, dk), lambda b, h, c: (row(b, h, c), ML_HEADS + h)),
                  pl.BlockSpec((chunk, dv), lambda b, h, c: (row(b, h, c), col_v + h)),
                  pl.BlockSpec((chunk, gates.shape[1]), lambda b, h, c: (row(b, h, c), 0)),
                  vec, vec,
                  pl.BlockSpec((chunk, dv), lambda b, h, c: (row(b, h, c), col_mo + h)),
                  pl.BlockSpec((1, dv), lambda b, h, c: (0, h))],
        out_specs=pl.BlockSpec((chunk, dv), lambda b, h, c: (row(b, h, c), h)),
        out_shape=jax.ShapeDtypeStruct((n, ML_HEADS * dv), BF16),
        scratch_shapes=[pltpu.VMEM((dk, dv), F32), pltpu.VMEM((1, dk), F32), pltpu.VMEM((8, 128), F32)],
        compiler_params=_params("parallel", "parallel", "arbitrary"),
        name="mlstm",
    )(qk, qk, qkv, gates, brow, irow, proj, norm_g.reshape(1, ML_HEADS * dv))


def _merge_kernel(yf_ref, ym_ref, wf_ref, wm_ref, ga_ref, gb_ref, o_ref):
    a = jnp.dot(yf_ref[...], wf_ref[...], preferred_element_type=F32)
    b = jnp.dot(ym_ref[...], wm_ref[...], preferred_element_type=F32)
    o_ref[...] = (jax.nn.sigmoid(ga_ref[...]) * a + jax.nn.sigmoid(gb_ref[...]) * b).astype(o_ref.dtype)


def _merge(y_fox, y_ml, w_fox, w_ml, proj, col_ga, col_gb):
    n, kf = y_fox.shape
    km = y_ml.shape[1]
    d = w_fox.shape[1]
    bm, bn = _tile(n, 512), _tile(d, 1024)
    ga0, gb0 = col_ga // bn, col_gb // bn
    return pl.pallas_call(
        _merge_kernel,
        grid=(n // bm, d // bn),
        in_specs=[pl.BlockSpec((bm, kf), lambda i, j: (i, 0)),
                  pl.BlockSpec((bm, km), lambda i, j: (i, 0)),
                  pl.BlockSpec((kf, bn), lambda i, j: (0, j)),
                  pl.BlockSpec((km, bn), lambda i, j: (0, j)),
                  pl.BlockSpec((bm, bn), lambda i, j: (i, ga0 + j)),
                  pl.BlockSpec((bm, bn), lambda i, j: (i, gb0 + j))],
        out_specs=pl.BlockSpec((bm, bn), lambda i, j: (i, j)),
        out_shape=jax.ShapeDtypeStruct((n, d), BF16),
        compiler_params=_params("parallel", "parallel"),
        name="merge_proj",
    )(y_fox, y_ml, w_fox, w_ml, proj, proj)


def _router_kernel(x_ref, w_ref, b_ref, oi_ref, of_ref, cnt_ref, carry_ref, *, n_groups, per_group):
    @pl.when(pl.program_id(0) == 0)
    def _():
        carry_ref[...] = jnp.zeros_like(carry_ref)

    tr = x_ref.shape[0]
    n_exp = n_groups * per_group
    logits = jnp.dot(x_ref[...], w_ref[...], preferred_element_type=F32,
                     precision=lax.Precision.HIGHEST) + b_ref[...]
    lane = lax.broadcasted_iota(jnp.int32, logits.shape, 1)
    big = jnp.int32(logits.shape[1])
    first_lane = lambda cond: jnp.min(jnp.where(cond, lane, big), axis=-1, keepdims=True)

    is_group = (lane >= n_exp) & (lane < n_exp + n_groups)
    gl = jnp.where(is_group, logits, -jnp.inf)
    g_max = jnp.max(gl, axis=-1, keepdims=True)
    g_sel = first_lane(gl == g_max) - n_exp
    p_g_sel = 1.0 / jnp.sum(jnp.exp(gl - g_max), axis=-1, keepdims=True)

    in_group = (lane >= g_sel * per_group) & (lane < (g_sel + 1) * per_group)
    el = jnp.where(in_group, logits, -jnp.inf)
    e_max = jnp.max(el, axis=-1, keepdims=True)
    ee = jnp.exp(el - e_max)
    pe = jnp.where(in_group, ee / jnp.sum(ee, axis=-1, keepdims=True), -1.0)
    p1 = jnp.max(pe, axis=-1, keepdims=True)
    e1 = first_lane(pe == p1)
    pe2 = jnp.where(lane == e1, -1.0, pe)
    p2 = jnp.max(pe2, axis=-1, keepdims=True)
    e2 = first_lane(pe2 == p2)
    p_sum = p1 + p2
    w1 = p_g_sel * p1 / p_sum
    w2 = p_g_sel * p2 / p_sum

    onehot = (lane == e1) | (lane == e2)
    strict = (lax.broadcasted_iota(jnp.int32, (tr, tr), 0)
              > lax.broadcasted_iota(jnp.int32, (tr, tr), 1)).astype(BF16)
    before = jnp.dot(strict, onehot.astype(BF16), preferred_element_type=F32) + carry_ref[...]
    r1 = jnp.sum(jnp.where(lane == e1, before, 0.0), axis=-1, keepdims=True).astype(jnp.int32)
    r2 = jnp.sum(jnp.where(lane == e2, before, 0.0), axis=-1, keepdims=True).astype(jnp.int32)
    carry_ref[...] += jnp.sum(onehot.astype(F32), axis=0, keepdims=True)
    cnt_ref[...] = carry_ref[...].astype(jnp.int32)

    oi_ref[...] = jnp.where(lane == 0, e1, jnp.where(lane == 1, e2, jnp.where(lane == 2, r1, r2)))
    of_ref[...] = jnp.where(lane == 0, w1, w2)


def _router(x, w_route, b_route, n_groups, per_group):
    n, d = x.shape
    lanes = w_route.shape[1]
    tr = _tile(n, 256)
    row = pl.BlockSpec((tr, lanes), lambda i: (i, 0))
    vec = pl.BlockSpec((1, lanes), lambda i: (0, 0))
    return pl.pallas_call(
        functools.partial(_router_kernel, n_groups=n_groups, per_group=per_group),
        grid=(n // tr,),
        in_specs=[pl.BlockSpec((tr, d), lambda i: (i, 0)),
                  pl.BlockSpec((d, lanes), lambda i: (0, 0)), vec],
        out_specs=[row, row, vec],
        out_shape=[jax.ShapeDtypeStruct((n, lanes), jnp.int32),
                   jax.ShapeDtypeStruct((n, lanes), F32),
                   jax.ShapeDtypeStruct((1, lanes), jnp.int32)],
        scratch_shapes=[pltpu.VMEM((1, lanes), F32)],
        compiler_params=_params("arbitrary"),
        name="router",
    )(x, w_route, b_route)


def _expert_kernel(blk_e_ref, n_used_ref, tok_ref, tok_next_ref, x_hbm, wt_ref, wg_ref, wu_ref, wd_ref,
                   o_ref, xbuf, sem):
    b = pl.program_id(0)
    tb, half = xbuf.shape[1:]
    n_used = n_used_ref[0]
    slot = b % 2

    def row_copy(ids_ref, r, s):
        return pltpu.make_async_copy(x_hbm.at[pl.ds(ids_ref[r], 1), :],
                                     xbuf.at[s, pl.ds(r, 1), :], sem.at[s])

    def wait_rows():
        pltpu.make_async_copy(x_hbm.at[pl.ds(0, tb), :], xbuf.at[slot], sem.at[slot]).wait()

    def compute():
        lo, hi = _unpack_bf16_pairs(xbuf[slot])
        lo, hi = lo.astype(BF16), hi.astype(BF16)
        proj = lambda w_ref: (jnp.dot(lo, w_ref[0:half, :], preferred_element_type=F32)
                              + jnp.dot(hi, w_ref[half:2 * half, :], preferred_element_type=F32))
        gate, up = proj(wg_ref), proj(wu_ref)
        act = (gate * jax.nn.sigmoid(gate) * up).astype(BF16)
        out = jnp.dot(act, wd_ref[...], preferred_element_type=F32) * wt_ref[...]
        o_ref[...] = _pack_bf16_pairs(out)

    @pl.when((b == 0) & (n_used > 0))
    def _():
        def start(r, c):
            row_copy(tok_ref, r, slot).start()
            return c
        lax.fori_loop(0, tb, start, 0)

    @pl.when(b + 1 < n_used)
    def _():
        wait_rows()
        for r in range(tb):
            row_copy(tok_next_ref, r, 1 - slot).start()
        compute()

    @pl.when(b + 1 == n_used)
    def _():
        wait_rows()
        compute()

    @pl.when(b >= n_used)
    def _():
        o_ref[...] = jnp.zeros_like(o_ref)


def _experts(xp, slot_tok, slot_w, block_e, n_used, w_gate, w_up, w_down, tb):
    n, half = xp.shape
    _, d, f = w_gate.shape
    assert d == 2 * half
    n_blocks = block_e.shape[0]
    smem_tb = lambda shift: pl.BlockSpec(
        (tb,), lambda b, *_: (jnp.minimum(b + shift, n_blocks - 1),), memory_space=pltpu.SMEM)
    grid_spec = pltpu.PrefetchScalarGridSpec(
        num_scalar_prefetch=2,
        grid=(n_blocks,),
        in_specs=[smem_tb(0), smem_tb(1),
                  pl.BlockSpec(memory_space=pl.ANY),
                  pl.BlockSpec((tb, 1), lambda b, *_: (b, 0)),
                  pl.BlockSpec((None, d, f), lambda b, be, nu: (be[b], 0, 0)),
                  pl.BlockSpec((None, d, f), lambda b, be, nu: (be[b], 0, 0)),
                  pl.BlockSpec((None, f, d), lambda b, be, nu: (be[b], 0, 0))],
        out_specs=pl.BlockSpec((tb, half), lambda b, *_: (b, 0)),
        scratch_shapes=[pltpu.VMEM((2, tb, half), jnp.int32), pltpu.SemaphoreType.DMA((2,))],
    )
    return pl.pallas_call(
        _expert_kernel,
        grid_spec=grid_spec,
        out_shape=jax.ShapeDtypeStruct((n_blocks * tb, half), jnp.int32),
        compiler_params=_params("arbitrary"),
        name="experts",
    )(block_e, n_used, slot_tok, slot_tok, xp, slot_w.reshape(-1, 1), w_gate, w_up, w_down)


def _combine_kernel(dest_ref, dest_next_ref, y_hbm, h_ref, g_ref, b_ref, o_ref, buf, sem, *, alpha):
    i = pl.program_id(0)
    last = pl.num_programs(0) - 1
    tm = h_ref.shape[0]
    slot = i % 2

    def row_copy(ids_ref, r, k, s):
        return pltpu.make_async_copy(y_hbm.at[pl.ds(ids_ref[r * TOP_K + k], 1), :],
                                     buf.at[s, k, pl.ds(r, 1), :], sem.at[s])

    def compute():
        y = None
        for k in range(TOP_K):
            y_k = jnp.concatenate(_unpack_bf16_pairs(buf[slot, k]), axis=1)
            y = y_k if y is None else y + y_k
        o_ref[...] = _layer_norm(alpha * h_ref[...] + y, g_ref[...], b_ref[...])

    @pl.when(i == 0)
    def _():
        def start(r, c):
            for k in range(TOP_K):
                row_copy(dest_ref, r, k, slot).start()
            return c
        lax.fori_loop(0, tm, start, 0)

    for k in range(TOP_K):
        pltpu.make_async_copy(y_hbm.at[pl.ds(0, tm), :], buf.at[slot, k], sem.at[slot]).wait()

    @pl.when(i < last)
    def _():
        for r in range(tm):
            for k in range(TOP_K):
                row_copy(dest_next_ref, r, k, 1 - slot).start()
        compute()

    @pl.when(i == last)
    def _():
        compute()


def _combine_ln(h, y_slots, dest, g, b, alpha):
    n, d = h.shape
    tm = _tile(n, 256)
    tiles = n // tm
    return pl.pallas_call(
        functools.partial(_combine_kernel, alpha=alpha),
        grid=(tiles,),
        in_specs=[pl.BlockSpec((tm * TOP_K,), lambda i: (i,), memory_space=pltpu.SMEM),
                  pl.BlockSpec((tm * TOP_K,), lambda i: (jnp.minimum(i + 1, tiles - 1),),
                               memory_space=pltpu.SMEM),
                  pl.BlockSpec(memory_space=pl.ANY),
                  pl.BlockSpec((tm, d), lambda i: (i, 0)),
                  pl.BlockSpec((1, d), lambda i: (0, 0)),
                  pl.BlockSpec((1, d), lambda i: (0, 0))],
        out_specs=pl.BlockSpec((tm, d), lambda i: (i, 0)),
        out_shape=jax.ShapeDtypeStruct((n, d), F32),
        scratch_shapes=[pltpu.VMEM((2, TOP_K, tm, d // 2), jnp.int32), pltpu.SemaphoreType.DMA((2,))],
        compiler_params=_params("arbitrary"),
        name="combine_ln",
    )(dest, dest, y_slots, h, g.reshape(1, d), b.reshape(1, d))


EXPERT_SLOT_BLOCK = 256
ML_CHUNK = 256
FOX_KEY_BLOCK = 512
FOX_UNROLL = 2


def _mixer(h32, h16, batch, seq, w_in, b_fox_f, b_ml_i, b_ml_f, conv_w, conv_b, ml_norm_g,
           w_proj_fox, w_proj_ml, w_out, expert_w):
    n, d = h32.shape
    fox_w = d // 2
    fox_heads = fox_w // FOX_HEAD_DIM
    qk_w = ML_HEADS * ML_QK_DIM
    v_w = ML_HEADS * ML_V_DIM
    widths = (fox_w, fox_w, fox_w, fox_heads, qk_w, qk_w, v_w, ML_HEADS, ML_HEADS, v_w, d, d)
    offs = [0]
    for w in widths:
        offs.append(offs[-1] + w)
    n_gate = fox_heads + 2 * ML_HEADS
    gate_bias = jnp.pad(jnp.concatenate([b_fox_f, b_ml_i, b_ml_f]), (0, GATE_LANES - n_gate)).reshape(1, -1)

    wt = jnp.transpose(w_in)
    proj_fox = _in_proj(h16, wt, offs[0], 3 * fox_w, BF16, "in_proj_fox")
    proj_qk = _in_proj(h16, wt, offs[4], 2 * qk_w, F32, "in_proj_qk")
    proj_mv = _in_proj(h16, wt, offs[6], v_w, BF16, "in_proj_mv")
    proj_og = _in_proj(h16, wt, offs[9], v_w + 2 * d, F32, "in_proj_og")
    gate_pre = _gate_proj(h16, wt, offs[3], offs[7], fox_heads)

    chunk = _tile(seq, ML_CHUNK)
    gates = _gates(gate_pre, gate_bias, batch, chunk, fox_heads)

    g3 = gates.reshape(batch, seq, GATE_LANES)
    tk = _tile(seq, FOX_KEY_BLOCK)
    c_heads = jnp.transpose(g3[:, :, :fox_heads], (0, 2, 1))
    c_rows = c_heads.reshape(batch, fox_heads, seq // tk, 1, tk)
    c_cols = c_heads.reshape(batch, fox_heads, seq, 1)
    lane_i, lane_b = fox_heads, fox_heads + ML_HEADS
    rows = lambda lo: jnp.transpose(g3[:, :, lo:lo + ML_HEADS], (0, 2, 1)).reshape(
        batch, ML_HEADS, seq // chunk, 1, chunk)
    irow, brow = rows(lane_i), rows(lane_b)

    v_t = jnp.transpose(proj_fox[:, 2 * fox_w:3 * fox_w].reshape(batch, seq, fox_heads, FOX_HEAD_DIM),
                        (0, 2, 3, 1))
    ones_row = (jnp.arange(BF16_SUBLANES) == 0).astype(BF16)[None, None, :, None]
    v_t = jnp.concatenate([v_t, jnp.broadcast_to(ones_row, v_t.shape[:2] + (BF16_SUBLANES, seq))], axis=2)
    later_w = [w_proj_fox, w_proj_ml, w_out] + list(expert_w)
    views = _cast_plan(later_w, batch * fox_heads * (seq // tk))
    y_fox, cast = _fox_attention(proj_fox, v_t, c_rows, c_cols, batch, seq, fox_heads,
                                 col_q=0, col_k=fox_heads, to_cast=views or [])
    if views:
        later_w16 = [c.reshape(a.shape) for c, a in zip(cast, later_w)]
    else:
        later_w16 = [a.astype(BF16) for a in later_w]
    w_proj_fox16, w_proj_ml16, w_out16 = later_w16[:3]
    qk = _conv_silu(proj_qk, conv_w, conv_b, seq, 2 * qk_w)
    y_ml = _mlstm(qk, proj_mv, proj_og, gates, brow, irow, ml_norm_g, batch, seq, chunk,
                  col_v=0, col_mo=0, lane_i=lane_i, lane_b=lane_b)
    merged = _merge(y_fox, y_ml, w_proj_fox16, w_proj_ml16, proj_og, col_ga=v_w, col_gb=v_w + d)
    return _matmul(merged, w_out16, F32, "out_proj"), later_w16[3:]


def _moe(h, h_packed, w_group, b_group, w_router, b_router, w_gate, w_up, w_down, ln_g, ln_b, alpha):
    n, d = h.shape
    n_groups = w_group.shape[1]
    n_exp = w_router.shape[1]
    lanes = GATE_LANES
    assert n_exp + n_groups <= lanes
    pad = lanes - n_exp - n_groups
    w_route = jnp.pad(jnp.concatenate([w_router, w_group], axis=1), ((0, 0), (0, pad)))
    b_route = jnp.pad(jnp.concatenate([b_router, b_group]), (0, pad)).reshape(1, lanes)
    oi, of, cnt = _router(h, w_route, b_route, n_groups, n_exp // n_groups)
    e_idx, rank, gate_w = oi[:, 0:TOP_K], oi[:, TOP_K:2 * TOP_K], of[:, 0:TOP_K]
    counts = cnt[0, :n_exp]

    tb = EXPERT_SLOT_BLOCK
    n_assign = n * TOP_K
    n_blocks = (n_assign + n_exp * (tb - 1) + tb - 1) // tb
    padded = (counts + tb - 1) // tb * tb
    pad_ends = jnp.cumsum(padded)
    dest = ((pad_ends - padded)[e_idx] + rank).reshape(-1)
    tok = jnp.repeat(jnp.arange(n, dtype=jnp.int32), TOP_K)
    pairs = jnp.stack([tok, lax.bitcast_convert_type(gate_w.reshape(-1), jnp.int32)], axis=1)
    slots = jnp.zeros((n_blocks * tb, 2), jnp.int32).at[dest].set(pairs)
    slot_tok = slots[:, 0]
    slot_w = lax.bitcast_convert_type(slots[:, 1], F32)
    n_used = pad_ends[-1] // tb
    blk = jnp.arange(n_blocks, dtype=jnp.int32)
    block_e = jnp.minimum(jnp.searchsorted(pad_ends, blk * tb, side="right"), n_exp - 1).astype(jnp.int32)
    block_e = jnp.where(blk < n_used, block_e, block_e[jnp.maximum(n_used - 1, 0)])

    y_slots = _experts(h_packed, slot_tok, slot_w, block_e, n_used.reshape(1).astype(jnp.int32),
                       w_gate, w_up, w_down, tb)
    return _combine_ln(h, y_slots, dest.astype(jnp.int32), ln_g, ln_b, alpha)


def kernel(x, ln_in_g, ln_in_b, w_in, b_fox_f, b_ml_i, b_ml_f, conv_w, conv_b, ml_norm_g, w_proj_fox, w_proj_ml, w_out, ln_mix_g, ln_mix_b, w_group, b_group, w_router, b_router, w_gate, w_up, w_down, ln_moe_g, ln_moe_b):
    batch, seq, d = x.shape
    depth = w_in.shape[0]
    alpha = (2 * depth) ** 0.25
    h32, h16 = _ln_in(x.reshape(batch * seq, d), ln_in_g, ln_in_b)
    for l in range(depth):
        mix, expert_w16 = _mixer(h32, h16, batch, seq, w_in[l], b_fox_f[l], b_ml_i[l], b_ml_f[l],
                                 conv_w[l], conv_b[l], ml_norm_g[l], w_proj_fox[l], w_proj_ml[l],
                                 w_out[l], [w_gate[l], w_up[l], w_down[l]])
        h32, h_packed = _ln_res(h32, mix, ln_mix_g[l], ln_mix_b[l], alpha)
        h32 = _moe(h32, h_packed, w_group[l], b_group[l], w_router[l], b_router[l], *expert_w16,
                   ln_moe_g[l], ln_moe_b[l], alpha)
        if l + 1 < depth:
            h16 = h32.astype(BF16)
    return h32.reshape(batch, seq, d)
```

```python
import functools

import jax
import jax.numpy as jnp
from jax import lax
from jax.experimental import pallas as pl
from jax.experimental.pallas import tpu as pltpu

F32 = jnp.float32
BF16 = jnp.bfloat16

LN_EPS = 1e-5
FOX_HEAD_DIM = 128
ML_HEADS = 4
ML_QK_DIM = 256
ML_V_DIM = 512
TOP_K = 2
LANES = 128
F32_SUBLANES = 8
BF16_SUBLANES = 16
GATE_LANES = LANES
LOG2_E = 1.4426950408889634

V7X_VMEM_BYTES = 64 * 1024 * 1024
VMEM_LIMIT_BYTES = V7X_VMEM_BYTES - 8 * 1024 * 1024


def _params(*sem):
    return pltpu.CompilerParams(dimension_semantics=sem, vmem_limit_bytes=VMEM_LIMIT_BYTES)


def _tile(dim, pref):
    t = min(dim, pref)
    while dim % t:
        t //= 2
    return t


def _layer_norm(x, g, b):
    mu = jnp.mean(x, axis=-1, keepdims=True)
    xc = x - mu
    var = jnp.mean(xc * xc, axis=-1, keepdims=True)
    return xc * lax.rsqrt(var + LN_EPS) * g + b


def _log_sigmoid(x):
    return jnp.minimum(x, 0.0) - jnp.log1p(jnp.exp(-jnp.abs(x)))


def _ln_in_kernel(x_ref, g_ref, b_ref, o16_ref):
    o16_ref[...] = _layer_norm(x_ref[...], g_ref[...], b_ref[...]).astype(BF16)


def _ln_in(x, g, b):
    n, d = x.shape
    tr = _tile(n, 256)
    row = pl.BlockSpec((tr, d), lambda i: (i, 0))
    vec = pl.BlockSpec((1, d), lambda i: (0, 0))
    return pl.pallas_call(
        _ln_in_kernel,
        grid=(n // tr,),
        in_specs=[row, vec, vec],
        out_specs=row,
        out_shape=jax.ShapeDtypeStruct((n, d), BF16),
        compiler_params=_params("parallel"),
        name="ln_in",
    )(x, g.reshape(1, d), b.reshape(1, d))


def _pack_bf16_pairs(x):
    half = x.shape[1] // 2
    bits = lambda v: lax.bitcast_convert_type(v.astype(BF16).astype(F32), jnp.uint32)
    word = lax.shift_right_logical(bits(x[:, :half]), jnp.uint32(16)) | bits(x[:, half:])
    return lax.bitcast_convert_type(word, jnp.int32)


def _unpack_bf16_pairs(w):
    u = lax.bitcast_convert_type(w, jnp.uint32)
    lo = lax.bitcast_convert_type(lax.shift_left(u, jnp.uint32(16)), F32)
    hi = lax.bitcast_convert_type(u & jnp.uint32(0xFFFF0000), F32)
    return lo, hi


def _ln_res_kernel(*refs, alpha, pre_norm):
    if pre_norm:
        h_ref, g0_ref, b0_ref, y_ref, g_ref, b_ref, o_ref, op_ref = refs
        h = _layer_norm(h_ref[...], g0_ref[...], b0_ref[...])
    else:
        h_ref, y_ref, g_ref, b_ref, o_ref, op_ref = refs
        h = h_ref[...]
    out = _layer_norm(alpha * h + y_ref[...], g_ref[...], b_ref[...])
    o_ref[...] = out
    op_ref[...] = _pack_bf16_pairs(out)


def _ln_res(h, y, g, b, alpha, pre_norm=None):
    n, d = h.shape
    tr = _tile(n, 256)
    row = pl.BlockSpec((tr, d), lambda i: (i, 0))
    vec = pl.BlockSpec((1, d), lambda i: (0, 0))
    pre = [v.reshape(1, d) for v in pre_norm] if pre_norm else []
    return pl.pallas_call(
        functools.partial(_ln_res_kernel, alpha=alpha, pre_norm=bool(pre_norm)),
        grid=(n // tr,),
        in_specs=[row] + [vec] * len(pre) + [row, vec, vec],
        out_specs=[row, pl.BlockSpec((tr, d // 2), lambda i: (i, 0))],
        out_shape=[jax.ShapeDtypeStruct((n, d), F32), jax.ShapeDtypeStruct((n, d // 2), jnp.int32)],
        compiler_params=_params("parallel"),
        name="ln_mix",
    )(h, *pre, y, g.reshape(1, d), b.reshape(1, d))


def _mm_kernel(x_ref, w_ref, o_ref):
    o_ref[...] = jnp.dot(x_ref[...], w_ref[...], preferred_element_type=F32).astype(o_ref.dtype)


def _matmul(x, w, out_dtype, name, bm=1024, bn=1024):
    m, k = x.shape
    _, n = w.shape
    bm, bn = _tile(m, bm), _tile(n, bn)
    return pl.pallas_call(
        _mm_kernel,
        grid=(m // bm, n // bn),
        in_specs=[pl.BlockSpec((bm, k), lambda i, j: (i, 0)),
                  pl.BlockSpec((k, bn), lambda i, j: (0, j))],
        out_specs=pl.BlockSpec((bm, bn), lambda i, j: (i, j)),
        out_shape=jax.ShapeDtypeStruct((m, n), out_dtype),
        compiler_params=_params("parallel", "parallel"),
        name=name,
    )(x, w)


_NT = (((1,), (1,)), ((), ()))


def _in_proj_kernel(*refs, off, n_col_blocks, features_major):
    if off:
        x_ref, w_ref, wx_ref, o_ref, w16_ref = refs
    else:
        x_ref, w_ref, o_ref, w16_ref = refs
    rc = w_ref.shape[0]
    j, i = pl.program_id(0), pl.program_id(1)

    @pl.when(j < n_col_blocks)
    def _():
        w = w_ref[...]
        if off:
            w = jnp.concatenate([w, wx_ref[...]], axis=0)[off:off + rc, :]
        w16_ref[j % 2, pl.ds(pl.multiple_of(i * rc, rc), rc), :] = w.astype(BF16)

    @pl.when(j > 0)
    def _():
        a, b = x_ref[...], w16_ref[(j - 1) % 2]
        if features_major:
            a, b = b, a
        o_ref[...] = lax.dot_general(a, b, _NT, preferred_element_type=F32).astype(o_ref.dtype)


def _in_proj(x, wt, col_start, n_cols, out_dtype, name, features_major=False):
    m, k = x.shape
    bm, bn = _tile(m, 1024), _tile(n_cols, 1024)
    ni, nj = m // bm, n_cols // bn
    rc = bn // ni
    assert bn % ni == 0 and rc % BF16_SUBLANES == 0
    off = col_start % rc
    assert off % F32_SUBLANES == 0
    blk0 = (col_start - off) // rc
    chunk = lambda j, i: blk0 + jnp.minimum(j, nj - 1) * ni + i
    row = lambda j, i: jnp.where(j == 0, 0, i)
    in_specs = [pl.BlockSpec((bm, k), lambda j, i: (row(j, i), 0)),
                pl.BlockSpec((rc, k), lambda j, i: (chunk(j, i), 0))]
    args = [x, wt]
    if off:
        in_specs.append(pl.BlockSpec((rc, k), lambda j, i: (chunk(j, i) + 1, 0)))
        args.append(wt)
    if features_major:
        out_spec = pl.BlockSpec((bn, bm), lambda j, i: (jnp.maximum(j - 1, 0), row(j, i)))
        out_shape = jax.ShapeDtypeStruct((n_cols, m), out_dtype)
    else:
        out_spec = pl.BlockSpec((bm, bn), lambda j, i: (row(j, i), jnp.maximum(j - 1, 0)))
        out_shape = jax.ShapeDtypeStruct((m, n_cols), out_dtype)
    return pl.pallas_call(
        functools.partial(_in_proj_kernel, off=off, n_col_blocks=nj, features_major=features_major),
        grid=(nj + 1, ni),
        in_specs=in_specs,
        out_specs=out_spec,
        out_shape=out_shape,
        scratch_shapes=[pltpu.VMEM((2, bn, k), BF16)],
        compiler_params=_params("arbitrary", "arbitrary"),
        name=name,
    )(*args)


def _gate_proj_kernel(x_ref, wa_ref, wb_ref, o_ref, w16_ref):
    @pl.when(pl.program_id(0) == 0)
    def _():
        pad = w16_ref.shape[0] - wa_ref.shape[0] - wb_ref.shape[0]
        zeros = jnp.zeros((pad, w16_ref.shape[1]), F32)
        w16_ref[...] = jnp.concatenate([wa_ref[...], wb_ref[...], zeros], axis=0).astype(BF16)

    o_ref[...] = lax.dot_general(x_ref[...], w16_ref[...], _NT, preferred_element_type=F32)


def _gate_proj(x, wt, col_ff, col_mi, n_fox):
    m, k = x.shape
    n_ml = 2 * ML_HEADS
    assert n_fox % F32_SUBLANES == 0 and n_ml % F32_SUBLANES == 0 and n_fox + n_ml <= LANES
    assert col_ff % n_fox == 0 and col_mi % n_ml == 0
    bm = _tile(m, 1024)
    return pl.pallas_call(
        _gate_proj_kernel,
        grid=(m // bm,),
        in_specs=[pl.BlockSpec((bm, k), lambda i: (i, 0)),
                  pl.BlockSpec((n_fox, k), lambda i: (col_ff // n_fox, 0)),
                  pl.BlockSpec((n_ml, k), lambda i: (col_mi // n_ml, 0))],
        out_specs=pl.BlockSpec((bm, LANES), lambda i: (i, 0)),
        out_shape=jax.ShapeDtypeStruct((m, LANES), F32),
        scratch_shapes=[pltpu.VMEM((LANES, k), BF16)],
        compiler_params=_params("arbitrary"),
        name="in_proj_gates",
    )(x, wt, wt)


def _gates_kernel(g_ref, bias_ref, o_ref, carry_ref, *, n_fox, fox_inv_scale):
    @pl.when(pl.program_id(1) == 0)
    def _():
        carry_ref[...] = jnp.zeros_like(carry_ref)

    ts = g_ref.shape[0]
    x = g_ref[...] + bias_ref[...]
    lane = lax.broadcasted_iota(jnp.int32, x.shape, 1)
    is_input_gate = (lane >= n_fox) & (lane < n_fox + ML_HEADS)
    val = jnp.where(is_input_gate, 0.0, _log_sigmoid(x))
    tril = (lax.broadcasted_iota(jnp.int32, (ts, ts), 0)
            >= lax.broadcasted_iota(jnp.int32, (ts, ts), 1)).astype(BF16)
    hi = val.astype(BF16)
    rem = val - hi.astype(F32)
    mid = rem.astype(BF16)
    lo = (rem - mid.astype(F32)).astype(BF16)
    cs = (jnp.dot(tril, hi, preferred_element_type=F32)
          + jnp.dot(tril, mid, preferred_element_type=F32)
          + jnp.dot(tril, lo, preferred_element_type=F32))
    total = cs + jnp.where(lane < n_fox, carry_ref[...], 0.0)
    carry_ref[...] = total[ts - 1:ts, :]
    total = jnp.where(lane < n_fox, total * fox_inv_scale, total)
    o_ref[...] = jnp.where(is_input_gate, x, total)


def _gates(g, bias, batch, ts, n_fox):
    n, lanes = g.shape
    tiles = n // batch // ts
    blk = pl.BlockSpec((ts, lanes), lambda b, j: (b * tiles + j, 0))
    return pl.pallas_call(
        functools.partial(_gates_kernel, n_fox=n_fox, fox_inv_scale=FOX_HEAD_DIM ** 0.5),
        grid=(batch, tiles),
        in_specs=[blk, pl.BlockSpec((1, lanes), lambda b, j: (0, 0))],
        out_specs=blk,
        out_shape=jax.ShapeDtypeStruct((n, lanes), F32),
        scratch_shapes=[pltpu.VMEM((1, lanes), F32)],
        compiler_params=_params("parallel", "arbitrary"),
        name="gates",
    )(g, bias)


def _conv_kernel(x_ref, prev_ref, w_ref, b_ref, o_ref, *, tiles_per_seq, k_blocks_from, k_scale):
    ts = x_ref.shape[0]
    kw = w_ref.shape[0]
    halo = prev_ref.shape[0]
    first = pl.program_id(0) % tiles_per_seq == 0
    prev = jnp.where(first, 0.0, prev_ref[...])
    ext = jnp.concatenate([prev, x_ref[...]], axis=0)
    w = w_ref[...]
    acc = None
    for j in range(kw):
        off = halo - (kw - 1) + j
        term = w[j:j + 1, :] * ext[off:off + ts, :]
        acc = term if acc is None else acc + term
    acc = acc + b_ref[...]
    y = acc * jax.nn.sigmoid(acc)
    scale = jnp.where(pl.program_id(1) >= k_blocks_from, k_scale, 1.0)
    o_ref[...] = y * scale


def _conv_silu(proj, conv_w, conv_b, seq, width):
    n = proj.shape[0]
    kw = conv_w.shape[0]
    ts, tc, halo = _tile(seq, 512), _tile(width // 2, 512), 8
    assert kw - 1 <= halo
    tiles_per_seq = seq // ts
    return pl.pallas_call(
        functools.partial(_conv_kernel, tiles_per_seq=tiles_per_seq,
                          k_blocks_from=(width // 2) // tc, k_scale=ML_QK_DIM ** -0.5),
        grid=(n // ts, width // tc),
        in_specs=[pl.BlockSpec((ts, tc), lambda i, j: (i, j)),
                  pl.BlockSpec((halo, tc), lambda i, j: (jnp.maximum(i * (ts // halo) - 1, 0), j)),
                  pl.BlockSpec((kw, tc), lambda i, j: (0, j)),
                  pl.BlockSpec((1, tc), lambda i, j: (0, j))],
        out_specs=pl.BlockSpec((ts, tc), lambda i, j: (i, j)),
        out_shape=jax.ShapeDtypeStruct((n, width), F32),
        compiler_params=_params("parallel", "parallel"),
        name="conv_silu",
    )(proj, proj, conv_w, conv_b.reshape(1, width))


def _fox_kernel(*refs, scale, n_cast):
    q_ref, k_ref, vt_ref, cq_ref, ck_ref = refs[:5]
    cast_in = refs[5:5 + n_cast]
    o_ref = refs[5 + n_cast]
    cast_out = refs[6 + n_cast:6 + 2 * n_cast]
    ua_ref, ub_ref, mxa_ref, mxb_ref, m_ref, acc_ref = refs[6 + 2 * n_cast:]
    for src, dst in zip(cast_in, cast_out):
        dst[...] = src[...].astype(dst.dtype)

    tq, dh = q_ref.shape
    tk = ua_ref.shape[0]
    qi = pl.program_id(2)
    c_exp = scale * LOG2_E
    q = q_ref[...]
    cq = cq_ref[...]
    buf_a, buf_b = (ua_ref, mxa_ref), (ub_ref, mxb_ref)

    def score(kb, buf, diag=None):
        u_ref, mx_ref = buf
        ks = pl.multiple_of(kb * tk, tk)
        u = lax.dot_general(k_ref[pl.ds(ks, tk), :], q, (((1,), (1,)), ((), ())),
                            preferred_element_type=F32) - ck_ref[pl.ds(ks, tk), :]
        if diag is not None:
            u = jnp.where(lax.broadcasted_iota(jnp.int32, (tk, tq), 0) + diag
                          <= lax.broadcasted_iota(jnp.int32, (tk, tq), 1), u, -jnp.inf)
        u_ref[...] = u
        mx_ref[...] = jnp.max(u, axis=0, keepdims=True)

    def absorb(kb, buf):
        u_ref, mx_ref = buf
        ks = pl.multiple_of(kb * tk, tk)
        m = m_ref[...]
        m_new = jnp.maximum(m, mx_ref[...] + cq)
        p = jnp.exp2((u_ref[...] + (cq - m_new)) * c_exp)
        a = jnp.exp2((m - m_new) * c_exp)
        ones = (lax.broadcasted_iota(jnp.int32, (BF16_SUBLANES, tk), 0) == 0).astype(BF16)
        v_aug = jnp.concatenate([vt_ref[:, pl.ds(ks, tk)], ones], axis=0)
        acc_ref[...] = a * acc_ref[...] + jnp.dot(v_aug, p.astype(BF16), preferred_element_type=F32)
        m_ref[...] = m_new

    m_ref[...] = jnp.full_like(m_ref, -jnp.inf)
    acc_ref[...] = jnp.zeros_like(acc_ref)

    @pl.when(qi == 0)
    def _():
        score(0, buf_a, diag=0)

    @pl.when(qi > 0)
    def _():
        score(0, buf_a)

        def pair(g, c):
            score(2 * g + 1, buf_b)
            absorb(2 * g, buf_a)
            score(2 * g + 2, buf_a)
            absorb(2 * g + 1, buf_b)
            return c

        lax.fori_loop(0, qi - 1, pair, 0)
        score(2 * qi - 1, buf_b)
        absorb(2 * qi - 2, buf_a)
        score(2 * qi, buf_a, diag=0)
        absorb(2 * qi - 1, buf_b)

    score(2 * qi + 1, buf_b, diag=tk)
    absorb(2 * qi, buf_a)
    absorb(2 * qi + 1, buf_b)
    o_ref[...] = jnp.transpose(acc_ref[0:dh, :] / acc_ref[dh:dh + 1, :]).astype(o_ref.dtype)


def _fox_attention(qk, vt, c_rows, c_cols, batch, seq, heads, col_q, col_k, to_cast):
    n = qk.shape[0]
    dh = FOX_HEAD_DIM
    tq = c_rows.shape[-1]
    nq = seq // tq
    steps = batch * heads * nq

    def slab(a):
        rb = _cast_rows(a.shape[0], steps)
        last = a.shape[0] // rb - 1
        return pl.BlockSpec((rb, a.shape[1]),
                            lambda b, h, i: (jnp.minimum((b * heads + h) * nq + i, last), 0))
    outs = pl.pallas_call(
        functools.partial(_fox_kernel, scale=dh ** -0.5, n_cast=len(to_cast)),
        grid=(batch, heads, nq),
        in_specs=[pl.BlockSpec((tq, dh), lambda b, h, i: (b * nq + i, col_q + h)),
                  pl.BlockSpec((seq, dh), lambda b, h, i: (b, col_k + h)),
                  pl.BlockSpec((dh, seq), lambda b, h, i: (h, b)),
                  pl.BlockSpec((None, None, None, 1, tq), lambda b, h, i: (b, h, i, 0, 0)),
                  pl.BlockSpec((None, None, seq, 1), lambda b, h, i: (b, h, 0, 0))]
                 + [slab(a) for a in to_cast],
        out_specs=[pl.BlockSpec((tq, dh), lambda b, h, i: (b * nq + i, h))] + [slab(a) for a in to_cast],
        out_shape=[jax.ShapeDtypeStruct((n, heads * dh), BF16)]
                  + [jax.ShapeDtypeStruct(a.shape, BF16) for a in to_cast],
        scratch_shapes=[pltpu.VMEM((tq // 2, tq), F32), pltpu.VMEM((tq // 2, tq), F32),
                        pltpu.VMEM((1, tq), F32), pltpu.VMEM((1, tq), F32),
                        pltpu.VMEM((1, tq), F32), pltpu.VMEM((dh + BF16_SUBLANES, tq), F32)],
        compiler_params=_params("arbitrary", "arbitrary", "arbitrary"),
        name="fox_attention",
    )(qk, qk, vt, c_rows, c_cols, *to_cast)
    return outs[0], outs[1:]


def _cast_rows(rows, steps):
    rb = max(BF16_SUBLANES, -(-rows // steps))
    return rb if rows % rb == 0 and rb % BF16_SUBLANES == 0 else None


def _cast_plan(arrays, steps):
    views = [a.reshape(-1, a.shape[-1]) for a in arrays]
    return views if all(_cast_rows(v.shape[0], steps) for v in views) else None


def _mlstm_kernel(q_ref, k_ref, v_ref, gate_ref, brow_ref, irow_ref, mo_ref, g_ref, o_ref,
                  c_ref, n_ref, m_ref, *, lane_i, lane_b):
    L = q_ref.shape[0]
    h = pl.program_id(1)

    @pl.when(pl.program_id(2) == 0)
    def _():
        c_ref[...] = jnp.zeros_like(c_ref)
        n_ref[...] = jnp.zeros_like(n_ref)
        m_ref[...] = jnp.zeros_like(m_ref)

    gates = gate_ref[...]
    lane = lax.broadcasted_iota(jnp.int32, gates.shape, 1)
    pick = lambda col: jnp.sum(jnp.where(lane == col + h, gates, 0.0), axis=-1, keepdims=True)
    icol = pick(lane_i)
    bcol = pick(lane_b)
    brow = brow_ref[...]
    irow = irow_ref[...]
    m_prev = m_ref[0:1, 0:1]

    q = q_ref[...].astype(BF16)
    k32 = k_ref[...]
    v = v_ref[...]

    tri = lax.broadcasted_iota(jnp.int32, (L, L), 0) >= lax.broadcasted_iota(jnp.int32, (L, L), 1)
    d = jnp.where(tri, bcol - brow + irow, -jnp.inf)
    inter = bcol + m_prev
    m = jnp.maximum(inter, jnp.max(d, axis=-1, keepdims=True))
    w_intra = jnp.exp(d - m)
    w_inter = jnp.exp(inter - m)
    s = lax.dot_general(q, k32.astype(BF16), (((1,), (1,)), ((), ())), preferred_element_type=F32)
    sc = s * w_intra
    num = (jnp.dot(sc.astype(BF16), v, preferred_element_type=F32)
           + w_inter * jnp.dot(q, c_ref[...].astype(BF16), preferred_element_type=F32))
    qn = jnp.sum(q_ref[...] * n_ref[...], axis=-1, keepdims=True)
    den = jnp.sum(sc, axis=-1, keepdims=True) + w_inter * qn
    hid = num / jnp.maximum(jnp.abs(den), jnp.exp(-m))

    b_last = bcol[L - 1:L, :]
    gcol = b_last - bcol + icol
    m_new = jnp.maximum(b_last + m_prev, jnp.max(gcol, axis=0, keepdims=True))
    w_k = jnp.exp(gcol - m_new)
    decay = jnp.exp(b_last + m_prev - m_new)
    kw = k32 * w_k
    c_ref[...] = decay * c_ref[...] + lax.dot_general(
        kw.astype(BF16), v, (((0,), (0,)), ((), ())), preferred_element_type=F32)
    n_ref[...] = decay * n_ref[...] + jnp.sum(kw, axis=0, keepdims=True)
    m_ref[...] = jnp.broadcast_to(m_new, m_ref.shape)

    mu = jnp.mean(hid, axis=-1, keepdims=True)
    hc = hid - mu
    var = jnp.mean(hc * hc, axis=-1, keepdims=True)
    hn = hc * lax.rsqrt(var + LN_EPS)
    o_ref[...] = (hn * g_ref[...] * jax.nn.sigmoid(mo_ref[...])).astype(o_ref.dtype)


def _mlstm(qk, qkv, proj, gates, brow, irow, norm_g, batch, seq, chunk, col_v, col_mo, lane_i, lane_b):
    n = qk.shape[0]
    dk, dv = ML_QK_DIM, ML_V_DIM
    nc = seq // chunk
    row = lambda b, h, c: b * nc + c
    vec = pl.BlockSpec((None, None, None, 1, chunk), lambda b, h, c: (b, h, c, 0, 0))
    return pl.pallas_call(
        functools.partial(_mlstm_kernel, lane_i=lane_i, lane_b=lane_b),
        grid=(batch, ML_HEADS, nc),
        in_specs=[pl.BlockSpec((chunk, dk), lambda b, h, c: (row(b, h, c), h)),
                  pl.BlockSpec((chunk, dk), lambda b, h, c: (row(b, h, c), ML_HEADS + h)),
                  pl.BlockSpec((chunk, dv), lambda b, h, c: (row(b, h, c), col_v + h)),
                  pl.BlockSpec((chunk, gates.shape[1]), lambda b, h, c: (row(b, h, c), 0)),
                  vec, vec,
                  pl.BlockSpec((chunk, dv), lambda b, h, c: (row(b, h, c), col_mo + h)),
                  pl.BlockSpec((1, dv), lambda b, h, c: (0, h))],
        out_specs=pl.BlockSpec((chunk, dv), lambda b, h, c: (row(b, h, c), h)),
        out_shape=jax.ShapeDtypeStruct((n, ML_HEADS * dv), BF16),
        scratch_shapes=[pltpu.VMEM((dk, dv), F32), pltpu.VMEM((1, dk), F32), pltpu.VMEM((8, 128), F32)],
        compiler_params=_params("parallel", "parallel", "arbitrary"),
        name="mlstm",
    )(qk, qk, qkv, gates, brow, irow, proj, norm_g.reshape(1, ML_HEADS * dv))


def _merge_kernel(yf_ref, ym_ref, wf_ref, wm_ref, ga_ref, gb_ref, o_ref):
    a = jnp.dot(yf_ref[...], wf_ref[...], preferred_element_type=F32)
    b = jnp.dot(ym_ref[...], wm_ref[...], preferred_element_type=F32)
    o_ref[...] = (jax.nn.sigmoid(ga_ref[...]) * a + jax.nn.sigmoid(gb_ref[...]) * b).astype(o_ref.dtype)


def _merge(y_fox, y_ml, w_fox, w_ml, proj, col_ga, col_gb):
    n, kf = y_fox.shape
    km = y_ml.shape[1]
    d = w_fox.shape[1]
    bm, bn = _tile(n, 512), _tile(d, 1024)
    ga0, gb0 = col_ga // bn, col_gb // bn
    return pl.pallas_call(
        _merge_kernel,
        grid=(n // bm, d // bn),
        in_specs=[pl.BlockSpec((bm, kf), lambda i, j: (i, 0)),
                  pl.BlockSpec((bm, km), lambda i, j: (i, 0)),
                  pl.BlockSpec((kf, bn), lambda i, j: (0, j)),
                  pl.BlockSpec((km, bn), lambda i, j: (0, j)),
                  pl.BlockSpec((bm, bn), lambda i, j: (i, ga0 + j)),
                  pl.BlockSpec((bm, bn), lambda i, j: (i, gb0 + j))],
        out_specs=pl.BlockSpec((bm, bn), lambda i, j: (i, j)),
        out_shape=jax.ShapeDtypeStruct((n, d), BF16),
        compiler_params=_params("parallel", "parallel"),
        name="merge_proj",
    )(y_fox, y_ml, w_fox, w_ml, proj, proj)


def _split_bf16(v):
    hi = v.astype(BF16)
    return hi, (v - hi.astype(F32)).astype(BF16)


def _router_kernel(x_ref, w_ref, b_ref, oi_ref, of_ref, cnt_ref, carry_ref, whi_ref, wlo_ref,
                   *, n_groups, per_group):
    @pl.when(pl.program_id(0) == 0)
    def _():
        carry_ref[...] = jnp.zeros_like(carry_ref)
        whi_ref[...], wlo_ref[...] = _split_bf16(w_ref[...])

    tr = x_ref.shape[0]
    n_exp = n_groups * per_group
    x_hi, x_lo = _split_bf16(x_ref[...])
    dot = lambda a, b: jnp.dot(a, b, preferred_element_type=F32)
    logits = (dot(x_hi, whi_ref[...]) + dot(x_lo, whi_ref[...]) + dot(x_hi, wlo_ref[...])) + b_ref[...]
    lane = lax.broadcasted_iota(jnp.int32, logits.shape, 1)
    big = jnp.int32(logits.shape[1])
    first_lane = lambda cond: jnp.min(jnp.where(cond, lane, big), axis=-1, keepdims=True)

    is_group = (lane >= n_exp) & (lane < n_exp + n_groups)
    gl = jnp.where(is_group, logits, -jnp.inf)
    g_max = jnp.max(gl, axis=-1, keepdims=True)
    g_sel = first_lane(gl == g_max) - n_exp
    p_g_sel = 1.0 / jnp.sum(jnp.exp(gl - g_max), axis=-1, keepdims=True)

    in_group = (lane >= g_sel * per_group) & (lane < (g_sel + 1) * per_group)
    el = jnp.where(in_group, logits, -jnp.inf)
    e_max = jnp.max(el, axis=-1, keepdims=True)
    ee = jnp.exp(el - e_max)
    pe = jnp.where(in_group, ee / jnp.sum(ee, axis=-1, keepdims=True), -1.0)
    p1 = jnp.max(pe, axis=-1, keepdims=True)
    e1 = first_lane(pe == p1)
    pe2 = jnp.where(lane == e1, -1.0, pe)
    p2 = jnp.max(pe2, axis=-1, keepdims=True)
    e2 = first_lane(pe2 == p2)
    p_sum = p1 + p2
    w1 = p_g_sel * p1 / p_sum
    w2 = p_g_sel * p2 / p_sum

    onehot = (lane == e1) | (lane == e2)
    strict = (lax.broadcasted_iota(jnp.int32, (tr, tr), 0)
              > lax.broadcasted_iota(jnp.int32, (tr, tr), 1)).astype(BF16)
    before = jnp.dot(strict, onehot.astype(BF16), preferred_element_type=F32) + carry_ref[...]
    r1 = jnp.sum(jnp.where(lane == e1, before, 0.0), axis=-1, keepdims=True).astype(jnp.int32)
    r2 = jnp.sum(jnp.where(lane == e2, before, 0.0), axis=-1, keepdims=True).astype(jnp.int32)
    carry_ref[...] += jnp.sum(onehot.astype(F32), axis=0, keepdims=True)
    cnt_ref[...] = carry_ref[...].astype(jnp.int32)

    oi_ref[...] = jnp.where(lane == 0, e1, jnp.where(lane == 1, e2, jnp.where(lane == 2, r1, r2)))
    of_ref[...] = jnp.where(lane == 0, w1, w2)


def _router(x, w_route, b_route, n_groups, per_group):
    n, d = x.shape
    lanes = w_route.shape[1]
    tr = _tile(n, 256)
    row = pl.BlockSpec((tr, lanes), lambda i: (i, 0))
    vec = pl.BlockSpec((1, lanes), lambda i: (0, 0))
    return pl.pallas_call(
        functools.partial(_router_kernel, n_groups=n_groups, per_group=per_group),
        grid=(n // tr,),
        in_specs=[pl.BlockSpec((tr, d), lambda i: (i, 0)),
                  pl.BlockSpec((d, lanes), lambda i: (0, 0)), vec],
        out_specs=[row, row, vec],
        out_shape=[jax.ShapeDtypeStruct((n, lanes), jnp.int32),
                   jax.ShapeDtypeStruct((n, lanes), F32),
                   jax.ShapeDtypeStruct((1, lanes), jnp.int32)],
        scratch_shapes=[pltpu.VMEM((1, lanes), F32),
                        pltpu.VMEM((d, lanes), BF16), pltpu.VMEM((d, lanes), BF16)],
        compiler_params=_params("arbitrary"),
        name="router",
    )(x, w_route, b_route)


def _expert_kernel(blk_e_ref, n_used_ref, tok_ref, tok_next_ref, x_hbm, wt_ref, wg_ref, wu_ref, wd_ref,
                   o_ref, xbuf, sem):
    b = pl.program_id(0)
    tb, half = xbuf.shape[1:]
    n_used = n_used_ref[0]
    slot = b % 2

    def row_copy(ids_ref, r, s):
        return pltpu.make_async_copy(x_hbm.at[pl.ds(ids_ref[r], 1), :],
                                     xbuf.at[s, pl.ds(r, 1), :], sem.at[s])

    def wait_rows():
        pltpu.make_async_copy(x_hbm.at[pl.ds(0, tb), :], xbuf.at[slot], sem.at[slot]).wait()

    def compute():
        lo, hi = _unpack_bf16_pairs(xbuf[slot])
        lo, hi = lo.astype(BF16), hi.astype(BF16)
        proj = lambda w_ref: (jnp.dot(lo, w_ref[0:half, :], preferred_element_type=F32)
                              + jnp.dot(hi, w_ref[half:2 * half, :], preferred_element_type=F32))
        gate, up = proj(wg_ref), proj(wu_ref)
        act = (gate * jax.nn.sigmoid(gate) * up).astype(BF16)
        out = jnp.dot(act, wd_ref[...], preferred_element_type=F32) * wt_ref[...]
        o_ref[...] = _pack_bf16_pairs(out)

    @pl.when((b == 0) & (n_used > 0))
    def _():
        def start(r, c):
            row_copy(tok_ref, r, slot).start()
            return c
        lax.fori_loop(0, tb, start, 0)

    @pl.when(b + 1 < n_used)
    def _():
        wait_rows()
        for r in range(tb):
            row_copy(tok_next_ref, r, 1 - slot).start()
        compute()

    @pl.when(b + 1 == n_used)
    def _():
        wait_rows()
        compute()

    @pl.when(b >= n_used)
    def _():
        o_ref[...] = jnp.zeros_like(o_ref)


def _experts(xp, slot_tok, slot_w, block_e, n_used, w_gate, w_up, w_down, tb):
    n, half = xp.shape
    _, d, f = w_gate.shape
    assert d == 2 * half
    n_blocks = block_e.shape[0]
    smem_tb = lambda shift: pl.BlockSpec(
        (tb,), lambda b, *_: (jnp.minimum(b + shift, n_blocks - 1),), memory_space=pltpu.SMEM)
    grid_spec = pltpu.PrefetchScalarGridSpec(
        num_scalar_prefetch=2,
        grid=(n_blocks,),
        in_specs=[smem_tb(0), smem_tb(1),
                  pl.BlockSpec(memory_space=pl.ANY),
                  pl.BlockSpec((tb, 1), lambda b, *_: (b, 0)),
                  pl.BlockSpec((None, d, f), lambda b, be, nu: (be[b], 0, 0)),
                  pl.BlockSpec((None, d, f), lambda b, be, nu: (be[b], 0, 0)),
                  pl.BlockSpec((None, f, d), lambda b, be, nu: (be[b], 0, 0))],
        out_specs=pl.BlockSpec((tb, half), lambda b, *_: (b, 0)),
        scratch_shapes=[pltpu.VMEM((2, tb, half), jnp.int32), pltpu.SemaphoreType.DMA((2,))],
    )
    return pl.pallas_call(
        _expert_kernel,
        grid_spec=grid_spec,
        out_shape=jax.ShapeDtypeStruct((n_blocks * tb, half), jnp.int32),
        compiler_params=_params("arbitrary"),
        name="experts",
    )(block_e, n_used, slot_tok, slot_tok, xp, slot_w.reshape(-1, 1), w_gate, w_up, w_down)


def _combine_kernel(dest_ref, dest_next_ref, y_hbm, h_ref, g_ref, b_ref, o_ref, buf, sem, *, alpha):
    i = pl.program_id(0)
    last = pl.num_programs(0) - 1
    tm = h_ref.shape[0]
    slot = i % 2

    def row_copy(ids_ref, r, k, s):
        return pltpu.make_async_copy(y_hbm.at[pl.ds(ids_ref[r * TOP_K + k], 1), :],
                                     buf.at[s, k, pl.ds(r, 1), :], sem.at[s])

    def compute():
        y = None
        for k in range(TOP_K):
            y_k = jnp.concatenate(_unpack_bf16_pairs(buf[slot, k]), axis=1)
            y = y_k if y is None else y + y_k
        o_ref[...] = _layer_norm(alpha * h_ref[...] + y, g_ref[...], b_ref[...])

    @pl.when(i == 0)
    def _():
        def start(r, c):
            for k in range(TOP_K):
                row_copy(dest_ref, r, k, slot).start()
            return c
        lax.fori_loop(0, tm, start, 0)

    for k in range(TOP_K):
        pltpu.make_async_copy(y_hbm.at[pl.ds(0, tm), :], buf.at[slot, k], sem.at[slot]).wait()

    @pl.when(i < last)
    def _():
        for r in range(tm):
            for k in range(TOP_K):
                row_copy(dest_next_ref, r, k, 1 - slot).start()
        compute()

    @pl.when(i == last)
    def _():
        compute()


def _combine_ln(h, y_slots, dest, g, b, alpha):
    n, d = h.shape
    tm = _tile(n, 256)
    tiles = n // tm
    return pl.pallas_call(
        functools.partial(_combine_kernel, alpha=alpha),
        grid=(tiles,),
        in_specs=[pl.BlockSpec((tm * TOP_K,), lambda i: (i,), memory_space=pltpu.SMEM),
                  pl.BlockSpec((tm * TOP_K,), lambda i: (jnp.minimum(i + 1, tiles - 1),),
                               memory_space=pltpu.SMEM),
                  pl.BlockSpec(memory_space=pl.ANY),
                  pl.BlockSpec((tm, d), lambda i: (i, 0)),
                  pl.BlockSpec((1, d), lambda i: (0, 0)),
                  pl.BlockSpec((1, d), lambda i: (0, 0))],
        out_specs=pl.BlockSpec((tm, d), lambda i: (i, 0)),
        out_shape=jax.ShapeDtypeStruct((n, d), F32),
        scratch_shapes=[pltpu.VMEM((2, TOP_K, tm, d // 2), jnp.int32), pltpu.SemaphoreType.DMA((2,))],
        compiler_params=_params("arbitrary"),
        name="combine_ln",
    )(dest, dest, y_slots, h, g.reshape(1, d), b.reshape(1, d))


EXPERT_SLOT_BLOCK = 256
ML_CHUNK = 256
FOX_KEY_BLOCK = 512


def _mixer(h16, batch, seq, w_in, b_fox_f, b_ml_i, b_ml_f, conv_w, conv_b, ml_norm_g,
           w_proj_fox, w_proj_ml, w_out, expert_w):
    n, d = h16.shape
    fox_w = d // 2
    fox_heads = fox_w // FOX_HEAD_DIM
    qk_w = ML_HEADS * ML_QK_DIM
    v_w = ML_HEADS * ML_V_DIM
    widths = (fox_w, fox_w, fox_w, fox_heads, qk_w, qk_w, v_w, ML_HEADS, ML_HEADS, v_w, d, d)
    offs = [0]
    for w in widths:
        offs.append(offs[-1] + w)
    n_gate = fox_heads + 2 * ML_HEADS
    gate_bias = jnp.pad(jnp.concatenate([b_fox_f, b_ml_i, b_ml_f]), (0, GATE_LANES - n_gate)).reshape(1, -1)

    wt = jnp.transpose(w_in)
    proj_fox = _in_proj(h16, wt, offs[0], 2 * fox_w, BF16, "in_proj_fox")
    v_t = _in_proj(h16, wt, offs[2], fox_w, BF16, "in_proj_fv", features_major=True)
    proj_qk = _in_proj(h16, wt, offs[4], 2 * qk_w, F32, "in_proj_qk")
    proj_mv = _in_proj(h16, wt, offs[6], v_w, BF16, "in_proj_mv")
    proj_og = _in_proj(h16, wt, offs[9], v_w + 2 * d, F32, "in_proj_og")
    gate_pre = _gate_proj(h16, wt, offs[3], offs[7], fox_heads)

    chunk = _tile(seq, ML_CHUNK)
    gates = _gates(gate_pre, gate_bias, batch, chunk, fox_heads)

    g3 = gates.reshape(batch, seq, GATE_LANES)
    tq = 2 * _tile(seq // 2, FOX_KEY_BLOCK)
    c_heads = jnp.transpose(g3[:, :, :fox_heads], (0, 2, 1))
    c_rows = c_heads.reshape(batch, fox_heads, seq // tq, 1, tq)
    c_cols = c_heads.reshape(batch, fox_heads, seq, 1)
    lane_i, lane_b = fox_heads, fox_heads + ML_HEADS
    rows = lambda lo: jnp.transpose(g3[:, :, lo:lo + ML_HEADS], (0, 2, 1)).reshape(
        batch, ML_HEADS, seq // chunk, 1, chunk)
    irow, brow = rows(lane_i), rows(lane_b)

    later_w = [w_proj_fox, w_proj_ml, w_out] + list(expert_w)
    views = _cast_plan(later_w, batch * fox_heads * (seq // tq))
    y_fox, cast = _fox_attention(proj_fox, v_t, c_rows, c_cols, batch, seq, fox_heads,
                                 col_q=0, col_k=fox_heads, to_cast=views or [])
    if views:
        later_w16 = [c.reshape(a.shape) for c, a in zip(cast, later_w)]
    else:
        later_w16 = [a.astype(BF16) for a in later_w]
    w_proj_fox16, w_proj_ml16, w_out16 = later_w16[:3]
    qk = _conv_silu(proj_qk, conv_w, conv_b, seq, 2 * qk_w)
    y_ml = _mlstm(qk, proj_mv, proj_og, gates, brow, irow, ml_norm_g, batch, seq, chunk,
                  col_v=0, col_mo=0, lane_i=lane_i, lane_b=lane_b)
    merged = _merge(y_fox, y_ml, w_proj_fox16, w_proj_ml16, proj_og, col_ga=v_w, col_gb=v_w + d)
    return _matmul(merged, w_out16, F32, "out_proj"), later_w16[3:]


def _moe(h, h_packed, w_group, b_group, w_router, b_router, w_gate, w_up, w_down, ln_g, ln_b, alpha):
    n, d = h.shape
    n_groups = w_group.shape[1]
    n_exp = w_router.shape[1]
    lanes = GATE_LANES
    assert n_exp + n_groups <= lanes
    pad = lanes - n_exp - n_groups
    w_route = jnp.pad(jnp.concatenate([w_router, w_group], axis=1), ((0, 0), (0, pad)))
    b_route = jnp.pad(jnp.concatenate([b_router, b_group]), (0, pad)).reshape(1, lanes)
    oi, of, cnt = _router(h, w_route, b_route, n_groups, n_exp // n_groups)
    e_idx, rank, gate_w = oi[:, 0:TOP_K], oi[:, TOP_K:2 * TOP_K], of[:, 0:TOP_K]
    counts = cnt[0, :n_exp]

    tb = EXPERT_SLOT_BLOCK
    n_assign = n * TOP_K
    n_blocks = (n_assign + n_exp * (tb - 1) + tb - 1) // tb
    padded = (counts + tb - 1) // tb * tb
    pad_ends = jnp.cumsum(padded)
    dest = ((pad_ends - padded)[e_idx] + rank).reshape(-1)
    tok = jnp.repeat(jnp.arange(n, dtype=jnp.int32), TOP_K)
    pairs = jnp.stack([tok, lax.bitcast_convert_type(gate_w.reshape(-1), jnp.int32)], axis=1)
    slots = jnp.zeros((n_blocks * tb, 2), jnp.int32).at[dest].set(pairs)
    slot_tok = slots[:, 0]
    slot_w = lax.bitcast_convert_type(slots[:, 1], F32)
    n_used = pad_ends[-1] // tb
    blk = jnp.arange(n_blocks, dtype=jnp.int32)
    block_e = jnp.minimum(jnp.searchsorted(pad_ends, blk * tb, side="right"), n_exp - 1).astype(jnp.int32)
    block_e = jnp.where(blk < n_used, block_e, block_e[jnp.maximum(n_used - 1, 0)])

    y_slots = _experts(h_packed, slot_tok, slot_w, block_e, n_used.reshape(1).astype(jnp.int32),
                       w_gate, w_up, w_down, tb)
    return _combine_ln(h, y_slots, dest.astype(jnp.int32), ln_g, ln_b, alpha)


def kernel(x, ln_in_g, ln_in_b, w_in, b_fox_f, b_ml_i, b_ml_f, conv_w, conv_b, ml_norm_g, w_proj_fox, w_proj_ml, w_out, ln_mix_g, ln_mix_b, w_group, b_group, w_router, b_router, w_gate, w_up, w_down, ln_moe_g, ln_moe_b):
    batch, seq, d = x.shape
    depth = w_in.shape[0]
    alpha = (2 * depth) ** 0.25
    h32 = x.reshape(batch * seq, d)
    h16 = _ln_in(h32, ln_in_g, ln_in_b)
    for l in range(depth):
        mix, expert_w16 = _mixer(h16, batch, seq, w_in[l], b_fox_f[l], b_ml_i[l], b_ml_f[l],
                                 conv_w[l], conv_b[l], ml_norm_g[l], w_proj_fox[l], w_proj_ml[l],
                                 w_out[l], [w_gate[l], w_up[l], w_down[l]])
        h32, h_packed = _ln_res(h32, mix, ln_mix_g[l], ln_mix_b[l], alpha,
                                pre_norm=(ln_in_g, ln_in_b) if l == 0 else None)
        h32 = _moe(h32, h_packed, w_group[l], b_group[l], w_router[l], b_router[l], *expert_w16,
                   ln_moe_g[l], ln_moe_b[l], alpha)
        if l + 1 < depth:
            h16 = h32.astype(BF16)
    return h32.reshape(batch, seq, d)
```

```python
import functools

import jax
import jax.numpy as jnp
from jax import lax
from jax.experimental import pallas as pl
from jax.experimental.pallas import tpu as pltpu

F32 = jnp.float32
BF16 = jnp.bfloat16

LN_EPS = 1e-5
FOX_HEAD_DIM = 128
ML_HEADS = 4
ML_QK_DIM = 256
ML_V_DIM = 512
TOP_K = 2
LANES = 128
F32_SUBLANES = 8
BF16_SUBLANES = 16
GATE_LANES = LANES
LOG2_E = 1.4426950408889634

V7X_VMEM_BYTES = 64 * 1024 * 1024
VMEM_LIMIT_BYTES = V7X_VMEM_BYTES - 8 * 1024 * 1024


def _params(*sem):
    return pltpu.CompilerParams(dimension_semantics=sem, vmem_limit_bytes=VMEM_LIMIT_BYTES)


def _tile(dim, pref):
    t = min(dim, pref)
    while dim % t:
        t //= 2
    return t


def _layer_norm(x, g, b):
    mu = jnp.mean(x, axis=-1, keepdims=True)
    xc = x - mu
    var = jnp.mean(xc * xc, axis=-1, keepdims=True)
    return xc * lax.rsqrt(var + LN_EPS) * g + b


def _log_sigmoid(x):
    return jnp.minimum(x, 0.0) - jnp.log1p(jnp.exp(-jnp.abs(x)))


def _ln_in_kernel(x_ref, g_ref, b_ref, o16_ref):
    o16_ref[...] = _layer_norm(x_ref[...], g_ref[...], b_ref[...]).astype(BF16)


def _ln_in(x, g, b):
    n, d = x.shape
    tr = _tile(n, 256)
    row = pl.BlockSpec((tr, d), lambda i: (i, 0))
    vec = pl.BlockSpec((1, d), lambda i: (0, 0))
    return pl.pallas_call(
        _ln_in_kernel,
        grid=(n // tr,),
        in_specs=[row, vec, vec],
        out_specs=row,
        out_shape=jax.ShapeDtypeStruct((n, d), BF16),
        compiler_params=_params("parallel"),
        name="ln_in",
    )(x, g.reshape(1, d), b.reshape(1, d))


def _pack_bf16_pairs(x):
    half = x.shape[1] // 2
    bits = lambda v: lax.bitcast_convert_type(v.astype(BF16).astype(F32), jnp.uint32)
    word = lax.shift_right_logical(bits(x[:, :half]), jnp.uint32(16)) | bits(x[:, half:])
    return lax.bitcast_convert_type(word, jnp.int32)


def _unpack_bf16_pairs(w):
    u = lax.bitcast_convert_type(w, jnp.uint32)
    lo = lax.bitcast_convert_type(lax.shift_left(u, jnp.uint32(16)), F32)
    hi = lax.bitcast_convert_type(u & jnp.uint32(0xFFFF0000), F32)
    return lo, hi


def _ln_route_kernel(*refs, alpha, pre_norm, n_groups, per_group):
    if pre_norm:
        h_ref, g0_ref, b0_ref, *refs = refs
        h = _layer_norm(h_ref[...], g0_ref[...], b0_ref[...])
    else:
        h_ref, *refs = refs
        h = h_ref[...]
    y_ref, g_ref, b_ref, w_ref, rb_ref, o_ref, op_ref, *route_refs = refs
    out = _layer_norm(alpha * h + y_ref[...], g_ref[...], b_ref[...])
    o_ref[...] = out
    op_ref[...] = _pack_bf16_pairs(out)
    _route_rows(out, w_ref, rb_ref, *route_refs, n_groups=n_groups, per_group=per_group)


def _ln_route(h, y, g, b, alpha, w_route, b_route, n_groups, per_group, pre_norm=None):
    n, d = h.shape
    lanes = w_route.shape[1]
    tr = _tile(n, 256)
    row = pl.BlockSpec((tr, d), lambda i: (i, 0))
    vec = pl.BlockSpec((1, d), lambda i: (0, 0))
    lrow = pl.BlockSpec((tr, lanes), lambda i: (i, 0))
    lvec = pl.BlockSpec((1, lanes), lambda i: (0, 0))
    pre = [v.reshape(1, d) for v in pre_norm] if pre_norm else []
    return pl.pallas_call(
        functools.partial(_ln_route_kernel, alpha=alpha, pre_norm=bool(pre_norm),
                          n_groups=n_groups, per_group=per_group),
        grid=(n // tr,),
        in_specs=[row] + [vec] * len(pre) + [row, vec, vec,
                                             pl.BlockSpec((d, lanes), lambda i: (0, 0)), lvec],
        out_specs=[row, pl.BlockSpec((tr, d // 2), lambda i: (i, 0)), lrow, lrow, lvec],
        out_shape=[jax.ShapeDtypeStruct((n, d), F32), jax.ShapeDtypeStruct((n, d // 2), jnp.int32),
                   jax.ShapeDtypeStruct((n, lanes), jnp.int32),
                   jax.ShapeDtypeStruct((n, lanes), F32),
                   jax.ShapeDtypeStruct((1, lanes), jnp.int32)],
        scratch_shapes=[pltpu.VMEM((1, lanes), F32),
                        pltpu.VMEM((d, lanes), BF16), pltpu.VMEM((d, lanes), BF16)],
        compiler_params=_params("arbitrary"),
        name="ln_mix_route",
    )(h, *pre, y, g.reshape(1, d), b.reshape(1, d), w_route, b_route)


def _mm_kernel(x_ref, w_ref, o_ref):
    o_ref[...] = jnp.dot(x_ref[...], w_ref[...], preferred_element_type=F32).astype(o_ref.dtype)


def _matmul(x, w, out_dtype, name, bm=1024, bn=1024):
    m, k = x.shape
    _, n = w.shape
    bm, bn = _tile(m, bm), _tile(n, bn)
    return pl.pallas_call(
        _mm_kernel,
        grid=(m // bm, n // bn),
        in_specs=[pl.BlockSpec((bm, k), lambda i, j: (i, 0)),
                  pl.BlockSpec((k, bn), lambda i, j: (0, j))],
        out_specs=pl.BlockSpec((bm, bn), lambda i, j: (i, j)),
        out_shape=jax.ShapeDtypeStruct((m, n), out_dtype),
        compiler_params=_params("parallel", "parallel"),
        name=name,
    )(x, w)


_NT = (((1,), (1,)), ((), ()))


def _in_proj_kernel(*refs, off, n_col_blocks, features_major):
    if off:
        x_ref, w_ref, wx_ref, o_ref, w16_ref = refs
    else:
        x_ref, w_ref, o_ref, w16_ref = refs
    rc = w_ref.shape[0]
    j, i = pl.program_id(0), pl.program_id(1)

    @pl.when(j < n_col_blocks)
    def _():
        w = w_ref[...]
        if off:
            w = jnp.concatenate([w, wx_ref[...]], axis=0)[off:off + rc, :]
        w16_ref[j % 2, pl.ds(pl.multiple_of(i * rc, rc), rc), :] = w.astype(BF16)

    @pl.when(j > 0)
    def _():
        a, b = x_ref[...], w16_ref[(j - 1) % 2]
        if features_major:
            a, b = b, a
        o_ref[...] = lax.dot_general(a, b, _NT, preferred_element_type=F32).astype(o_ref.dtype)


def _in_proj(x, wt, col_start, n_cols, out_dtype, name, features_major=False):
    m, k = x.shape
    bm, bn = _tile(m, 1024), _tile(n_cols, 1024)
    ni, nj = m // bm, n_cols // bn
    rc = bn // ni
    assert bn % ni == 0 and rc % BF16_SUBLANES == 0
    off = col_start % rc
    assert off % F32_SUBLANES == 0
    blk0 = (col_start - off) // rc
    chunk = lambda j, i: blk0 + jnp.minimum(j, nj - 1) * ni + i
    row = lambda j, i: jnp.where(j == 0, 0, i)
    in_specs = [pl.BlockSpec((bm, k), lambda j, i: (row(j, i), 0)),
                pl.BlockSpec((rc, k), lambda j, i: (chunk(j, i), 0))]
    args = [x, wt]
    if off:
        in_specs.append(pl.BlockSpec((rc, k), lambda j, i: (chunk(j, i) + 1, 0)))
        args.append(wt)
    if features_major:
        out_spec = pl.BlockSpec((bn, bm), lambda j, i: (jnp.maximum(j - 1, 0), row(j, i)))
        out_shape = jax.ShapeDtypeStruct((n_cols, m), out_dtype)
    else:
        out_spec = pl.BlockSpec((bm, bn), lambda j, i: (row(j, i), jnp.maximum(j - 1, 0)))
        out_shape = jax.ShapeDtypeStruct((m, n_cols), out_dtype)
    return pl.pallas_call(
        functools.partial(_in_proj_kernel, off=off, n_col_blocks=nj, features_major=features_major),
        grid=(nj + 1, ni),
        in_specs=in_specs,
        out_specs=out_spec,
        out_shape=out_shape,
        scratch_shapes=[pltpu.VMEM((2, bn, k), BF16)],
        compiler_params=_params("arbitrary", "arbitrary"),
        name=name,
    )(*args)


def _gate_proj_kernel(x_ref, wa_ref, wb_ref, o_ref, w16_ref):
    @pl.when(pl.program_id(0) == 0)
    def _():
        pad = w16_ref.shape[0] - wa_ref.shape[0] - wb_ref.shape[0]
        zeros = jnp.zeros((pad, w16_ref.shape[1]), F32)
        w16_ref[...] = jnp.concatenate([wa_ref[...], wb_ref[...], zeros], axis=0).astype(BF16)

    o_ref[...] = lax.dot_general(x_ref[...], w16_ref[...], _NT, preferred_element_type=F32)


def _gate_proj(x, wt, col_ff, col_mi, n_fox):
    m, k = x.shape
    n_ml = 2 * ML_HEADS
    assert n_fox % F32_SUBLANES == 0 and n_ml % F32_SUBLANES == 0 and n_fox + n_ml <= LANES
    assert col_ff % n_fox == 0 and col_mi % n_ml == 0
    bm = _tile(m, 1024)
    return pl.pallas_call(
        _gate_proj_kernel,
        grid=(m // bm,),
        in_specs=[pl.BlockSpec((bm, k), lambda i: (i, 0)),
                  pl.BlockSpec((n_fox, k), lambda i: (col_ff // n_fox, 0)),
                  pl.BlockSpec((n_ml, k), lambda i: (col_mi // n_ml, 0))],
        out_specs=pl.BlockSpec((bm, LANES), lambda i: (i, 0)),
        out_shape=jax.ShapeDtypeStruct((m, LANES), F32),
        scratch_shapes=[pltpu.VMEM((LANES, k), BF16)],
        compiler_params=_params("arbitrary"),
        name="in_proj_gates",
    )(x, wt, wt)


def _gates_kernel(g_ref, bias_ref, o_ref, carry_ref, *, n_fox, fox_inv_scale):
    @pl.when(pl.program_id(1) == 0)
    def _():
        carry_ref[...] = jnp.zeros_like(carry_ref)

    ts = g_ref.shape[0]
    x = g_ref[...] + bias_ref[...]
    lane = lax.broadcasted_iota(jnp.int32, x.shape, 1)
    is_input_gate = (lane >= n_fox) & (lane < n_fox + ML_HEADS)
    val = jnp.where(is_input_gate, 0.0, _log_sigmoid(x))
    tril = (lax.broadcasted_iota(jnp.int32, (ts, ts), 0)
            >= lax.broadcasted_iota(jnp.int32, (ts, ts), 1)).astype(BF16)
    hi = val.astype(BF16)
    rem = val - hi.astype(F32)
    mid = rem.astype(BF16)
    lo = (rem - mid.astype(F32)).astype(BF16)
    cs = (jnp.dot(tril, hi, preferred_element_type=F32)
          + jnp.dot(tril, mid, preferred_element_type=F32)
          + jnp.dot(tril, lo, preferred_element_type=F32))
    total = cs + jnp.where(lane < n_fox, carry_ref[...], 0.0)
    carry_ref[...] = total[ts - 1:ts, :]
    total = jnp.where(lane < n_fox, total * fox_inv_scale, total)
    o_ref[...] = jnp.where(is_input_gate, x, total)


def _gates(g, bias, batch, ts, n_fox):
    n, lanes = g.shape
    tiles = n // batch // ts
    blk = pl.BlockSpec((ts, lanes), lambda b, j: (b * tiles + j, 0))
    return pl.pallas_call(
        functools.partial(_gates_kernel, n_fox=n_fox, fox_inv_scale=FOX_HEAD_DIM ** 0.5),
        grid=(batch, tiles),
        in_specs=[blk, pl.BlockSpec((1, lanes), lambda b, j: (0, 0))],
        out_specs=blk,
        out_shape=jax.ShapeDtypeStruct((n, lanes), F32),
        scratch_shapes=[pltpu.VMEM((1, lanes), F32)],
        compiler_params=_params("parallel", "arbitrary"),
        name="gates",
    )(g, bias)


def _conv_kernel(x_ref, prev_ref, w_ref, b_ref, o_ref, *, tiles_per_seq, k_blocks_from, k_scale):
    ts = x_ref.shape[0]
    kw = w_ref.shape[0]
    halo = prev_ref.shape[0]
    first = pl.program_id(0) % tiles_per_seq == 0
    prev = jnp.where(first, 0.0, prev_ref[...])
    ext = jnp.concatenate([prev, x_ref[...]], axis=0)
    w = w_ref[...]
    acc = None
    for j in range(kw):
        off = halo - (kw - 1) + j
        term = w[j:j + 1, :] * ext[off:off + ts, :]
        acc = term if acc is None else acc + term
    acc = acc + b_ref[...]
    y = acc * jax.nn.sigmoid(acc)
    scale = jnp.where(pl.program_id(1) >= k_blocks_from, k_scale, 1.0)
    o_ref[...] = y * scale


def _conv_silu(proj, conv_w, conv_b, seq, width):
    n = proj.shape[0]
    kw = conv_w.shape[0]
    ts, tc, halo = _tile(seq, 512), _tile(width // 2, 512), 8
    assert kw - 1 <= halo
    tiles_per_seq = seq // ts
    return pl.pallas_call(
        functools.partial(_conv_kernel, tiles_per_seq=tiles_per_seq,
                          k_blocks_from=(width // 2) // tc, k_scale=ML_QK_DIM ** -0.5),
        grid=(n // ts, width // tc),
        in_specs=[pl.BlockSpec((ts, tc), lambda i, j: (i, j)),
                  pl.BlockSpec((halo, tc), lambda i, j: (jnp.maximum(i * (ts // halo) - 1, 0), j)),
                  pl.BlockSpec((kw, tc), lambda i, j: (0, j)),
                  pl.BlockSpec((1, tc), lambda i, j: (0, j))],
        out_specs=pl.BlockSpec((ts, tc), lambda i, j: (i, j)),
        out_shape=jax.ShapeDtypeStruct((n, width), F32),
        compiler_params=_params("parallel", "parallel"),
        name="conv_silu",
    )(proj, proj, conv_w, conv_b.reshape(1, width))


def _fox_kernel(*refs, scale, n_cast):
    q_ref, k_ref, vt_ref, cq_ref, ck_ref = refs[:5]
    cast_in = refs[5:5 + n_cast]
    o_ref = refs[5 + n_cast]
    cast_out = refs[6 + n_cast:6 + 2 * n_cast]
    ua_ref, ub_ref, mxa_ref, mxb_ref, m_ref, acc_ref = refs[6 + 2 * n_cast:]
    for src, dst in zip(cast_in, cast_out):
        dst[...] = src[...].astype(dst.dtype)

    tq, dh = q_ref.shape
    tk = ua_ref.shape[0]
    qi = pl.program_id(2)
    c_exp = scale * LOG2_E
    q = q_ref[...]
    cq = cq_ref[...]
    buf_a, buf_b = (ua_ref, mxa_ref), (ub_ref, mxb_ref)

    def score(kb, buf, diag=None):
        u_ref, mx_ref = buf
        ks = pl.multiple_of(kb * tk, tk)
        u = lax.dot_general(k_ref[pl.ds(ks, tk), :], q, (((1,), (1,)), ((), ())),
                            preferred_element_type=F32) - ck_ref[pl.ds(ks, tk), :]
        if diag is not None:
            u = jnp.where(lax.broadcasted_iota(jnp.int32, (tk, tq), 0) + diag
                          <= lax.broadcasted_iota(jnp.int32, (tk, tq), 1), u, -jnp.inf)
        u_ref[...] = u
        mx_ref[...] = jnp.max(u, axis=0, keepdims=True)

    def absorb(kb, buf):
        u_ref, mx_ref = buf
        ks = pl.multiple_of(kb * tk, tk)
        m = m_ref[...]
        m_new = jnp.maximum(m, mx_ref[...] + cq)
        p = jnp.exp2((u_ref[...] + (cq - m_new)) * c_exp)
        a = jnp.exp2((m - m_new) * c_exp)
        ones = (lax.broadcasted_iota(jnp.int32, (BF16_SUBLANES, tk), 0) == 0).astype(BF16)
        v_aug = jnp.concatenate([vt_ref[:, pl.ds(ks, tk)], ones], axis=0)
        acc_ref[...] = a * acc_ref[...] + jnp.dot(v_aug, p.astype(BF16), preferred_element_type=F32)
        m_ref[...] = m_new

    m_ref[...] = jnp.full_like(m_ref, -jnp.inf)
    acc_ref[...] = jnp.zeros_like(acc_ref)

    @pl.when(qi == 0)
    def _():
        score(0, buf_a, diag=0)

    @pl.when(qi > 0)
    def _():
        score(0, buf_a)

        def pair(g, c):
            score(2 * g + 1, buf_b)
            absorb(2 * g, buf_a)
            score(2 * g + 2, buf_a)
            absorb(2 * g + 1, buf_b)
            return c

        lax.fori_loop(0, qi - 1, pair, 0)
        score(2 * qi - 1, buf_b)
        absorb(2 * qi - 2, buf_a)
        score(2 * qi, buf_a, diag=0)
        absorb(2 * qi - 1, buf_b)

    score(2 * qi + 1, buf_b, diag=tk)
    absorb(2 * qi, buf_a)
    absorb(2 * qi + 1, buf_b)
    o_ref[...] = jnp.transpose(acc_ref[0:dh, :] / acc_ref[dh:dh + 1, :]).astype(o_ref.dtype)


def _fox_attention(qk, vt, c_rows, c_cols, batch, seq, heads, col_q, col_k, to_cast):
    n = qk.shape[0]
    dh = FOX_HEAD_DIM
    tq = c_rows.shape[-1]
    nq = seq // tq
    steps = batch * heads * nq

    def slab(a):
        rb = _cast_rows(a.shape[0], steps)
        last = a.shape[0] // rb - 1
        return pl.BlockSpec((rb, a.shape[1]),
                            lambda b, h, i: (jnp.minimum((b * heads + h) * nq + i, last), 0))
    outs = pl.pallas_call(
        functools.partial(_fox_kernel, scale=dh ** -0.5, n_cast=len(to_cast)),
        grid=(batch, heads, nq),
        in_specs=[pl.BlockSpec((tq, dh), lambda b, h, i: (b * nq + i, col_q + h)),
                  pl.BlockSpec((seq, dh), lambda b, h, i: (b, col_k + h)),
                  pl.BlockSpec((dh, seq), lambda b, h, i: (h, b)),
                  pl.BlockSpec((None, None, None, 1, tq), lambda b, h, i: (b, h, i, 0, 0)),
                  pl.BlockSpec((None, None, seq, 1), lambda b, h, i: (b, h, 0, 0))]
                 + [slab(a) for a in to_cast],
        out_specs=[pl.BlockSpec((tq, dh), lambda b, h, i: (b * nq + i, h))] + [slab(a) for a in to_cast],
        out_shape=[jax.ShapeDtypeStruct((n, heads * dh), BF16)]
                  + [jax.ShapeDtypeStruct(a.shape, BF16) for a in to_cast],
        scratch_shapes=[pltpu.VMEM((tq // 2, tq), F32), pltpu.VMEM((tq // 2, tq), F32),
                        pltpu.VMEM((1, tq), F32), pltpu.VMEM((1, tq), F32),
                        pltpu.VMEM((1, tq), F32), pltpu.VMEM((dh + BF16_SUBLANES, tq), F32)],
        compiler_params=_params("arbitrary", "arbitrary", "arbitrary"),
        name="fox_attention",
    )(qk, qk, vt, c_rows, c_cols, *to_cast)
    return outs[0], outs[1:]


def _cast_rows(rows, steps):
    rb = max(BF16_SUBLANES, -(-rows // steps))
    return rb if rows % rb == 0 and rb % BF16_SUBLANES == 0 else None


def _cast_plan(arrays, steps):
    views = [a.reshape(-1, a.shape[-1]) for a in arrays]
    return views if all(_cast_rows(v.shape[0], steps) for v in views) else None


def _mlstm_kernel(q_ref, k_ref, v_ref, gate_ref, brow_ref, irow_ref, mo_ref, g_ref, o_ref,
                  c_ref, n_ref, m_ref, *, lane_i, lane_b):
    L = q_ref.shape[0]
    h = pl.program_id(1)

    @pl.when(pl.program_id(2) == 0)
    def _():
        c_ref[...] = jnp.zeros_like(c_ref)
        n_ref[...] = jnp.zeros_like(n_ref)
        m_ref[...] = jnp.zeros_like(m_ref)

    gates = gate_ref[...]
    lane = lax.broadcasted_iota(jnp.int32, gates.shape, 1)
    pick = lambda col: jnp.sum(jnp.where(lane == col + h, gates, 0.0), axis=-1, keepdims=True)
    icol = pick(lane_i)
    bcol = pick(lane_b)
    brow = brow_ref[...]
    irow = irow_ref[...]
    m_prev = m_ref[0:1, 0:1]

    q = q_ref[...].astype(BF16)
    k32 = k_ref[...]
    v = v_ref[...]

    tri = lax.broadcasted_iota(jnp.int32, (L, L), 0) >= lax.broadcasted_iota(jnp.int32, (L, L), 1)
    d = jnp.where(tri, bcol - brow + irow, -jnp.inf)
    inter = bcol + m_prev
    m = jnp.maximum(inter, jnp.max(d, axis=-1, keepdims=True))
    w_intra = jnp.exp(d - m)
    w_inter = jnp.exp(inter - m)
    s = lax.dot_general(q, k32.astype(BF16), (((1,), (1,)), ((), ())), preferred_element_type=F32)
    sc = s * w_intra
    num = (jnp.dot(sc.astype(BF16), v, preferred_element_type=F32)
           + w_inter * jnp.dot(q, c_ref[...].astype(BF16), preferred_element_type=F32))
    qn = jnp.sum(q_ref[...] * n_ref[...], axis=-1, keepdims=True)
    den = jnp.sum(sc, axis=-1, keepdims=True) + w_inter * qn
    hid = num / jnp.maximum(jnp.abs(den), jnp.exp(-m))

    b_last = bcol[L - 1:L, :]
    gcol = b_last - bcol + icol
    m_new = jnp.maximum(b_last + m_prev, jnp.max(gcol, axis=0, keepdims=True))
    w_k = jnp.exp(gcol - m_new)
    decay = jnp.exp(b_last + m_prev - m_new)
    kw = k32 * w_k
    c_ref[...] = decay * c_ref[...] + lax.dot_general(
        kw.astype(BF16), v, (((0,), (0,)), ((), ())), preferred_element_type=F32)
    n_ref[...] = decay * n_ref[...] + jnp.sum(kw, axis=0, keepdims=True)
    m_ref[...] = jnp.broadcast_to(m_new, m_ref.shape)

    mu = jnp.mean(hid, axis=-1, keepdims=True)
    hc = hid - mu
    var = jnp.mean(hc * hc, axis=-1, keepdims=True)
    hn = hc * lax.rsqrt(var + LN_EPS)
    o_ref[...] = (hn * g_ref[...] * jax.nn.sigmoid(mo_ref[...])).astype(o_ref.dtype)


def _mlstm(qk, qkv, proj, gates, brow, irow, norm_g, batch, seq, chunk, col_v, col_mo, lane_i, lane_b):
    n = qk.shape[0]
    dk, dv = ML_QK_DIM, ML_V_DIM
    nc = seq // chunk
    row = lambda b, h, c: b * nc + c
    vec = pl.BlockSpec((None, None, None, 1, chunk), lambda b, h, c: (b, h, c, 0, 0))
    return pl.pallas_call(
        functools.partial(_mlstm_kernel, lane_i=lane_i, lane_b=lane_b),
        grid=(batch, ML_HEADS, nc),
        in_specs=[pl.BlockSpec((chunk, dk), lambda b, h, c: (row(b, h, c), h)),
                  pl.BlockSpec((chunk, dk), lambda b, h, c: (row(b, h, c), ML_HEADS + h)),
                  pl.BlockSpec((chunk, dv), lambda b, h, c: (row(b, h, c), col_v + h)),
                  pl.BlockSpec((chunk, gates.shape[1]), lambda b, h, c: (row(b, h, c), 0)),
                  vec, vec,
                  pl.BlockSpec((chunk, dv), lambda b, h, c: (row(b, h, c), col_mo + h)),
                  pl.BlockSpec((1, dv), lambda b, h, c: (0, h))],
        out_specs=pl.BlockSpec((chunk, dv), lambda b, h, c: (row(b, h, c), h)),
        out_shape=jax.ShapeDtypeStruct((n, ML_HEADS * dv), BF16),
        scratch_shapes=[pltpu.VMEM((dk, dv), F32), pltpu.VMEM((1, dk), F32), pltpu.VMEM((8, 128), F32)],
        compiler_params=_params("parallel", "parallel", "arbitrary"),
        name="mlstm",
    )(qk, qk, qkv, gates, brow, irow, proj, norm_g.reshape(1, ML_HEADS * dv))


def _merge_kernel(yf_ref, ym_ref, wf_ref, wm_ref, ga_ref, gb_ref, o_ref):
    a = jnp.dot(yf_ref[...], wf_ref[...], preferred_element_type=F32)
    b = jnp.dot(ym_ref[...], wm_ref[...], preferred_element_type=F32)
    o_ref[...] = (jax.nn.sigmoid(ga_ref[...]) * a + jax.nn.sigmoid(gb_ref[...]) * b).astype(o_ref.dtype)


def _merge(y_fox, y_ml, w_fox, w_ml, proj, col_ga, col_gb):
    n, kf = y_fox.shape
    km = y_ml.shape[1]
    d = w_fox.shape[1]
    bm, bn = _tile(n, 512), _tile(d, 1024)
    ga0, gb0 = col_ga // bn, col_gb // bn
    return pl.pallas_call(
        _merge_kernel,
        grid=(n // bm, d // bn),
        in_specs=[pl.BlockSpec((bm, kf), lambda i, j: (i, 0)),
                  pl.BlockSpec((bm, km), lambda i, j: (i, 0)),
                  pl.BlockSpec((kf, bn), lambda i, j: (0, j)),
                  pl.BlockSpec((km, bn), lambda i, j: (0, j)),
                  pl.BlockSpec((bm, bn), lambda i, j: (i, ga0 + j)),
                  pl.BlockSpec((bm, bn), lambda i, j: (i, gb0 + j))],
        out_specs=pl.BlockSpec((bm, bn), lambda i, j: (i, j)),
        out_shape=jax.ShapeDtypeStruct((n, d), BF16),
        compiler_params=_params("parallel", "parallel"),
        name="merge_proj",
    )(y_fox, y_ml, w_fox, w_ml, proj, proj)


def _split_bf16(v):
    hi = v.astype(BF16)
    return hi, (v - hi.astype(F32)).astype(BF16)


def _route_rows(x, w_ref, b_ref, oi_ref, of_ref, cnt_ref, carry_ref, whi_ref, wlo_ref,
                *, n_groups, per_group):
    @pl.when(pl.program_id(0) == 0)
    def _():
        carry_ref[...] = jnp.zeros_like(carry_ref)
        whi_ref[...], wlo_ref[...] = _split_bf16(w_ref[...])

    tr = x.shape[0]
    n_exp = n_groups * per_group
    x_hi, x_lo = _split_bf16(x)
    dot = lambda a, b: jnp.dot(a, b, preferred_element_type=F32)
    logits = (dot(x_hi, whi_ref[...]) + dot(x_lo, whi_ref[...]) + dot(x_hi, wlo_ref[...])) + b_ref[...]
    lane = lax.broadcasted_iota(jnp.int32, logits.shape, 1)
    big = jnp.int32(logits.shape[1])
    first_lane = lambda cond: jnp.min(jnp.where(cond, lane, big), axis=-1, keepdims=True)

    is_group = (lane >= n_exp) & (lane < n_exp + n_groups)
    gl = jnp.where(is_group, logits, -jnp.inf)
    g_max = jnp.max(gl, axis=-1, keepdims=True)
    g_sel = first_lane(gl == g_max) - n_exp
    p_g_sel = 1.0 / jnp.sum(jnp.exp(gl - g_max), axis=-1, keepdims=True)

    in_group = (lane >= g_sel * per_group) & (lane < (g_sel + 1) * per_group)
    el = jnp.where(in_group, logits, -jnp.inf)
    e_max = jnp.max(el, axis=-1, keepdims=True)
    ee = jnp.exp(el - e_max)
    pe = jnp.where(in_group, ee / jnp.sum(ee, axis=-1, keepdims=True), -1.0)
    p1 = jnp.max(pe, axis=-1, keepdims=True)
    e1 = first_lane(pe == p1)
    pe2 = jnp.where(lane == e1, -1.0, pe)
    p2 = jnp.max(pe2, axis=-1, keepdims=True)
    e2 = first_lane(pe2 == p2)
    p_sum = p1 + p2
    w1 = p_g_sel * p1 / p_sum
    w2 = p_g_sel * p2 / p_sum

    onehot = (lane == e1) | (lane == e2)
    strict = (lax.broadcasted_iota(jnp.int32, (tr, tr), 0)
              > lax.broadcasted_iota(jnp.int32, (tr, tr), 1)).astype(BF16)
    before = jnp.dot(strict, onehot.astype(BF16), preferred_element_type=F32) + carry_ref[...]
    r1 = jnp.sum(jnp.where(lane == e1, before, 0.0), axis=-1, keepdims=True).astype(jnp.int32)
    r2 = jnp.sum(jnp.where(lane == e2, before, 0.0), axis=-1, keepdims=True).astype(jnp.int32)
    carry_ref[...] += jnp.sum(onehot.astype(F32), axis=0, keepdims=True)
    cnt_ref[...] = carry_ref[...].astype(jnp.int32)

    oi_ref[...] = jnp.where(lane == 0, e1, jnp.where(lane == 1, e2, jnp.where(lane == 2, r1, r2)))
    of_ref[...] = jnp.where(lane == 0, w1, w2)


def _expert_kernel(blk_e_ref, n_used_ref, tok_ref, tok_next_ref, x_hbm, wt_ref, wg_ref, wu_ref, wd_ref,
                   o_ref, xbuf, sem):
    b = pl.program_id(0)
    tb, half = xbuf.shape[1:]
    n_used = n_used_ref[0]
    slot = b % 2

    def row_copy(ids_ref, r, s):
        return pltpu.make_async_copy(x_hbm.at[pl.ds(ids_ref[r], 1), :],
                                     xbuf.at[s, pl.ds(r, 1), :], sem.at[s])

    def wait_rows():
        pltpu.make_async_copy(x_hbm.at[pl.ds(0, tb), :], xbuf.at[slot], sem.at[slot]).wait()

    def compute():
        lo, hi = _unpack_bf16_pairs(xbuf[slot])
        lo, hi = lo.astype(BF16), hi.astype(BF16)
        proj = lambda w_ref: (jnp.dot(lo, w_ref[0:half, :], preferred_element_type=F32)
                              + jnp.dot(hi, w_ref[half:2 * half, :], preferred_element_type=F32))
        gate, up = proj(wg_ref), proj(wu_ref)
        act = (gate * jax.nn.sigmoid(gate) * up).astype(BF16)
        out = jnp.dot(act, wd_ref[...], preferred_element_type=F32) * wt_ref[...]
        o_ref[...] = _pack_bf16_pairs(out)

    @pl.when((b == 0) & (n_used > 0))
    def _():
        def start(r, c):
            row_copy(tok_ref, r, slot).start()
            return c
        lax.fori_loop(0, tb, start, 0)

    @pl.when(b + 1 < n_used)
    def _():
        wait_rows()
        for r in range(tb):
            row_copy(tok_next_ref, r, 1 - slot).start()
        compute()

    @pl.when(b + 1 == n_used)
    def _():
        wait_rows()
        compute()

    @pl.when(b >= n_used)
    def _():
        o_ref[...] = jnp.zeros_like(o_ref)


def _experts(xp, slot_tok, slot_w, block_e, n_used, w_gate, w_up, w_down, tb):
    n, half = xp.shape
    _, d, f = w_gate.shape
    assert d == 2 * half
    n_blocks = block_e.shape[0]
    smem_tb = lambda shift: pl.BlockSpec(
        (tb,), lambda b, *_: (jnp.minimum(b + shift, n_blocks - 1),), memory_space=pltpu.SMEM)
    grid_spec = pltpu.PrefetchScalarGridSpec(
        num_scalar_prefetch=2,
        grid=(n_blocks,),
        in_specs=[smem_tb(0), smem_tb(1),
                  pl.BlockSpec(memory_space=pl.ANY),
                  pl.BlockSpec((tb, 1), lambda b, *_: (b, 0)),
                  pl.BlockSpec((None, d, f), lambda b, be, nu: (be[b], 0, 0)),
                  pl.BlockSpec((None, d, f), lambda b, be, nu: (be[b], 0, 0)),
                  pl.BlockSpec((None, f, d), lambda b, be, nu: (be[b], 0, 0))],
        out_specs=pl.BlockSpec((tb, half), lambda b, *_: (b, 0)),
        scratch_shapes=[pltpu.VMEM((2, tb, half), jnp.int32), pltpu.SemaphoreType.DMA((2,))],
    )
    return pl.pallas_call(
        _expert_kernel,
        grid_spec=grid_spec,
        out_shape=jax.ShapeDtypeStruct((n_blocks * tb, half), jnp.int32),
        compiler_params=_params("arbitrary"),
        name="experts",
    )(block_e, n_used, slot_tok, slot_tok, xp, slot_w.reshape(-1, 1), w_gate, w_up, w_down)


def _combine_kernel(dest_ref, dest_next_ref, y_hbm, h_ref, g_ref, b_ref, o_ref, buf, sem, *, alpha):
    i = pl.program_id(0)
    last = pl.num_programs(0) - 1
    tm = h_ref.shape[0]
    slot = i % 2

    def row_copy(ids_ref, r, k, s):
        return pltpu.make_async_copy(y_hbm.at[pl.ds(ids_ref[r * TOP_K + k], 1), :],
                                     buf.at[s, k, pl.ds(r, 1), :], sem.at[s])

    def compute():
        y = None
        for k in range(TOP_K):
            y_k = jnp.concatenate(_unpack_bf16_pairs(buf[slot, k]), axis=1)
            y = y_k if y is None else y + y_k
        o_ref[...] = _layer_norm(alpha * h_ref[...] + y, g_ref[...], b_ref[...])

    @pl.when(i == 0)
    def _():
        def start(r, c):
            for k in range(TOP_K):
                row_copy(dest_ref, r, k, slot).start()
            return c
        lax.fori_loop(0, tm, start, 0)

    for k in range(TOP_K):
        pltpu.make_async_copy(y_hbm.at[pl.ds(0, tm), :], buf.at[slot, k], sem.at[slot]).wait()

    @pl.when(i < last)
    def _():
        for r in range(tm):
            for k in range(TOP_K):
                row_copy(dest_next_ref, r, k, 1 - slot).start()
        compute()

    @pl.when(i == last)
    def _():
        compute()


def _combine_ln(h, y_slots, dest, g, b, alpha):
    n, d = h.shape
    tm = _tile(n, 256)
    tiles = n // tm
    return pl.pallas_call(
        functools.partial(_combine_kernel, alpha=alpha),
        grid=(tiles,),
        in_specs=[pl.BlockSpec((tm * TOP_K,), lambda i: (i,), memory_space=pltpu.SMEM),
                  pl.BlockSpec((tm * TOP_K,), lambda i: (jnp.minimum(i + 1, tiles - 1),),
                               memory_space=pltpu.SMEM),
                  pl.BlockSpec(memory_space=pl.ANY),
                  pl.BlockSpec((tm, d), lambda i: (i, 0)),
                  pl.BlockSpec((1, d), lambda i: (0, 0)),
                  pl.BlockSpec((1, d), lambda i: (0, 0))],
        out_specs=pl.BlockSpec((tm, d), lambda i: (i, 0)),
        out_shape=jax.ShapeDtypeStruct((n, d), F32),
        scratch_shapes=[pltpu.VMEM((2, TOP_K, tm, d // 2), jnp.int32), pltpu.SemaphoreType.DMA((2,))],
        compiler_params=_params("arbitrary"),
        name="combine_ln",
    )(dest, dest, y_slots, h, g.reshape(1, d), b.reshape(1, d))


EXPERT_SLOT_BLOCK = 256
ML_CHUNK = 256
FOX_KEY_BLOCK = 512


def _mixer(h16, batch, seq, w_in, b_fox_f, b_ml_i, b_ml_f, conv_w, conv_b, ml_norm_g,
           w_proj_fox, w_proj_ml, w_out, expert_w):
    n, d = h16.shape
    fox_w = d // 2
    fox_heads = fox_w // FOX_HEAD_DIM
    qk_w = ML_HEADS * ML_QK_DIM
    v_w = ML_HEADS * ML_V_DIM
    widths = (fox_w, fox_w, fox_w, fox_heads, qk_w, qk_w, v_w, ML_HEADS, ML_HEADS, v_w, d, d)
    offs = [0]
    for w in widths:
        offs.append(offs[-1] + w)
    n_gate = fox_heads + 2 * ML_HEADS
    gate_bias = jnp.pad(jnp.concatenate([b_fox_f, b_ml_i, b_ml_f]), (0, GATE_LANES - n_gate)).reshape(1, -1)

    wt = jnp.transpose(w_in)
    proj_fox = _in_proj(h16, wt, offs[0], 2 * fox_w, BF16, "in_proj_fox")
    v_t = _in_proj(h16, wt, offs[2], fox_w, BF16, "in_proj_fv", features_major=True)
    proj_qk = _in_proj(h16, wt, offs[4], 2 * qk_w, F32, "in_proj_qk")
    proj_mv = _in_proj(h16, wt, offs[6], v_w, BF16, "in_proj_mv")
    proj_og = _in_proj(h16, wt, offs[9], v_w + 2 * d, F32, "in_proj_og")
    gate_pre = _gate_proj(h16, wt, offs[3], offs[7], fox_heads)

    chunk = _tile(seq, ML_CHUNK)
    gates = _gates(gate_pre, gate_bias, batch, chunk, fox_heads)

    g3 = gates.reshape(batch, seq, GATE_LANES)
    tq = 2 * _tile(seq // 2, FOX_KEY_BLOCK)
    c_heads = jnp.transpose(g3[:, :, :fox_heads], (0, 2, 1))
    c_rows = c_heads.reshape(batch, fox_heads, seq // tq, 1, tq)
    c_cols = c_heads.reshape(batch, fox_heads, seq, 1)
    lane_i, lane_b = fox_heads, fox_heads + ML_HEADS
    rows = lambda lo: jnp.transpose(g3[:, :, lo:lo + ML_HEADS], (0, 2, 1)).reshape(
        batch, ML_HEADS, seq // chunk, 1, chunk)
    irow, brow = rows(lane_i), rows(lane_b)

    later_w = [w_proj_fox, w_proj_ml, w_out] + list(expert_w)
    views = _cast_plan(later_w, batch * fox_heads * (seq // tq))
    y_fox, cast = _fox_attention(proj_fox, v_t, c_rows, c_cols, batch, seq, fox_heads,
                                 col_q=0, col_k=fox_heads, to_cast=views or [])
    if views:
        later_w16 = [c.reshape(a.shape) for c, a in zip(cast, later_w)]
    else:
        later_w16 = [a.astype(BF16) for a in later_w]
    w_proj_fox16, w_proj_ml16, w_out16 = later_w16[:3]
    qk = _conv_silu(proj_qk, conv_w, conv_b, seq, 2 * qk_w)
    y_ml = _mlstm(qk, proj_mv, proj_og, gates, brow, irow, ml_norm_g, batch, seq, chunk,
                  col_v=0, col_mo=0, lane_i=lane_i, lane_b=lane_b)
    merged = _merge(y_fox, y_ml, w_proj_fox16, w_proj_ml16, proj_og, col_ga=v_w, col_gb=v_w + d)
    return _matmul(merged, w_out16, F32, "out_proj"), later_w16[3:]


def _route_params(w_group, b_group, w_router, b_router):
    n_groups, n_exp = w_group.shape[1], w_router.shape[1]
    assert n_exp + n_groups <= GATE_LANES
    pad = GATE_LANES - n_exp - n_groups
    w_route = jnp.pad(jnp.concatenate([w_router, w_group], axis=1), ((0, 0), (0, pad)))
    b_route = jnp.pad(jnp.concatenate([b_router, b_group]), (0, pad)).reshape(1, GATE_LANES)
    return w_route, b_route, n_groups, n_exp // n_groups


def _moe(h, h_packed, routing, n_exp, w_gate, w_up, w_down, ln_g, ln_b, alpha):
    n, d = h.shape
    oi, of, cnt = routing
    e_idx, rank, gate_w = oi[:, 0:TOP_K], oi[:, TOP_K:2 * TOP_K], of[:, 0:TOP_K]
    counts = cnt[0, :n_exp]

    tb = EXPERT_SLOT_BLOCK
    n_assign = n * TOP_K
    n_blocks = (n_assign + n_exp * (tb - 1) + tb - 1) // tb
    padded = (counts + tb - 1) // tb * tb
    pad_ends = jnp.cumsum(padded)
    onehot = (e_idx[..., None] == jnp.arange(n_exp, dtype=jnp.int32)).astype(F32)
    seg_start = jnp.einsum("nke,e->nk", onehot, (pad_ends - padded).astype(F32),
                           precision=lax.Precision.HIGHEST).astype(jnp.int32)
    dest = (seg_start + rank).reshape(-1)
    tok = jnp.repeat(jnp.arange(n, dtype=jnp.int32), TOP_K)
    pairs = jnp.stack([tok, lax.bitcast_convert_type(gate_w.reshape(-1), jnp.int32)], axis=1)
    slots = jnp.zeros((n_blocks * tb, 2), jnp.int32).at[dest].set(pairs)
    slot_tok = slots[:, 0]
    slot_w = lax.bitcast_convert_type(slots[:, 1], F32)
    n_used = pad_ends[-1] // tb
    blk = jnp.arange(n_blocks, dtype=jnp.int32)
    block_e = jnp.minimum(jnp.searchsorted(pad_ends, blk * tb, side="right"), n_exp - 1).astype(jnp.int32)
    block_e = jnp.where(blk < n_used, block_e, block_e[jnp.maximum(n_used - 1, 0)])

    y_slots = _experts(h_packed, slot_tok, slot_w, block_e, n_used.reshape(1).astype(jnp.int32),
                       w_gate, w_up, w_down, tb)
    return _combine_ln(h, y_slots, dest.astype(jnp.int32), ln_g, ln_b, alpha)


def kernel(x, ln_in_g, ln_in_b, w_in, b_fox_f, b_ml_i, b_ml_f, conv_w, conv_b, ml_norm_g, w_proj_fox, w_proj_ml, w_out, ln_mix_g, ln_mix_b, w_group, b_group, w_router, b_router, w_gate, w_up, w_down, ln_moe_g, ln_moe_b):
    batch, seq, d = x.shape
    depth = w_in.shape[0]
    alpha = (2 * depth) ** 0.25
    h32 = x.reshape(batch * seq, d)
    h16 = _ln_in(h32, ln_in_g, ln_in_b)
    for l in range(depth):
        mix, expert_w16 = _mixer(h16, batch, seq, w_in[l], b_fox_f[l], b_ml_i[l], b_ml_f[l],
                                 conv_w[l], conv_b[l], ml_norm_g[l], w_proj_fox[l], w_proj_ml[l],
                                 w_out[l], [w_gate[l], w_up[l], w_down[l]])
        w_route, b_route, n_groups, per_group = _route_params(w_group[l], b_group[l],
                                                              w_router[l], b_router[l])
        h32, h_packed, *routing = _ln_route(h32, mix, ln_mix_g[l], ln_mix_b[l], alpha,
                                            w_route, b_route, n_groups, per_group,
                                            pre_norm=(ln_in_g, ln_in_b) if l == 0 else None)
        h32 = _moe(h32, h_packed, routing, n_groups * per_group, *expert_w16,
                   ln_moe_g[l], ln_moe_b[l], alpha)
        if l + 1 < depth:
            h16 = h32.astype(BF16)
    return h32.reshape(batch, seq, d)
```

```python
import functools

import jax
import jax.numpy as jnp
from jax import lax
from jax.experimental import pallas as pl
from jax.experimental.pallas import tpu as pltpu

F32 = jnp.float32
BF16 = jnp.bfloat16

LN_EPS = 1e-5
FOX_HEAD_DIM = 128
ML_HEADS = 4
ML_QK_DIM = 256
ML_V_DIM = 512
TOP_K = 2
LANES = 128
F32_SUBLANES = 8
BF16_SUBLANES = 16
GATE_LANES = LANES
LOG2_E = 1.4426950408889634

V7X_VMEM_BYTES = 64 * 1024 * 1024
VMEM_LIMIT_BYTES = V7X_VMEM_BYTES - 8 * 1024 * 1024


def _params(*sem):
    return pltpu.CompilerParams(dimension_semantics=sem, vmem_limit_bytes=VMEM_LIMIT_BYTES)


def _tile(dim, pref):
    t = min(dim, pref)
    while dim % t:
        t //= 2
    return t


def _layer_norm(x, g, b):
    mu = jnp.mean(x, axis=-1, keepdims=True)
    xc = x - mu
    var = jnp.mean(xc * xc, axis=-1, keepdims=True)
    return xc * lax.rsqrt(var + LN_EPS) * g + b


def _log_sigmoid(x):
    return jnp.minimum(x, 0.0) - jnp.log1p(jnp.exp(-jnp.abs(x)))


def _ln_in_kernel(x_ref, g_ref, b_ref, o16_ref):
    o16_ref[...] = _layer_norm(x_ref[...], g_ref[...], b_ref[...]).astype(BF16)


def _ln_in(x, g, b):
    n, d = x.shape
    tr = _tile(n, 256)
    row = pl.BlockSpec((tr, d), lambda i: (i, 0))
    vec = pl.BlockSpec((1, d), lambda i: (0, 0))
    return pl.pallas_call(
        _ln_in_kernel,
        grid=(n // tr,),
        in_specs=[row, vec, vec],
        out_specs=row,
        out_shape=jax.ShapeDtypeStruct((n, d), BF16),
        compiler_params=_params("parallel"),
        name="ln_in",
    )(x, g.reshape(1, d), b.reshape(1, d))


def _pack_bf16_pairs(x):
    half = x.shape[1] // 2
    bits = lambda v: lax.bitcast_convert_type(v.astype(BF16).astype(F32), jnp.uint32)
    word = lax.shift_right_logical(bits(x[:, :half]), jnp.uint32(16)) | bits(x[:, half:])
    return lax.bitcast_convert_type(word, jnp.int32)


def _unpack_bf16_pairs(w):
    u = lax.bitcast_convert_type(w, jnp.uint32)
    lo = lax.bitcast_convert_type(lax.shift_left(u, jnp.uint32(16)), F32)
    hi = lax.bitcast_convert_type(u & jnp.uint32(0xFFFF0000), F32)
    return lo, hi


def _ln_route_kernel(*refs, alpha, pre_norm, n_groups, per_group):
    if pre_norm:
        h_ref, g0_ref, b0_ref, *refs = refs
        h = _layer_norm(h_ref[...], g0_ref[...], b0_ref[...])
    else:
        h_ref, *refs = refs
        h = h_ref[...]
    y_ref, g_ref, b_ref, w_ref, rb_ref, o_ref, op_ref, *route_refs = refs
    out = _layer_norm(alpha * h + y_ref[...], g_ref[...], b_ref[...])
    o_ref[...] = out
    op_ref[...] = _pack_bf16_pairs(out)
    _route_rows(out, w_ref, rb_ref, *route_refs, n_groups=n_groups, per_group=per_group)


def _ln_route(h, y, g, b, alpha, w_route, b_route, n_groups, per_group, pre_norm=None):
    n, d = h.shape
    lanes = w_route.shape[1]
    tr = _tile(n, 256)
    row = pl.BlockSpec((tr, d), lambda i: (i, 0))
    vec = pl.BlockSpec((1, d), lambda i: (0, 0))
    lrow = pl.BlockSpec((tr, lanes), lambda i: (i, 0))
    lvec = pl.BlockSpec((1, lanes), lambda i: (0, 0))
    pre = [v.reshape(1, d) for v in pre_norm] if pre_norm else []
    return pl.pallas_call(
        functools.partial(_ln_route_kernel, alpha=alpha, pre_norm=bool(pre_norm),
                          n_groups=n_groups, per_group=per_group),
        grid=(n // tr,),
        in_specs=[row] + [vec] * len(pre) + [row, vec, vec,
                                             pl.BlockSpec((d, lanes), lambda i: (0, 0)), lvec],
        out_specs=[row, pl.BlockSpec((tr, d // 2), lambda i: (i, 0)), lrow, lrow, lvec],
        out_shape=[jax.ShapeDtypeStruct((n, d), F32), jax.ShapeDtypeStruct((n, d // 2), jnp.int32),
                   jax.ShapeDtypeStruct((n, lanes), jnp.int32),
                   jax.ShapeDtypeStruct((n, lanes), F32),
                   jax.ShapeDtypeStruct((1, lanes), jnp.int32)],
        scratch_shapes=[pltpu.VMEM((1, lanes), F32),
                        pltpu.VMEM((d, lanes), BF16), pltpu.VMEM((d, lanes), BF16)],
        compiler_params=_params("arbitrary"),
        name="ln_mix_route",
    )(h, *pre, y, g.reshape(1, d), b.reshape(1, d), w_route, b_route)


def _mm_kernel(x_ref, w_ref, o_ref):
    o_ref[...] = jnp.dot(x_ref[...], w_ref[...], preferred_element_type=F32).astype(o_ref.dtype)


def _matmul(x, w, out_dtype, name, bm=1024, bn=1024):
    m, k = x.shape
    _, n = w.shape
    bm, bn = _tile(m, bm), _tile(n, bn)
    return pl.pallas_call(
        _mm_kernel,
        grid=(m // bm, n // bn),
        in_specs=[pl.BlockSpec((bm, k), lambda i, j: (i, 0)),
                  pl.BlockSpec((k, bn), lambda i, j: (0, j))],
        out_specs=pl.BlockSpec((bm, bn), lambda i, j: (i, j)),
        out_shape=jax.ShapeDtypeStruct((m, n), out_dtype),
        compiler_params=_params("parallel", "parallel"),
        name=name,
    )(x, w)


_NT = (((1,), (1,)), ((), ()))


def _in_proj_kernel(*refs, off, n_col_blocks, features_major):
    if off:
        x_ref, w_ref, wx_ref, o_ref, w16_ref = refs
    else:
        x_ref, w_ref, o_ref, w16_ref = refs
    rc = w_ref.shape[0]
    j, i = pl.program_id(0), pl.program_id(1)

    @pl.when(j < n_col_blocks)
    def _():
        w = w_ref[...]
        if off:
            w = jnp.concatenate([w, wx_ref[...]], axis=0)[off:off + rc, :]
        w16_ref[j % 2, pl.ds(pl.multiple_of(i * rc, rc), rc), :] = w.astype(BF16)

    @pl.when(j > 0)
    def _():
        a, b = x_ref[...], w16_ref[(j - 1) % 2]
        if features_major:
            a, b = b, a
        o_ref[...] = lax.dot_general(a, b, _NT, preferred_element_type=F32).astype(o_ref.dtype)


def _in_proj(x, wt, col_start, n_cols, out_dtype, name, features_major=False):
    m, k = x.shape
    bm, bn = _tile(m, 1024), _tile(n_cols, 1024)
    ni, nj = m // bm, n_cols // bn
    rc = bn // ni
    assert bn % ni == 0 and rc % BF16_SUBLANES == 0
    off = col_start % rc
    assert off % F32_SUBLANES == 0
    blk0 = (col_start - off) // rc
    chunk = lambda j, i: blk0 + jnp.minimum(j, nj - 1) * ni + i
    row = lambda j, i: jnp.where(j == 0, 0, i)
    in_specs = [pl.BlockSpec((bm, k), lambda j, i: (row(j, i), 0)),
                pl.BlockSpec((rc, k), lambda j, i: (chunk(j, i), 0))]
    args = [x, wt]
    if off:
        in_specs.append(pl.BlockSpec((rc, k), lambda j, i: (chunk(j, i) + 1, 0)))
        args.append(wt)
    if features_major:
        out_spec = pl.BlockSpec((bn, bm), lambda j, i: (jnp.maximum(j - 1, 0), row(j, i)))
        out_shape = jax.ShapeDtypeStruct((n_cols, m), out_dtype)
    else:
        out_spec = pl.BlockSpec((bm, bn), lambda j, i: (row(j, i), jnp.maximum(j - 1, 0)))
        out_shape = jax.ShapeDtypeStruct((m, n_cols), out_dtype)
    return pl.pallas_call(
        functools.partial(_in_proj_kernel, off=off, n_col_blocks=nj, features_major=features_major),
        grid=(nj + 1, ni),
        in_specs=in_specs,
        out_specs=out_spec,
        out_shape=out_shape,
        scratch_shapes=[pltpu.VMEM((2, bn, k), BF16)],
        compiler_params=_params("arbitrary", "arbitrary"),
        name=name,
    )(*args)


def _gate_proj_kernel(x_ref, wa_ref, wb_ref, o_ref, w16_ref):
    @pl.when(pl.program_id(0) == 0)
    def _():
        pad = w16_ref.shape[0] - wa_ref.shape[0] - wb_ref.shape[0]
        zeros = jnp.zeros((pad, w16_ref.shape[1]), F32)
        w16_ref[...] = jnp.concatenate([wa_ref[...], wb_ref[...], zeros], axis=0).astype(BF16)

    o_ref[...] = lax.dot_general(x_ref[...], w16_ref[...], _NT, preferred_element_type=F32)


def _gate_proj(x, wt, col_ff, col_mi, n_fox):
    m, k = x.shape
    n_ml = 2 * ML_HEADS
    assert n_fox % F32_SUBLANES == 0 and n_ml % F32_SUBLANES == 0 and n_fox + n_ml <= LANES
    assert col_ff % n_fox == 0 and col_mi % n_ml == 0
    bm = _tile(m, 1024)
    return pl.pallas_call(
        _gate_proj_kernel,
        grid=(m // bm,),
        in_specs=[pl.BlockSpec((bm, k), lambda i: (i, 0)),
                  pl.BlockSpec((n_fox, k), lambda i: (col_ff // n_fox, 0)),
                  pl.BlockSpec((n_ml, k), lambda i: (col_mi // n_ml, 0))],
        out_specs=pl.BlockSpec((bm, LANES), lambda i: (i, 0)),
        out_shape=jax.ShapeDtypeStruct((m, LANES), F32),
        scratch_shapes=[pltpu.VMEM((LANES, k), BF16)],
        compiler_params=_params("arbitrary"),
        name="in_proj_gates",
    )(x, wt, wt)


def _gates_kernel(g_ref, bias_ref, o_ref, carry_ref, *, n_fox, fox_inv_scale):
    @pl.when(pl.program_id(1) == 0)
    def _():
        carry_ref[...] = jnp.zeros_like(carry_ref)

    ts = g_ref.shape[0]
    x = g_ref[...] + bias_ref[...]
    lane = lax.broadcasted_iota(jnp.int32, x.shape, 1)
    is_input_gate = (lane >= n_fox) & (lane < n_fox + ML_HEADS)
    val = jnp.where(is_input_gate, 0.0, _log_sigmoid(x))
    tril = (lax.broadcasted_iota(jnp.int32, (ts, ts), 0)
            >= lax.broadcasted_iota(jnp.int32, (ts, ts), 1)).astype(BF16)
    hi = val.astype(BF16)
    rem = val - hi.astype(F32)
    mid = rem.astype(BF16)
    lo = (rem - mid.astype(F32)).astype(BF16)
    cs = (jnp.dot(tril, hi, preferred_element_type=F32)
          + jnp.dot(tril, mid, preferred_element_type=F32)
          + jnp.dot(tril, lo, preferred_element_type=F32))
    total = cs + jnp.where(lane < n_fox, carry_ref[...], 0.0)
    carry_ref[...] = total[ts - 1:ts, :]
    total = jnp.where(lane < n_fox, total * fox_inv_scale, total)
    o_ref[...] = jnp.where(is_input_gate, x, total)


def _gates(g, bias, batch, ts, n_fox):
    n, lanes = g.shape
    tiles = n // batch // ts
    blk = pl.BlockSpec((ts, lanes), lambda b, j: (b * tiles + j, 0))
    return pl.pallas_call(
        functools.partial(_gates_kernel, n_fox=n_fox, fox_inv_scale=FOX_HEAD_DIM ** 0.5),
        grid=(batch, tiles),
        in_specs=[blk, pl.BlockSpec((1, lanes), lambda b, j: (0, 0))],
        out_specs=blk,
        out_shape=jax.ShapeDtypeStruct((n, lanes), F32),
        scratch_shapes=[pltpu.VMEM((1, lanes), F32)],
        compiler_params=_params("parallel", "arbitrary"),
        name="gates",
    )(g, bias)


def _conv_kernel(x_ref, prev_ref, w_ref, b_ref, o_ref, *, tiles_per_seq, k_blocks_from, k_scale):
    ts = x_ref.shape[0]
    kw = w_ref.shape[0]
    halo = prev_ref.shape[0]
    first = pl.program_id(0) % tiles_per_seq == 0
    prev = jnp.where(first, 0.0, prev_ref[...])
    ext = jnp.concatenate([prev, x_ref[...]], axis=0)
    w = w_ref[...]
    acc = None
    for j in range(kw):
        off = halo - (kw - 1) + j
        term = w[j:j + 1, :] * ext[off:off + ts, :]
        acc = term if acc is None else acc + term
    acc = acc + b_ref[...]
    y = acc * jax.nn.sigmoid(acc)
    scale = jnp.where(pl.program_id(1) >= k_blocks_from, k_scale, 1.0)
    o_ref[...] = y * scale


def _conv_silu(proj, conv_w, conv_b, seq, width):
    n = proj.shape[0]
    kw = conv_w.shape[0]
    ts, tc, halo = _tile(seq, 512), _tile(width // 2, 512), 8
    assert kw - 1 <= halo
    tiles_per_seq = seq // ts
    return pl.pallas_call(
        functools.partial(_conv_kernel, tiles_per_seq=tiles_per_seq,
                          k_blocks_from=(width // 2) // tc, k_scale=ML_QK_DIM ** -0.5),
        grid=(n // ts, width // tc),
        in_specs=[pl.BlockSpec((ts, tc), lambda i, j: (i, j)),
                  pl.BlockSpec((halo, tc), lambda i, j: (jnp.maximum(i * (ts // halo) - 1, 0), j)),
                  pl.BlockSpec((kw, tc), lambda i, j: (0, j)),
                  pl.BlockSpec((1, tc), lambda i, j: (0, j))],
        out_specs=pl.BlockSpec((ts, tc), lambda i, j: (i, j)),
        out_shape=jax.ShapeDtypeStruct((n, width), F32),
        compiler_params=_params("parallel", "parallel"),
        name="conv_silu",
    )(proj, proj, conv_w, conv_b.reshape(1, width))


def _fox_kernel(*refs, scale, n_cast):
    q_ref, k_ref, vt_ref, cq_ref, ckr_ref = refs[:5]
    cast_in = refs[5:5 + n_cast]
    o_ref = refs[5 + n_cast]
    cast_out = refs[6 + n_cast:6 + 2 * n_cast]
    ua_ref, ub_ref, mxa_ref, mxb_ref, m_ref, acc_ref, ck_ref = refs[6 + 2 * n_cast:]
    for src, dst in zip(cast_in, cast_out):
        dst[...] = src[...].astype(dst.dtype)

    tq, dh = q_ref.shape
    tk = ua_ref.shape[0]
    qi = pl.program_id(2)

    @pl.when(qi == 0)
    def _():
        nk = ckr_ref.shape[0]
        rows = jnp.concatenate([ckr_ref[...], jnp.zeros((LANES - nk, tk), F32)], axis=0)
        cols = jnp.transpose(rows)
        for j in range(nk):
            ck_ref[j] = cols[:, j:j + 1]

    c_exp = scale * LOG2_E
    q = q_ref[...]
    cq = cq_ref[...]
    buf_a, buf_b = (ua_ref, mxa_ref), (ub_ref, mxb_ref)

    def score(kb, buf, diagonal=False, late=False):
        u_ref, mx_ref = buf
        cols = slice(tk, tq) if late else slice(0, tq)
        nq = cols.stop - cols.start
        ks = pl.multiple_of(kb * tk, tk)
        u = lax.dot_general(k_ref[pl.ds(ks, tk), :], q[cols], (((1,), (1,)), ((), ())),
                            preferred_element_type=F32) - ck_ref[kb]
        if diagonal:
            u = jnp.where(lax.broadcasted_iota(jnp.int32, (tk, nq), 0)
                          <= lax.broadcasted_iota(jnp.int32, (tk, nq), 1), u, -jnp.inf)
        u_ref[:, cols] = u
        mx_ref[:, cols] = jnp.max(u, axis=0, keepdims=True)

    def absorb(kb, buf, late=False):
        u_ref, mx_ref = buf
        cols = slice(tk, tq) if late else slice(0, tq)
        ks = pl.multiple_of(kb * tk, tk)
        m, cq_c = m_ref[:, cols], cq[:, cols]
        m_new = jnp.maximum(m, mx_ref[:, cols] + cq_c)
        p = jnp.exp2((u_ref[:, cols] + (cq_c - m_new)) * c_exp)
        a = jnp.exp2((m - m_new) * c_exp)
        ones = (lax.broadcasted_iota(jnp.int32, (BF16_SUBLANES, tk), 0) == 0).astype(BF16)
        v_aug = jnp.concatenate([vt_ref[:, pl.ds(ks, tk)], ones], axis=0)
        acc_ref[:, cols] = a * acc_ref[:, cols] + jnp.dot(v_aug, p.astype(BF16),
                                                          preferred_element_type=F32)
        m_ref[:, cols] = m_new

    m_ref[...] = jnp.full_like(m_ref, -jnp.inf)
    acc_ref[...] = jnp.zeros_like(acc_ref)

    @pl.when(qi == 0)
    def _():
        score(0, buf_a, diagonal=True)

    @pl.when(qi > 0)
    def _():
        score(0, buf_a)

        def pair(g, c):
            score(2 * g + 1, buf_b)
            absorb(2 * g, buf_a)
            score(2 * g + 2, buf_a)
            absorb(2 * g + 1, buf_b)
            return c

        lax.fori_loop(0, qi - 1, pair, 0)
        score(2 * qi - 1, buf_b)
        absorb(2 * qi - 2, buf_a)
        score(2 * qi, buf_a, diagonal=True)
        absorb(2 * qi - 1, buf_b)

    score(2 * qi + 1, buf_b, diagonal=True, late=True)
    absorb(2 * qi, buf_a)
    absorb(2 * qi + 1, buf_b, late=True)
    o_ref[...] = jnp.transpose(acc_ref[0:dh, :] / acc_ref[dh:dh + 1, :]).astype(o_ref.dtype)


def _fox_attention(qk, vt, c_heads, tq, batch, seq, heads, col_q, col_k, to_cast):
    n = qk.shape[0]
    dh = FOX_HEAD_DIM
    tk = tq // 2
    nq, nk = seq // tq, seq // tk
    assert nk <= LANES
    c_qrows = c_heads.reshape(batch, heads, nq, 1, tq)
    c_krows = c_heads.reshape(batch, heads, nk, tk)
    steps = batch * heads * nq

    def slab(a):
        rb = _cast_rows(a.shape[0], steps)
        last = a.shape[0] // rb - 1
        return pl.BlockSpec((rb, a.shape[1]),
                            lambda b, h, i: (jnp.minimum((b * heads + h) * nq + i, last), 0))
    outs = pl.pallas_call(
        functools.partial(_fox_kernel, scale=dh ** -0.5, n_cast=len(to_cast)),
        grid=(batch, heads, nq),
        in_specs=[pl.BlockSpec((tq, dh), lambda b, h, i: (b * nq + i, col_q + h)),
                  pl.BlockSpec((seq, dh), lambda b, h, i: (b, col_k + h)),
                  pl.BlockSpec((dh, seq), lambda b, h, i: (h, b)),
                  pl.BlockSpec((None, None, None, 1, tq), lambda b, h, i: (b, h, i, 0, 0)),
                  pl.BlockSpec((None, None, nk, tk), lambda b, h, i: (b, h, 0, 0))]
                 + [slab(a) for a in to_cast],
        out_specs=[pl.BlockSpec((tq, dh), lambda b, h, i: (b * nq + i, h))] + [slab(a) for a in to_cast],
        out_shape=[jax.ShapeDtypeStruct((n, heads * dh), BF16)]
                  + [jax.ShapeDtypeStruct(a.shape, BF16) for a in to_cast],
        scratch_shapes=[pltpu.VMEM((tq // 2, tq), F32), pltpu.VMEM((tq // 2, tq), F32),
                        pltpu.VMEM((1, tq), F32), pltpu.VMEM((1, tq), F32),
                        pltpu.VMEM((1, tq), F32), pltpu.VMEM((dh + BF16_SUBLANES, tq), F32),
                        pltpu.VMEM((nk, tk, 1), F32)],
        compiler_params=_params("arbitrary", "arbitrary", "arbitrary"),
        name="fox_attention",
    )(qk, qk, vt, c_qrows, c_krows, *to_cast)
    return outs[0], outs[1:]


def _cast_rows(rows, steps):
    rb = max(BF16_SUBLANES, -(-rows // steps))
    return rb if rows % rb == 0 and rb % BF16_SUBLANES == 0 else None


def _cast_plan(arrays, steps):
    views = [a.reshape(-1, a.shape[-1]) for a in arrays]
    return views if all(_cast_rows(v.shape[0], steps) for v in views) else None


def _mlstm_kernel(q_ref, k_ref, v_ref, gate_ref, brow_ref, irow_ref, mo_ref, g_ref, o_ref,
                  c_ref, n_ref, m_ref, *, lane_i, lane_b):
    L = q_ref.shape[0]
    h = pl.program_id(1)

    @pl.when(pl.program_id(2) == 0)
    def _():
        c_ref[...] = jnp.zeros_like(c_ref)
        n_ref[...] = jnp.zeros_like(n_ref)
        m_ref[...] = jnp.zeros_like(m_ref)

    gates = gate_ref[...]
    lane = lax.broadcasted_iota(jnp.int32, gates.shape, 1)
    pick = lambda col: jnp.sum(jnp.where(lane == col + h, gates, 0.0), axis=-1, keepdims=True)
    icol = pick(lane_i)
    bcol = pick(lane_b)
    brow = brow_ref[...]
    irow = irow_ref[...]
    m_prev = m_ref[0:1, 0:1]

    q = q_ref[...].astype(BF16)
    k32 = k_ref[...]
    v = v_ref[...]

    tri = lax.broadcasted_iota(jnp.int32, (L, L), 0) >= lax.broadcasted_iota(jnp.int32, (L, L), 1)
    d = jnp.where(tri, bcol - brow + irow, -jnp.inf)
    inter = bcol + m_prev
    m = jnp.maximum(inter, jnp.max(d, axis=-1, keepdims=True))
    w_intra = jnp.exp(d - m)
    w_inter = jnp.exp(inter - m)
    s = lax.dot_general(q, k32.astype(BF16), (((1,), (1,)), ((), ())), preferred_element_type=F32)
    sc = s * w_intra
    num = (jnp.dot(sc.astype(BF16), v, preferred_element_type=F32)
           + w_inter * jnp.dot(q, c_ref[...].astype(BF16), preferred_element_type=F32))
    qn = jnp.sum(q_ref[...] * n_ref[...], axis=-1, keepdims=True)
    den = jnp.sum(sc, axis=-1, keepdims=True) + w_inter * qn
    hid = num / jnp.maximum(jnp.abs(den), jnp.exp(-m))

    b_last = bcol[L - 1:L, :]
    gcol = b_last - bcol + icol
    m_new = jnp.maximum(b_last + m_prev, jnp.max(gcol, axis=0, keepdims=True))
    w_k = jnp.exp(gcol - m_new)
    decay = jnp.exp(b_last + m_prev - m_new)
    kw = k32 * w_k
    c_ref[...] = decay * c_ref[...] + lax.dot_general(
        kw.astype(BF16), v, (((0,), (0,)), ((), ())), preferred_element_type=F32)
    n_ref[...] = decay * n_ref[...] + jnp.sum(kw, axis=0, keepdims=True)
    m_ref[...] = jnp.broadcast_to(m_new, m_ref.shape)

    mu = jnp.mean(hid, axis=-1, keepdims=True)
    hc = hid - mu
    var = jnp.mean(hc * hc, axis=-1, keepdims=True)
    hn = hc * lax.rsqrt(var + LN_EPS)
    o_ref[...] = (hn * g_ref[...] * jax.nn.sigmoid(mo_ref[...])).astype(o_ref.dtype)


def _mlstm(qk, qkv, proj, gates, brow, irow, norm_g, batch, seq, chunk, col_v, col_mo, lane_i, lane_b):
    n = qk.shape[0]
    dk, dv = ML_QK_DIM, ML_V_DIM
    nc = seq // chunk
    row = lambda b, h, c: b * nc + c
    vec = pl.BlockSpec((None, None, None, 1, chunk), lambda b, h, c: (b, h, c, 0, 0))
    return pl.pallas_call(
        functools.partial(_mlstm_kernel, lane_i=lane_i, lane_b=lane_b),
        grid=(batch, ML_HEADS, nc),
        in_specs=[pl.BlockSpec((chunk, dk), lambda b, h, c: (row(b, h, c), h)),
                  pl.BlockSpec((chunk, dk), lambda b, h, c: (row(b, h, c), ML_HEADS + h)),
                  pl.BlockSpec((chunk, dv), lambda b, h, c: (row(b, h, c), col_v + h)),
                  pl.BlockSpec((chunk, gates.shape[1]), lambda b, h, c: (row(b, h, c), 0)),
                  vec, vec,
                  pl.BlockSpec((chunk, dv), lambda b, h, c: (row(b, h, c), col_mo + h)),
                  pl.BlockSpec((1, dv), lambda b, h, c: (0, h))],
        out_specs=pl.BlockSpec((chunk, dv), lambda b, h, c: (row(b, h, c), h)),
        out_shape=jax.ShapeDtypeStruct((n, ML_HEADS * dv), BF16),
        scratch_shapes=[pltpu.VMEM((dk, dv), F32), pltpu.VMEM((1, dk), F32), pltpu.VMEM((8, 128), F32)],
        compiler_params=_params("parallel", "parallel", "arbitrary"),
        name="mlstm",
    )(qk, qk, qkv, gates, brow, irow, proj, norm_g.reshape(1, ML_HEADS * dv))


def _merge_kernel(yf_ref, ym_ref, wf_ref, wm_ref, ga_ref, gb_ref, o_ref):
    a = jnp.dot(yf_ref[...], wf_ref[...], preferred_element_type=F32)
    b = jnp.dot(ym_ref[...], wm_ref[...], preferred_element_type=F32)
    o_ref[...] = (jax.nn.sigmoid(ga_ref[...]) * a + jax.nn.sigmoid(gb_ref[...]) * b).astype(o_ref.dtype)


def _merge(y_fox, y_ml, w_fox, w_ml, proj, col_ga, col_gb):
    n, kf = y_fox.shape
    km = y_ml.shape[1]
    d = w_fox.shape[1]
    bm, bn = _tile(n, 512), _tile(d, 1024)
    ga0, gb0 = col_ga // bn, col_gb // bn
    return pl.pallas_call(
        _merge_kernel,
        grid=(n // bm, d // bn),
        in_specs=[pl.BlockSpec((bm, kf), lambda i, j: (i, 0)),
                  pl.BlockSpec((bm, km), lambda i, j: (i, 0)),
                  pl.BlockSpec((kf, bn), lambda i, j: (0, j)),
                  pl.BlockSpec((km, bn), lambda i, j: (0, j)),
                  pl.BlockSpec((bm, bn), lambda i, j: (i, ga0 + j)),
                  pl.BlockSpec((bm, bn), lambda i, j: (i, gb0 + j))],
        out_specs=pl.BlockSpec((bm, bn), lambda i, j: (i, j)),
        out_shape=jax.ShapeDtypeStruct((n, d), BF16),
        compiler_params=_params("parallel", "parallel"),
        name="merge_proj",
    )(y_fox, y_ml, w_fox, w_ml, proj, proj)


def _split_bf16(v):
    hi = v.astype(BF16)
    return hi, (v - hi.astype(F32)).astype(BF16)


def _route_rows(x, w_ref, b_ref, oi_ref, of_ref, cnt_ref, carry_ref, whi_ref, wlo_ref,
                *, n_groups, per_group):
    @pl.when(pl.program_id(0) == 0)
    def _():
        carry_ref[...] = jnp.zeros_like(carry_ref)
        whi_ref[...], wlo_ref[...] = _split_bf16(w_ref[...])

    tr = x.shape[0]
    n_exp = n_groups * per_group
    x_hi, x_lo = _split_bf16(x)
    dot = lambda a, b: jnp.dot(a, b, preferred_element_type=F32)
    logits = (dot(x_hi, whi_ref[...]) + dot(x_lo, whi_ref[...]) + dot(x_hi, wlo_ref[...])) + b_ref[...]
    lane = lax.broadcasted_iota(jnp.int32, logits.shape, 1)
    big = jnp.int32(logits.shape[1])
    first_lane = lambda cond: jnp.min(jnp.where(cond, lane, big), axis=-1, keepdims=True)

    is_group = (lane >= n_exp) & (lane < n_exp + n_groups)
    gl = jnp.where(is_group, logits, -jnp.inf)
    g_max = jnp.max(gl, axis=-1, keepdims=True)
    g_sel = first_lane(gl == g_max) - n_exp
    p_g_sel = 1.0 / jnp.sum(jnp.exp(gl - g_max), axis=-1, keepdims=True)

    in_group = (lane >= g_sel * per_group) & (lane < (g_sel + 1) * per_group)
    el = jnp.where(in_group, logits, -jnp.inf)
    e_max = jnp.max(el, axis=-1, keepdims=True)
    ee = jnp.exp(el - e_max)
    pe = jnp.where(in_group, ee / jnp.sum(ee, axis=-1, keepdims=True), -1.0)
    p1 = jnp.max(pe, axis=-1, keepdims=True)
    e1 = first_lane(pe == p1)
    pe2 = jnp.where(lane == e1, -1.0, pe)
    p2 = jnp.max(pe2, axis=-1, keepdims=True)
    e2 = first_lane(pe2 == p2)
    p_sum = p1 + p2
    w1 = p_g_sel * p1 / p_sum
    w2 = p_g_sel * p2 / p_sum

    onehot = (lane == e1) | (lane == e2)
    strict = (lax.broadcasted_iota(jnp.int32, (tr, tr), 0)
              > lax.broadcasted_iota(jnp.int32, (tr, tr), 1)).astype(BF16)
    before = jnp.dot(strict, onehot.astype(BF16), preferred_element_type=F32) + carry_ref[...]
    r1 = jnp.sum(jnp.where(lane == e1, before, 0.0), axis=-1, keepdims=True).astype(jnp.int32)
    r2 = jnp.sum(jnp.where(lane == e2, before, 0.0), axis=-1, keepdims=True).astype(jnp.int32)
    carry_ref[...] += jnp.sum(onehot.astype(F32), axis=0, keepdims=True)
    cnt_ref[...] = carry_ref[...].astype(jnp.int32)

    oi_ref[...] = jnp.where(lane == 0, e1, jnp.where(lane == 1, e2, jnp.where(lane == 2, r1, r2)))
    of_ref[...] = jnp.where(lane == 0, w1, w2)


def _expert_kernel(blk_e_ref, n_used_ref, tok_ref, tok_next_ref, x_hbm, wt_ref, wg_ref, wu_ref, wd_ref,
                   o_ref, xbuf, sem):
    b = pl.program_id(0)
    tb, half = xbuf.shape[1:]
    n_used = n_used_ref[0]
    slot = b % 2

    def row_copy(ids_ref, r, s):
        return pltpu.make_async_copy(x_hbm.at[pl.ds(ids_ref[r], 1), :],
                                     xbuf.at[s, pl.ds(r, 1), :], sem.at[s])

    def wait_rows():
        pltpu.make_async_copy(x_hbm.at[pl.ds(0, tb), :], xbuf.at[slot], sem.at[slot]).wait()

    def compute():
        lo, hi = _unpack_bf16_pairs(xbuf[slot])
        lo, hi = lo.astype(BF16), hi.astype(BF16)
        proj = lambda w_ref: (jnp.dot(lo, w_ref[0:half, :], preferred_element_type=F32)
                              + jnp.dot(hi, w_ref[half:2 * half, :], preferred_element_type=F32))
        gate, up = proj(wg_ref), proj(wu_ref)
        act = (gate * jax.nn.sigmoid(gate) * up).astype(BF16)
        out = jnp.dot(act, wd_ref[...], preferred_element_type=F32) * wt_ref[...]
        o_ref[...] = _pack_bf16_pairs(out)

    @pl.when((b == 0) & (n_used > 0))
    def _():
        def start(r, c):
            row_copy(tok_ref, r, slot).start()
            return c
        lax.fori_loop(0, tb, start, 0)

    @pl.when(b + 1 < n_used)
    def _():
        wait_rows()
        for r in range(tb):
            row_copy(tok_next_ref, r, 1 - slot).start(priority=r % 2)
        compute()

    @pl.when(b + 1 == n_used)
    def _():
        wait_rows()
        compute()

    @pl.when(b >= n_used)
    def _():
        o_ref[...] = jnp.zeros_like(o_ref)


def _experts(xp, slot_tok, slot_w, block_e, n_used, w_gate, w_up, w_down, tb):
    n, half = xp.shape
    _, d, f = w_gate.shape
    assert d == 2 * half
    n_blocks = block_e.shape[0]
    smem_tb = lambda shift: pl.BlockSpec(
        (tb,), lambda b, *_: (jnp.minimum(b + shift, n_blocks - 1),), memory_space=pltpu.SMEM)
    grid_spec = pltpu.PrefetchScalarGridSpec(
        num_scalar_prefetch=2,
        grid=(n_blocks,),
        in_specs=[smem_tb(0), smem_tb(1),
                  pl.BlockSpec(memory_space=pl.ANY),
                  pl.BlockSpec((tb, 1), lambda b, *_: (b, 0)),
                  pl.BlockSpec((None, d, f), lambda b, be, nu: (be[b], 0, 0)),
                  pl.BlockSpec((None, d, f), lambda b, be, nu: (be[b], 0, 0)),
                  pl.BlockSpec((None, f, d), lambda b, be, nu: (be[b], 0, 0))],
        out_specs=pl.BlockSpec((tb, half), lambda b, *_: (b, 0)),
        scratch_shapes=[pltpu.VMEM((2, tb, half), jnp.int32), pltpu.SemaphoreType.DMA((2,))],
    )
    return pl.pallas_call(
        _expert_kernel,
        grid_spec=grid_spec,
        out_shape=jax.ShapeDtypeStruct((n_blocks * tb, half), jnp.int32),
        compiler_params=_params("arbitrary"),
        name="experts",
    )(block_e, n_used, slot_tok, slot_tok, xp, slot_w.reshape(-1, 1), w_gate, w_up, w_down)


def _combine_kernel(dest_ref, dest_next_ref, y_hbm, h_ref, g_ref, b_ref, o_ref, buf, sem, *, alpha):
    i = pl.program_id(0)
    last = pl.num_programs(0) - 1
    tm = h_ref.shape[0]
    slot = i % 2

    def row_copy(ids_ref, r, k, s):
        return pltpu.make_async_copy(y_hbm.at[pl.ds(ids_ref[r * TOP_K + k], 1), :],
                                     buf.at[s, k, pl.ds(r, 1), :], sem.at[s])

    def compute():
        y = None
        for k in range(TOP_K):
            y_k = jnp.concatenate(_unpack_bf16_pairs(buf[slot, k]), axis=1)
            y = y_k if y is None else y + y_k
        o_ref[...] = _layer_norm(alpha * h_ref[...] + y, g_ref[...], b_ref[...])

    @pl.when(i == 0)
    def _():
        def start(r, c):
            for k in range(TOP_K):
                row_copy(dest_ref, r, k, slot).start()
            return c
        lax.fori_loop(0, tm, start, 0)

    for k in range(TOP_K):
        pltpu.make_async_copy(y_hbm.at[pl.ds(0, tm), :], buf.at[slot, k], sem.at[slot]).wait()

    @pl.when(i < last)
    def _():
        for r in range(tm):
            for k in range(TOP_K):
                row_copy(dest_next_ref, r, k, 1 - slot).start(priority=(r * TOP_K + k) % 2)
        compute()

    @pl.when(i == last)
    def _():
        compute()


def _combine_ln(h, y_slots, dest, g, b, alpha):
    n, d = h.shape
    tm = _tile(n, 256)
    tiles = n // tm
    return pl.pallas_call(
        functools.partial(_combine_kernel, alpha=alpha),
        grid=(tiles,),
        in_specs=[pl.BlockSpec((tm * TOP_K,), lambda i: (i,), memory_space=pltpu.SMEM),
                  pl.BlockSpec((tm * TOP_K,), lambda i: (jnp.minimum(i + 1, tiles - 1),),
                               memory_space=pltpu.SMEM),
                  pl.BlockSpec(memory_space=pl.ANY),
                  pl.BlockSpec((tm, d), lambda i: (i, 0)),
                  pl.BlockSpec((1, d), lambda i: (0, 0)),
                  pl.BlockSpec((1, d), lambda i: (0, 0))],
        out_specs=pl.BlockSpec((tm, d), lambda i: (i, 0)),
        out_shape=jax.ShapeDtypeStruct((n, d), F32),
        scratch_shapes=[pltpu.VMEM((2, TOP_K, tm, d // 2), jnp.int32), pltpu.SemaphoreType.DMA((2,))],
        compiler_params=_params("arbitrary"),
        name="combine_ln",
    )(dest, dest, y_slots, h, g.reshape(1, d), b.reshape(1, d))


EXPERT_SLOT_BLOCK = 256
ML_CHUNK = 256
FOX_KEY_BLOCK = 512


def _mixer(h16, batch, seq, w_in, b_fox_f, b_ml_i, b_ml_f, conv_w, conv_b, ml_norm_g,
           w_proj_fox, w_proj_ml, w_out, expert_w):
    n, d = h16.shape
    fox_w = d // 2
    fox_heads = fox_w // FOX_HEAD_DIM
    qk_w = ML_HEADS * ML_QK_DIM
    v_w = ML_HEADS * ML_V_DIM
    widths = (fox_w, fox_w, fox_w, fox_heads, qk_w, qk_w, v_w, ML_HEADS, ML_HEADS, v_w, d, d)
    offs = [0]
    for w in widths:
        offs.append(offs[-1] + w)
    n_gate = fox_heads + 2 * ML_HEADS
    gate_bias = jnp.pad(jnp.concatenate([b_fox_f, b_ml_i, b_ml_f]), (0, GATE_LANES - n_gate)).reshape(1, -1)

    wt = jnp.transpose(w_in)
    proj_fox = _in_proj(h16, wt, offs[0], 2 * fox_w, BF16, "in_proj_fox")
    v_t = _in_proj(h16, wt, offs[2], fox_w, BF16, "in_proj_fv", features_major=True)
    proj_qk = _in_proj(h16, wt, offs[4], 2 * qk_w, F32, "in_proj_qk")
    proj_mv = _in_proj(h16, wt, offs[6], v_w, BF16, "in_proj_mv")
    proj_og = _in_proj(h16, wt, offs[9], v_w + 2 * d, F32, "in_proj_og")
    gate_pre = _gate_proj(h16, wt, offs[3], offs[7], fox_heads)

    chunk = _tile(seq, ML_CHUNK)
    gates = _gates(gate_pre, gate_bias, batch, chunk, fox_heads)

    g3 = gates.reshape(batch, seq, GATE_LANES)
    tq = 2 * _tile(seq // 2, FOX_KEY_BLOCK)
    c_heads = jnp.transpose(g3[:, :, :fox_heads], (0, 2, 1))
    lane_i, lane_b = fox_heads, fox_heads + ML_HEADS
    rows = lambda lo: jnp.transpose(g3[:, :, lo:lo + ML_HEADS], (0, 2, 1)).reshape(
        batch, ML_HEADS, seq // chunk, 1, chunk)
    irow, brow = rows(lane_i), rows(lane_b)

    later_w = [w_proj_fox, w_proj_ml, w_out] + list(expert_w)
    views = _cast_plan(later_w, batch * fox_heads * (seq // tq))
    y_fox, cast = _fox_attention(proj_fox, v_t, c_heads, tq, batch, seq, fox_heads,
                                 col_q=0, col_k=fox_heads, to_cast=views or [])
    if views:
        later_w16 = [c.reshape(a.shape) for c, a in zip(cast, later_w)]
    else:
        later_w16 = [a.astype(BF16) for a in later_w]
    w_proj_fox16, w_proj_ml16, w_out16 = later_w16[:3]
    qk = _conv_silu(proj_qk, conv_w, conv_b, seq, 2 * qk_w)
    y_ml = _mlstm(qk, proj_mv, proj_og, gates, brow, irow, ml_norm_g, batch, seq, chunk,
                  col_v=0, col_mo=0, lane_i=lane_i, lane_b=lane_b)
    merged = _merge(y_fox, y_ml, w_proj_fox16, w_proj_ml16, proj_og, col_ga=v_w, col_gb=v_w + d)
    return _matmul(merged, w_out16, F32, "out_proj"), later_w16[3:]


def _route_params(w_group, b_group, w_router, b_router):
    n_groups, n_exp = w_group.shape[1], w_router.shape[1]
    assert n_exp + n_groups <= GATE_LANES
    pad = GATE_LANES - n_exp - n_groups
    w_route = jnp.pad(jnp.concatenate([w_router, w_group], axis=1), ((0, 0), (0, pad)))
    b_route = jnp.pad(jnp.concatenate([b_router, b_group]), (0, pad)).reshape(1, GATE_LANES)
    return w_route, b_route, n_groups, n_exp // n_groups


def _moe(h, h_packed, routing, n_exp, w_gate, w_up, w_down, ln_g, ln_b, alpha):
    n, d = h.shape
    oi, of, cnt = routing
    e_idx, rank, gate_w = oi[:, 0:TOP_K], oi[:, TOP_K:2 * TOP_K], of[:, 0:TOP_K]
    counts = cnt[0, :n_exp]

    tb = EXPERT_SLOT_BLOCK
    n_assign = n * TOP_K
    n_blocks = (n_assign + n_exp * (tb - 1) + tb - 1) // tb
    padded = (counts + tb - 1) // tb * tb
    pad_ends = jnp.cumsum(padded)
    onehot = (e_idx[..., None] == jnp.arange(n_exp, dtype=jnp.int32)).astype(F32)
    seg_start = jnp.einsum("nke,e->nk", onehot, (pad_ends - padded).astype(F32),
                           precision=lax.Precision.HIGHEST).astype(jnp.int32)
    dest = (seg_start + rank).reshape(-1)
    tok = jnp.repeat(jnp.arange(n, dtype=jnp.int32), TOP_K)
    pairs = jnp.stack([tok, lax.bitcast_convert_type(gate_w.reshape(-1), jnp.int32)], axis=1)
    slots = jnp.zeros((n_blocks * tb, 2), jnp.int32).at[dest].set(pairs)
    slot_tok = slots[:, 0]
    slot_w = lax.bitcast_convert_type(slots[:, 1], F32)
    n_used = pad_ends[-1] // tb
    blk = jnp.arange(n_blocks, dtype=jnp.int32)
    block_e = jnp.minimum(jnp.searchsorted(pad_ends, blk * tb, side="right"), n_exp - 1).astype(jnp.int32)
    block_e = jnp.where(blk < n_used, block_e, block_e[jnp.maximum(n_used - 1, 0)])

    y_slots = _experts(h_packed, slot_tok, slot_w, block_e, n_used.reshape(1).astype(jnp.int32),
                       w_gate, w_up, w_down, tb)
    return _combine_ln(h, y_slots, dest.astype(jnp.int32), ln_g, ln_b, alpha)


def kernel(x, ln_in_g, ln_in_b, w_in, b_fox_f, b_ml_i, b_ml_f, conv_w, conv_b, ml_norm_g, w_proj_fox, w_proj_ml, w_out, ln_mix_g, ln_mix_b, w_group, b_group, w_router, b_router, w_gate, w_up, w_down, ln_moe_g, ln_moe_b):
    batch, seq, d = x.shape
    depth = w_in.shape[0]
    alpha = (2 * depth) ** 0.25
    h32 = x.reshape(batch * seq, d)
    h16 = _ln_in(h32, ln_in_g, ln_in_b)
    for l in range(depth):
        mix, expert_w16 = _mixer(h16, batch, seq, w_in[l], b_fox_f[l], b_ml_i[l], b_ml_f[l],
                                 conv_w[l], conv_b[l], ml_norm_g[l], w_proj_fox[l], w_proj_ml[l],
                                 w_out[l], [w_gate[l], w_up[l], w_down[l]])
        w_route, b_route, n_groups, per_group = _route_params(w_group[l], b_group[l],
                                                              w_router[l], b_router[l])
        h32, h_packed, *routing = _ln_route(h32, mix, ln_mix_g[l], ln_mix_b[l], alpha,
                                            w_route, b_route, n_groups, per_group,
                                            pre_norm=(ln_in_g, ln_in_b) if l == 0 else None)
        h32 = _moe(h32, h_packed, routing, n_groups * per_group, *expert_w16,
                   ln_moe_g[l], ln_moe_b[l], alpha)
        if l + 1 < depth:
            h16 = h32.astype(BF16)
    return h32.reshape(batch, seq, d)
```

```python
import functools

import jax
import jax.numpy as jnp
from jax import lax
from jax.experimental import pallas as pl
from jax.experimental.pallas import tpu as pltpu

F32 = jnp.float32
BF16 = jnp.bfloat16

LN_EPS = 1e-5
FOX_HEAD_DIM = 128
ML_HEADS = 4
ML_QK_DIM = 256
ML_V_DIM = 512
TOP_K = 2
LANES = 128
F32_SUBLANES = 8
BF16_SUBLANES = 16
GATE_LANES = LANES
LOG2_E = 1.4426950408889634

V7X_VMEM_BYTES = 64 * 1024 * 1024
VMEM_LIMIT_BYTES = V7X_VMEM_BYTES - 8 * 1024 * 1024

MATMUL_BLOCK = 1024
MERGE_ROW_BLOCK = 512
ROW_TILE = 256


def _params(*sem):
    return pltpu.CompilerParams(dimension_semantics=sem, vmem_limit_bytes=VMEM_LIMIT_BYTES)


def _tile(dim, pref):
    t = min(dim, pref)
    while dim % t:
        t //= 2
    return t


def _layer_norm(x, g, b):
    mu = jnp.mean(x, axis=-1, keepdims=True)
    xc = x - mu
    var = jnp.mean(xc * xc, axis=-1, keepdims=True)
    return xc * lax.rsqrt(var + LN_EPS) * g + b


def _log_sigmoid(x):
    return jnp.minimum(x, 0.0) - jnp.log1p(jnp.exp(-jnp.abs(x)))


def _ln_in_kernel(x_ref, g_ref, b_ref, o16_ref):
    o16_ref[...] = _layer_norm(x_ref[...], g_ref[...], b_ref[...]).astype(BF16)


def _ln_in(x, g, b):
    n, d = x.shape
    tr = _tile(n, ROW_TILE)
    row = pl.BlockSpec((tr, d), lambda i: (i, 0))
    vec = pl.BlockSpec((1, d), lambda i: (0, 0))
    return pl.pallas_call(
        _ln_in_kernel,
        grid=(n // tr,),
        in_specs=[row, vec, vec],
        out_specs=row,
        out_shape=jax.ShapeDtypeStruct((n, d), BF16),
        compiler_params=_params("parallel"),
        name="ln_in",
    )(x, g.reshape(1, d), b.reshape(1, d))


def _pack_bf16_pairs(x):
    half = x.shape[1] // 2
    bits = lambda v: lax.bitcast_convert_type(v.astype(BF16).astype(F32), jnp.uint32)
    word = lax.shift_right_logical(bits(x[:, :half]), jnp.uint32(16)) | bits(x[:, half:])
    return lax.bitcast_convert_type(word, jnp.int32)


def _unpack_bf16_pairs(w):
    u = lax.bitcast_convert_type(w, jnp.uint32)
    lo = lax.bitcast_convert_type(lax.shift_left(u, jnp.uint32(16)), F32)
    hi = lax.bitcast_convert_type(u & jnp.uint32(0xFFFF0000), F32)
    return lo, hi


def _ln_route_kernel(*refs, alpha, pre_norm, n_groups, per_group):
    if pre_norm:
        h_ref, g0_ref, b0_ref, *refs = refs
        h = _layer_norm(h_ref[...], g0_ref[...], b0_ref[...])
    else:
        h_ref, *refs = refs
        h = h_ref[...]
    y_ref, g_ref, b_ref, w_ref, rb_ref, o_ref, op_ref, *route_refs = refs
    out = _layer_norm(alpha * h + y_ref[...], g_ref[...], b_ref[...])
    o_ref[...] = out
    op_ref[...] = _pack_bf16_pairs(out)
    _route_rows(out, w_ref, rb_ref, *route_refs, n_groups=n_groups, per_group=per_group)


def _ln_route(h, y, g, b, alpha, w_route, b_route, n_groups, per_group, pre_norm=None):
    n, d = h.shape
    lanes = w_route.shape[1]
    tr = _tile(n, ROW_TILE)
    row = pl.BlockSpec((tr, d), lambda i: (i, 0))
    vec = pl.BlockSpec((1, d), lambda i: (0, 0))
    lrow = pl.BlockSpec((tr, lanes), lambda i: (i, 0))
    lvec = pl.BlockSpec((1, lanes), lambda i: (0, 0))
    pre = [v.reshape(1, d) for v in pre_norm] if pre_norm else []
    return pl.pallas_call(
        functools.partial(_ln_route_kernel, alpha=alpha, pre_norm=bool(pre_norm),
                          n_groups=n_groups, per_group=per_group),
        grid=(n // tr,),
        in_specs=[row] + [vec] * len(pre) + [row, vec, vec,
                                             pl.BlockSpec((d, lanes), lambda i: (0, 0)), lvec],
        out_specs=[row, pl.BlockSpec((tr, d // 2), lambda i: (i, 0)), lrow, lrow, lvec],
        out_shape=[jax.ShapeDtypeStruct((n, d), F32), jax.ShapeDtypeStruct((n, d // 2), jnp.int32),
                   jax.ShapeDtypeStruct((n, lanes), jnp.int32),
                   jax.ShapeDtypeStruct((n, lanes), F32),
                   jax.ShapeDtypeStruct((1, lanes), jnp.int32)],
        scratch_shapes=[pltpu.VMEM((1, lanes), F32),
                        pltpu.VMEM((d, lanes), BF16), pltpu.VMEM((d, lanes), BF16)],
        compiler_params=_params("arbitrary"),
        name="ln_mix_route",
    )(h, *pre, y, g.reshape(1, d), b.reshape(1, d), w_route, b_route)


def _mm_kernel(x_ref, w_ref, o_ref):
    o_ref[...] = jnp.dot(x_ref[...], w_ref[...], preferred_element_type=F32).astype(o_ref.dtype)


def _matmul(x, w, out_dtype, name):
    m, k = x.shape
    _, n = w.shape
    bm, bn = _tile(m, MATMUL_BLOCK), _tile(n, MATMUL_BLOCK)
    return pl.pallas_call(
        _mm_kernel,
        grid=(m // bm, n // bn),
        in_specs=[pl.BlockSpec((bm, k), lambda i, j: (i, 0)),
                  pl.BlockSpec((k, bn), lambda i, j: (0, j))],
        out_specs=pl.BlockSpec((bm, bn), lambda i, j: (i, j)),
        out_shape=jax.ShapeDtypeStruct((m, n), out_dtype),
        compiler_params=_params("parallel", "parallel"),
        name=name,
    )(x, w)


_NT = (((1,), (1,)), ((), ()))


def _in_proj_kernel(*refs, off, n_col_blocks, features_major):
    if off:
        x_ref, w_ref, wx_ref, o_ref, w16_ref = refs
    else:
        x_ref, w_ref, o_ref, w16_ref = refs
    rc = w_ref.shape[0]
    j, i = pl.program_id(0), pl.program_id(1)

    @pl.when(j < n_col_blocks)
    def _():
        w = w_ref[...]
        if off:
            w = jnp.concatenate([w, wx_ref[...]], axis=0)[off:off + rc, :]
        w16_ref[j % 2, pl.ds(pl.multiple_of(i * rc, rc), rc), :] = w.astype(BF16)

    @pl.when(j > 0)
    def _():
        a, b = x_ref[...], w16_ref[(j - 1) % 2]
        if features_major:
            a, b = b, a
        o_ref[...] = lax.dot_general(a, b, _NT, preferred_element_type=F32).astype(o_ref.dtype)


def _in_proj(x, wt, col_start, n_cols, out_dtype, name, features_major=False):
    m, k = x.shape
    bm, bn = _tile(m, MATMUL_BLOCK), _tile(n_cols, MATMUL_BLOCK)
    ni, nj = m // bm, n_cols // bn
    rc = bn // ni
    assert bn % ni == 0 and rc % BF16_SUBLANES == 0
    off = col_start % rc
    assert off % F32_SUBLANES == 0
    blk0 = (col_start - off) // rc
    chunk = lambda j, i: blk0 + jnp.minimum(j, nj - 1) * ni + i
    row = lambda j, i: jnp.where(j == 0, 0, i)
    in_specs = [pl.BlockSpec((bm, k), lambda j, i: (row(j, i), 0)),
                pl.BlockSpec((rc, k), lambda j, i: (chunk(j, i), 0))]
    args = [x, wt]
    if off:
        in_specs.append(pl.BlockSpec((rc, k), lambda j, i: (chunk(j, i) + 1, 0)))
        args.append(wt)
    if features_major:
        out_spec = pl.BlockSpec((bn, bm), lambda j, i: (jnp.maximum(j - 1, 0), row(j, i)))
        out_shape = jax.ShapeDtypeStruct((n_cols, m), out_dtype)
    else:
        out_spec = pl.BlockSpec((bm, bn), lambda j, i: (row(j, i), jnp.maximum(j - 1, 0)))
        out_shape = jax.ShapeDtypeStruct((m, n_cols), out_dtype)
    return pl.pallas_call(
        functools.partial(_in_proj_kernel, off=off, n_col_blocks=nj, features_major=features_major),
        grid=(nj + 1, ni),
        in_specs=in_specs,
        out_specs=out_spec,
        out_shape=out_shape,
        scratch_shapes=[pltpu.VMEM((2, bn, k), BF16)],
        compiler_params=_params("arbitrary", "arbitrary"),
        name=name,
    )(*args)


def _gate_proj_kernel(x_ref, wa_ref, wb_ref, o_ref, w16_ref):
    @pl.when(pl.program_id(0) == 0)
    def _():
        pad = w16_ref.shape[0] - wa_ref.shape[0] - wb_ref.shape[0]
        zeros = jnp.zeros((pad, w16_ref.shape[1]), F32)
        w16_ref[...] = jnp.concatenate([wa_ref[...], wb_ref[...], zeros], axis=0).astype(BF16)

    o_ref[...] = lax.dot_general(x_ref[...], w16_ref[...], _NT, preferred_element_type=F32)


def _gate_proj(x, wt, col_ff, col_mi, n_fox):
    m, k = x.shape
    n_ml = 2 * ML_HEADS
    assert n_fox % F32_SUBLANES == 0 and n_ml % F32_SUBLANES == 0 and n_fox + n_ml <= LANES
    assert col_ff % n_fox == 0 and col_mi % n_ml == 0
    bm = _tile(m, MATMUL_BLOCK)
    return pl.pallas_call(
        _gate_proj_kernel,
        grid=(m // bm,),
        in_specs=[pl.BlockSpec((bm, k), lambda i: (i, 0)),
                  pl.BlockSpec((n_fox, k), lambda i: (col_ff // n_fox, 0)),
                  pl.BlockSpec((n_ml, k), lambda i: (col_mi // n_ml, 0))],
        out_specs=pl.BlockSpec((bm, LANES), lambda i: (i, 0)),
        out_shape=jax.ShapeDtypeStruct((m, LANES), F32),
        scratch_shapes=[pltpu.VMEM((LANES, k), BF16)],
        compiler_params=_params("arbitrary"),
        name="in_proj_gates",
    )(x, wt, wt)


def _gates_kernel(g_ref, bias_ref, o_ref, carry_ref, *, n_fox, fox_inv_scale):
    @pl.when(pl.program_id(1) == 0)
    def _():
        carry_ref[...] = jnp.zeros_like(carry_ref)

    ts = g_ref.shape[0]
    x = g_ref[...] + bias_ref[...]
    lane = lax.broadcasted_iota(jnp.int32, x.shape, 1)
    is_input_gate = (lane >= n_fox) & (lane < n_fox + ML_HEADS)
    val = jnp.where(is_input_gate, 0.0, _log_sigmoid(x))
    tril = (lax.broadcasted_iota(jnp.int32, (ts, ts), 0)
            >= lax.broadcasted_iota(jnp.int32, (ts, ts), 1)).astype(BF16)
    hi = val.astype(BF16)
    rem = val - hi.astype(F32)
    mid = rem.astype(BF16)
    lo = (rem - mid.astype(F32)).astype(BF16)
    cs = (jnp.dot(tril, hi, preferred_element_type=F32)
          + jnp.dot(tril, mid, preferred_element_type=F32)
          + jnp.dot(tril, lo, preferred_element_type=F32))
    total = cs + jnp.where(lane < n_fox, carry_ref[...], 0.0)
    carry_ref[...] = total[ts - 1:ts, :]
    total = jnp.where(lane < n_fox, total * fox_inv_scale, total)
    o_ref[...] = jnp.where(is_input_gate, x, total)


def _gates(g, bias, batch, ts, n_fox):
    n, lanes = g.shape
    tiles = n // batch // ts
    blk = pl.BlockSpec((ts, lanes), lambda b, j: (b * tiles + j, 0))
    return pl.pallas_call(
        functools.partial(_gates_kernel, n_fox=n_fox, fox_inv_scale=FOX_HEAD_DIM ** 0.5),
        grid=(batch, tiles),
        in_specs=[blk, pl.BlockSpec((1, lanes), lambda b, j: (0, 0))],
        out_specs=blk,
        out_shape=jax.ShapeDtypeStruct((n, lanes), F32),
        scratch_shapes=[pltpu.VMEM((1, lanes), F32)],
        compiler_params=_params("parallel", "arbitrary"),
        name="gates",
    )(g, bias)


def _fox_kernel(*refs, scale, n_cast):
    q_ref, k_ref, vt_ref, cq_ref, ckr_ref = refs[:5]
    cast_in = refs[5:5 + n_cast]
    o_ref = refs[5 + n_cast]
    cast_out = refs[6 + n_cast:6 + 2 * n_cast]
    ua_ref, ub_ref, mxa_ref, mxb_ref, m_ref, acc_ref, ck_ref = refs[6 + 2 * n_cast:]
    for src, dst in zip(cast_in, cast_out):
        dst[...] = src[...].astype(dst.dtype)

    tq, dh = q_ref.shape
    tk = ua_ref.shape[0]
    qi = pl.program_id(2)

    @pl.when(qi == 0)
    def _():
        nk = ckr_ref.shape[0]
        rows = jnp.concatenate([ckr_ref[...], jnp.zeros((LANES - nk, tk), F32)], axis=0)
        cols = jnp.transpose(rows)
        for j in range(nk):
            ck_ref[j] = cols[:, j:j + 1]

    c_exp = scale * LOG2_E
    q = q_ref[...]
    cq = cq_ref[...]
    buf_a, buf_b = (ua_ref, mxa_ref), (ub_ref, mxb_ref)

    def score(kb, buf, diagonal=False, late=False):
        u_ref, mx_ref = buf
        cols = slice(tk, tq) if late else slice(0, tq)
        nq = cols.stop - cols.start
        ks = pl.multiple_of(kb * tk, tk)
        u = lax.dot_general(k_ref[pl.ds(ks, tk), :], q[cols], (((1,), (1,)), ((), ())),
                            preferred_element_type=F32) - ck_ref[kb]
        if diagonal:
            u = jnp.where(lax.broadcasted_iota(jnp.int32, (tk, nq), 0)
                          <= lax.broadcasted_iota(jnp.int32, (tk, nq), 1), u, -jnp.inf)
        u_ref[:, cols] = u
        mx_ref[:, cols] = jnp.max(u, axis=0, keepdims=True)

    def absorb(kb, buf, late=False):
        u_ref, mx_ref = buf
        cols = slice(tk, tq) if late else slice(0, tq)
        ks = pl.multiple_of(kb * tk, tk)
        m, cq_c = m_ref[:, cols], cq[:, cols]
        m_new = jnp.maximum(m, mx_ref[:, cols] + cq_c)
        p = jnp.exp2((u_ref[:, cols] + (cq_c - m_new)) * c_exp)
        a = jnp.exp2((m - m_new) * c_exp)
        ones = (lax.broadcasted_iota(jnp.int32, (BF16_SUBLANES, tk), 0) == 0).astype(BF16)
        v_aug = jnp.concatenate([vt_ref[:, pl.ds(ks, tk)], ones], axis=0)
        acc_ref[:, cols] = a * acc_ref[:, cols] + jnp.dot(v_aug, p.astype(BF16),
                                                          preferred_element_type=F32)
        m_ref[:, cols] = m_new

    m_ref[...] = jnp.full_like(m_ref, -jnp.inf)
    acc_ref[...] = jnp.zeros_like(acc_ref)

    @pl.when(qi == 0)
    def _():
        score(0, buf_a, diagonal=True)

    @pl.when(qi > 0)
    def _():
        score(0, buf_a)

        def pair(g, c):
            score(2 * g + 1, buf_b)
            absorb(2 * g, buf_a)
            score(2 * g + 2, buf_a)
            absorb(2 * g + 1, buf_b)
            return c

        lax.fori_loop(0, qi - 1, pair, 0)
        score(2 * qi - 1, buf_b)
        absorb(2 * qi - 2, buf_a)
        score(2 * qi, buf_a, diagonal=True)
        absorb(2 * qi - 1, buf_b)

    score(2 * qi + 1, buf_b, diagonal=True, late=True)
    absorb(2 * qi, buf_a)
    absorb(2 * qi + 1, buf_b, late=True)
    o_ref[...] = jnp.transpose(acc_ref[0:dh, :] / acc_ref[dh:dh + 1, :]).astype(o_ref.dtype)


def _fox_attention(qk, vt, c_heads, tq, batch, seq, heads, col_q, col_k, to_cast):
    n = qk.shape[0]
    dh = FOX_HEAD_DIM
    tk = tq // 2
    nq, nk = seq // tq, seq // tk
    assert nk <= LANES
    c_qrows = c_heads.reshape(batch, heads, nq, 1, tq)
    c_krows = c_heads.reshape(batch, heads, nk, tk)
    steps = batch * heads * nq

    def slab(a):
        rb = _cast_rows(a.shape[0], steps)
        last = a.shape[0] // rb - 1
        return pl.BlockSpec((rb, a.shape[1]),
                            lambda b, h, i: (jnp.minimum((b * heads + h) * nq + i, last), 0))
    outs = pl.pallas_call(
        functools.partial(_fox_kernel, scale=dh ** -0.5, n_cast=len(to_cast)),
        grid=(batch, heads, nq),
        in_specs=[pl.BlockSpec((tq, dh), lambda b, h, i: (b * nq + i, col_q + h)),
                  pl.BlockSpec((seq, dh), lambda b, h, i: (b, col_k + h)),
                  pl.BlockSpec((dh, seq), lambda b, h, i: (h, b)),
                  pl.BlockSpec((None, None, None, 1, tq), lambda b, h, i: (b, h, i, 0, 0)),
                  pl.BlockSpec((None, None, nk, tk), lambda b, h, i: (b, h, 0, 0))]
                 + [slab(a) for a in to_cast],
        out_specs=[pl.BlockSpec((tq, dh), lambda b, h, i: (b * nq + i, h))] + [slab(a) for a in to_cast],
        out_shape=[jax.ShapeDtypeStruct((n, heads * dh), BF16)]
                  + [jax.ShapeDtypeStruct(a.shape, BF16) for a in to_cast],
        scratch_shapes=[pltpu.VMEM((tq // 2, tq), F32), pltpu.VMEM((tq // 2, tq), F32),
                        pltpu.VMEM((1, tq), F32), pltpu.VMEM((1, tq), F32),
                        pltpu.VMEM((1, tq), F32), pltpu.VMEM((dh + BF16_SUBLANES, tq), F32),
                        pltpu.VMEM((nk, tk, 1), F32)],
        compiler_params=_params("arbitrary", "arbitrary", "arbitrary"),
        name="fox_attention",
    )(qk, qk, vt, c_qrows, c_krows, *to_cast)
    return outs[0], outs[1:]


def _cast_rows(rows, steps):
    rb = max(BF16_SUBLANES, -(-rows // steps))
    return rb if rows % rb == 0 and rb % BF16_SUBLANES == 0 else None


def _cast_plan(arrays, steps):
    views = [a.reshape(-1, a.shape[-1]) for a in arrays]
    return views if all(_cast_rows(v.shape[0], steps) for v in views) else None


def _mlstm_kernel(q_ref, qh_ref, k_ref, kh_ref, cwq_ref, cwk_ref, cbq_ref, cbk_ref, v_ref, gate_ref,
                  brow_ref, irow_ref, mo_ref, g_ref, o_ref, c_ref, n_ref, m_ref, *, lane_i, lane_b):
    L = q_ref.shape[0]
    h = pl.program_id(1)
    first_chunk = pl.program_id(2) == 0

    @pl.when(first_chunk)
    def _():
        c_ref[...] = jnp.zeros_like(c_ref)
        n_ref[...] = jnp.zeros_like(n_ref)
        m_ref[...] = jnp.zeros_like(m_ref)

    def conv_silu(x_ref, halo_ref, w_ref, b_ref):
        kw, halo = w_ref.shape[0], halo_ref.shape[0]
        ext = jnp.concatenate([jnp.where(first_chunk, 0.0, halo_ref[...]), x_ref[...]], axis=0)
        w = w_ref[...]
        acc = None
        for j in range(kw):
            off = halo - (kw - 1) + j
            term = w[j:j + 1, :] * ext[off:off + L, :]
            acc = term if acc is None else acc + term
        acc = acc + b_ref[...]
        return acc * jax.nn.sigmoid(acc)

    q32 = conv_silu(q_ref, qh_ref, cwq_ref, cbq_ref)
    k32 = conv_silu(k_ref, kh_ref, cwk_ref, cbk_ref) * (ML_QK_DIM ** -0.5)

    gates = gate_ref[...]
    lane = lax.broadcasted_iota(jnp.int32, gates.shape, 1)
    pick = lambda col: jnp.sum(jnp.where(lane == col + h, gates, 0.0), axis=-1, keepdims=True)
    icol = pick(lane_i)
    bcol = pick(lane_b)
    brow = brow_ref[...]
    irow = irow_ref[...]
    m_prev = m_ref[0:1, 0:1]

    q = q32.astype(BF16)
    v = v_ref[...]

    tri = lax.broadcasted_iota(jnp.int32, (L, L), 0) >= lax.broadcasted_iota(jnp.int32, (L, L), 1)
    d = jnp.where(tri, bcol - brow + irow, -jnp.inf)
    inter = bcol + m_prev
    m = jnp.maximum(inter, jnp.max(d, axis=-1, keepdims=True))
    w_intra = jnp.exp(d - m)
    w_inter = jnp.exp(inter - m)
    s = lax.dot_general(q, k32.astype(BF16), (((1,), (1,)), ((), ())), preferred_element_type=F32)
    sc = s * w_intra
    num = (jnp.dot(sc.astype(BF16), v, preferred_element_type=F32)
           + w_inter * jnp.dot(q, c_ref[...].astype(BF16), preferred_element_type=F32))
    qn = jnp.sum(q32 * n_ref[...], axis=-1, keepdims=True)
    den = jnp.sum(sc, axis=-1, keepdims=True) + w_inter * qn
    hid = num / jnp.maximum(jnp.abs(den), jnp.exp(-m))

    b_last = bcol[L - 1:L, :]
    gcol = b_last - bcol + icol
    m_new = jnp.maximum(b_last + m_prev, jnp.max(gcol, axis=0, keepdims=True))
    w_k = jnp.exp(gcol - m_new)
    decay = jnp.exp(b_last + m_prev - m_new)
    kw = k32 * w_k
    c_ref[...] = decay * c_ref[...] + lax.dot_general(
        kw.astype(BF16), v, (((0,), (0,)), ((), ())), preferred_element_type=F32)
    n_ref[...] = decay * n_ref[...] + jnp.sum(kw, axis=0, keepdims=True)
    m_ref[...] = jnp.broadcast_to(m_new, m_ref.shape)

    mu = jnp.mean(hid, axis=-1, keepdims=True)
    hc = hid - mu
    var = jnp.mean(hc * hc, axis=-1, keepdims=True)
    hn = hc * lax.rsqrt(var + LN_EPS)
    o_ref[...] = (hn * g_ref[...] * jax.nn.sigmoid(mo_ref[...])).astype(o_ref.dtype)


def _mlstm(qk, conv_w, conv_b, qkv, proj, gates, brow, irow, norm_g, batch, seq, chunk, col_v, col_mo,
           lane_i, lane_b):
    n = qk.shape[0]
    dk, dv = ML_QK_DIM, ML_V_DIM
    nc = seq // chunk
    kw = conv_w.shape[0]
    halo = F32_SUBLANES
    assert kw - 1 <= halo and chunk % halo == 0
    row = lambda b, h, c: b * nc + c
    before = lambda b, h, c: jnp.maximum(row(b, h, c) * (chunk // halo) - 1, 0)
    vec = pl.BlockSpec((None, None, None, 1, chunk), lambda b, h, c: (b, h, c, 0, 0))
    conv_b = conv_b.reshape(1, -1)
    return pl.pallas_call(
        functools.partial(_mlstm_kernel, lane_i=lane_i, lane_b=lane_b),
        grid=(batch, ML_HEADS, nc),
        in_specs=[pl.BlockSpec((chunk, dk), lambda b, h, c: (row(b, h, c), h)),
                  pl.BlockSpec((halo, dk), lambda b, h, c: (before(b, h, c), h)),
                  pl.BlockSpec((chunk, dk), lambda b, h, c: (row(b, h, c), ML_HEADS + h)),
                  pl.BlockSpec((halo, dk), lambda b, h, c: (before(b, h, c), ML_HEADS + h)),
                  pl.BlockSpec((kw, dk), lambda b, h, c: (0, h)),
                  pl.BlockSpec((kw, dk), lambda b, h, c: (0, ML_HEADS + h)),
                  pl.BlockSpec((1, dk), lambda b, h, c: (0, h)),
                  pl.BlockSpec((1, dk), lambda b, h, c: (0, ML_HEADS + h)),
                  pl.BlockSpec((chunk, dv), lambda b, h, c: (row(b, h, c), col_v + h)),
                  pl.BlockSpec((chunk, gates.shape[1]), lambda b, h, c: (row(b, h, c), 0)),
                  vec, vec,
                  pl.BlockSpec((chunk, dv), lambda b, h, c: (row(b, h, c), col_mo + h)),
                  pl.BlockSpec((1, dv), lambda b, h, c: (0, h))],
        out_specs=pl.BlockSpec((chunk, dv), lambda b, h, c: (row(b, h, c), h)),
        out_shape=jax.ShapeDtypeStruct((n, ML_HEADS * dv), BF16),
        scratch_shapes=[pltpu.VMEM((dk, dv), F32), pltpu.VMEM((1, dk), F32), pltpu.VMEM((8, 128), F32)],
        compiler_params=_params("parallel", "parallel", "arbitrary"),
        name="mlstm",
    )(qk, qk, qk, qk, conv_w, conv_w, conv_b, conv_b, qkv, gates, brow, irow, proj,
      norm_g.reshape(1, ML_HEADS * dv))


def _merge_kernel(yf_ref, ym_ref, wf_ref, wm_ref, ga_ref, gb_ref, o_ref):
    a = jnp.dot(yf_ref[...], wf_ref[...], preferred_element_type=F32)
    b = jnp.dot(ym_ref[...], wm_ref[...], preferred_element_type=F32)
    o_ref[...] = (jax.nn.sigmoid(ga_ref[...]) * a + jax.nn.sigmoid(gb_ref[...]) * b).astype(o_ref.dtype)


def _merge(y_fox, y_ml, w_fox, w_ml, proj, col_ga, col_gb):
    n, kf = y_fox.shape
    km = y_ml.shape[1]
    d = w_fox.shape[1]
    bm, bn = _tile(n, MERGE_ROW_BLOCK), _tile(d, MATMUL_BLOCK)
    ga0, gb0 = col_ga // bn, col_gb // bn
    return pl.pallas_call(
        _merge_kernel,
        grid=(n // bm, d // bn),
        in_specs=[pl.BlockSpec((bm, kf), lambda i, j: (i, 0)),
                  pl.BlockSpec((bm, km), lambda i, j: (i, 0)),
                  pl.BlockSpec((kf, bn), lambda i, j: (0, j)),
                  pl.BlockSpec((km, bn), lambda i, j: (0, j)),
                  pl.BlockSpec((bm, bn), lambda i, j: (i, ga0 + j)),
                  pl.BlockSpec((bm, bn), lambda i, j: (i, gb0 + j))],
        out_specs=pl.BlockSpec((bm, bn), lambda i, j: (i, j)),
        out_shape=jax.ShapeDtypeStruct((n, d), BF16),
        compiler_params=_params("parallel", "parallel"),
        name="merge_proj",
    )(y_fox, y_ml, w_fox, w_ml, proj, proj)


def _split_bf16(v):
    hi = v.astype(BF16)
    return hi, (v - hi.astype(F32)).astype(BF16)


def _route_rows(x, w_ref, b_ref, oi_ref, of_ref, cnt_ref, carry_ref, whi_ref, wlo_ref,
                *, n_groups, per_group):
    @pl.when(pl.program_id(0) == 0)
    def _():
        carry_ref[...] = jnp.zeros_like(carry_ref)
        whi_ref[...], wlo_ref[...] = _split_bf16(w_ref[...])

    tr = x.shape[0]
    n_exp = n_groups * per_group
    x_hi, x_lo = _split_bf16(x)
    dot = lambda a, b: jnp.dot(a, b, preferred_element_type=F32)
    logits = (dot(x_hi, whi_ref[...]) + dot(x_lo, whi_ref[...]) + dot(x_hi, wlo_ref[...])) + b_ref[...]
    lane = lax.broadcasted_iota(jnp.int32, logits.shape, 1)
    big = jnp.int32(logits.shape[1])
    first_lane = lambda cond: jnp.min(jnp.where(cond, lane, big), axis=-1, keepdims=True)

    is_group = (lane >= n_exp) & (lane < n_exp + n_groups)
    gl = jnp.where(is_group, logits, -jnp.inf)
    g_max = jnp.max(gl, axis=-1, keepdims=True)
    g_sel = first_lane(gl == g_max) - n_exp
    p_g_sel = 1.0 / jnp.sum(jnp.exp(gl - g_max), axis=-1, keepdims=True)

    in_group = (lane >= g_sel * per_group) & (lane < (g_sel + 1) * per_group)
    el = jnp.where(in_group, logits, -jnp.inf)
    e_max = jnp.max(el, axis=-1, keepdims=True)
    ee = jnp.exp(el - e_max)
    pe = jnp.where(in_group, ee / jnp.sum(ee, axis=-1, keepdims=True), -1.0)
    p1 = jnp.max(pe, axis=-1, keepdims=True)
    e1 = first_lane(pe == p1)
    pe2 = jnp.where(lane == e1, -1.0, pe)
    p2 = jnp.max(pe2, axis=-1, keepdims=True)
    e2 = first_lane(pe2 == p2)
    p_sum = p1 + p2
    w1 = p_g_sel * p1 / p_sum
    w2 = p_g_sel * p2 / p_sum

    onehot = (lane == e1) | (lane == e2)
    strict = (lax.broadcasted_iota(jnp.int32, (tr, tr), 0)
              > lax.broadcasted_iota(jnp.int32, (tr, tr), 1)).astype(BF16)
    before = jnp.dot(strict, onehot.astype(BF16), preferred_element_type=F32) + carry_ref[...]
    r1 = jnp.sum(jnp.where(lane == e1, before, 0.0), axis=-1, keepdims=True).astype(jnp.int32)
    r2 = jnp.sum(jnp.where(lane == e2, before, 0.0), axis=-1, keepdims=True).astype(jnp.int32)
    carry_ref[...] += jnp.sum(onehot.astype(F32), axis=0, keepdims=True)
    cnt_ref[...] = carry_ref[...].astype(jnp.int32)

    oi_ref[...] = jnp.where(lane == 0, e1, jnp.where(lane == 1, e2, jnp.where(lane == 2, r1, r2)))
    of_ref[...] = jnp.where(lane == 0, w1, w2)


def _expert_kernel(blk_e_ref, n_used_ref, tok_ref, tok_next_ref, x_hbm, wt_ref, wg_ref, wu_ref, wd_ref,
                   o_ref, xbuf, sem):
    b = pl.program_id(0)
    tb, half = xbuf.shape[1:]
    n_used = n_used_ref[0]
    slot = b % 2

    def row_copy(ids_ref, r, s):
        return pltpu.make_async_copy(x_hbm.at[pl.ds(ids_ref[r], 1), :],
                                     xbuf.at[s, pl.ds(r, 1), :], sem.at[s])

    def wait_rows():
        pltpu.make_async_copy(x_hbm.at[pl.ds(0, tb), :], xbuf.at[slot], sem.at[slot]).wait()

    def compute():
        lo, hi = _unpack_bf16_pairs(xbuf[slot])
        lo, hi = lo.astype(BF16), hi.astype(BF16)
        proj = lambda w_ref: (jnp.dot(lo, w_ref[0:half, :], preferred_element_type=F32)
                              + jnp.dot(hi, w_ref[half:2 * half, :], preferred_element_type=F32))
        gate, up = proj(wg_ref), proj(wu_ref)
        act = (gate * jax.nn.sigmoid(gate) * up).astype(BF16)
        out = jnp.dot(act, wd_ref[...], preferred_element_type=F32) * wt_ref[...]
        o_ref[...] = _pack_bf16_pairs(out)

    @pl.when((b == 0) & (n_used > 0))
    def _():
        def start(r, c):
            row_copy(tok_ref, r, slot).start()
            return c
        lax.fori_loop(0, tb, start, 0)

    @pl.when(b + 1 < n_used)
    def _():
        wait_rows()
        for r in range(tb):
            row_copy(tok_next_ref, r, 1 - slot).start()
        compute()

    @pl.when(b + 1 == n_used)
    def _():
        wait_rows()
        compute()

    @pl.when(b >= n_used)
    def _():
        o_ref[...] = jnp.zeros_like(o_ref)


def _experts(xp, slot_tok, slot_w, block_e, n_used, w_gate, w_up, w_down, tb):
    n, half = xp.shape
    _, d, f = w_gate.shape
    assert d == 2 * half
    n_blocks = block_e.shape[0]
    smem_tb = lambda shift: pl.BlockSpec(
        (tb,), lambda b, *_: (jnp.minimum(b + shift, n_blocks - 1),), memory_space=pltpu.SMEM)
    grid_spec = pltpu.PrefetchScalarGridSpec(
        num_scalar_prefetch=2,
        grid=(n_blocks,),
        in_specs=[smem_tb(0), smem_tb(1),
                  pl.BlockSpec(memory_space=pl.ANY),
                  pl.BlockSpec((tb, 1), lambda b, *_: (b, 0)),
                  pl.BlockSpec((None, d, f), lambda b, be, nu: (be[b], 0, 0)),
                  pl.BlockSpec((None, d, f), lambda b, be, nu: (be[b], 0, 0)),
                  pl.BlockSpec((None, f, d), lambda b, be, nu: (be[b], 0, 0))],
        out_specs=pl.BlockSpec((tb, half), lambda b, *_: (b, 0)),
        scratch_shapes=[pltpu.VMEM((2, tb, half), jnp.int32), pltpu.SemaphoreType.DMA((2,))],
    )
    return pl.pallas_call(
        _expert_kernel,
        grid_spec=grid_spec,
        out_shape=jax.ShapeDtypeStruct((n_blocks * tb, half), jnp.int32),
        compiler_params=_params("arbitrary"),
        name="experts",
    )(block_e, n_used, slot_tok, slot_tok, xp, slot_w.reshape(-1, 1), w_gate, w_up, w_down)


def _combine_kernel(dest_ref, dest_next_ref, y_hbm, h_ref, g_ref, b_ref, o_ref, buf, sem, *, alpha):
    i = pl.program_id(0)
    last = pl.num_programs(0) - 1
    tm = h_ref.shape[0]
    slot = i % 2

    def row_copy(ids_ref, r, k, s):
        return pltpu.make_async_copy(y_hbm.at[pl.ds(ids_ref[r * TOP_K + k], 1), :],
                                     buf.at[s, k, pl.ds(r, 1), :], sem.at[s])

    def compute():
        y = None
        for k in range(TOP_K):
            y_k = jnp.concatenate(_unpack_bf16_pairs(buf[slot, k]), axis=1)
            y = y_k if y is None else y + y_k
        o_ref[...] = _layer_norm(alpha * h_ref[...] + y, g_ref[...], b_ref[...])

    @pl.when(i == 0)
    def _():
        def start(r, c):
            for k in range(TOP_K):
                row_copy(dest_ref, r, k, slot).start()
            return c
        lax.fori_loop(0, tm, start, 0)

    for k in range(TOP_K):
        pltpu.make_async_copy(y_hbm.at[pl.ds(0, tm), :], buf.at[slot, k], sem.at[slot]).wait()

    @pl.when(i < last)
    def _():
        for r in range(tm):
            for k in range(TOP_K):
                row_copy(dest_next_ref, r, k, 1 - slot).start()
        compute()

    @pl.when(i == last)
    def _():
        compute()


def _combine_ln(h, y_slots, dest, g, b, alpha):
    n, d = h.shape
    tm = _tile(n, ROW_TILE)
    tiles = n // tm
    return pl.pallas_call(
        functools.partial(_combine_kernel, alpha=alpha),
        grid=(tiles,),
        in_specs=[pl.BlockSpec((tm * TOP_K,), lambda i: (i,), memory_space=pltpu.SMEM),
                  pl.BlockSpec((tm * TOP_K,), lambda i: (jnp.minimum(i + 1, tiles - 1),),
                               memory_space=pltpu.SMEM),
                  pl.BlockSpec(memory_space=pl.ANY),
                  pl.BlockSpec((tm, d), lambda i: (i, 0)),
                  pl.BlockSpec((1, d), lambda i: (0, 0)),
                  pl.BlockSpec((1, d), lambda i: (0, 0))],
        out_specs=pl.BlockSpec((tm, d), lambda i: (i, 0)),
        out_shape=jax.ShapeDtypeStruct((n, d), F32),
        scratch_shapes=[pltpu.VMEM((2, TOP_K, tm, d // 2), jnp.int32), pltpu.SemaphoreType.DMA((2,))],
        compiler_params=_params("arbitrary"),
        name="combine_ln",
    )(dest, dest, y_slots, h, g.reshape(1, d), b.reshape(1, d))


EXPERT_SLOT_BLOCK = 256
ML_CHUNK = 256
FOX_KEY_BLOCK = 512


def _mixer(h16, batch, seq, w_in, b_fox_f, b_ml_i, b_ml_f, conv_w, conv_b, ml_norm_g,
           w_proj_fox, w_proj_ml, w_out, expert_w):
    n, d = h16.shape
    fox_w = d // 2
    fox_heads = fox_w // FOX_HEAD_DIM
    qk_w = ML_HEADS * ML_QK_DIM
    v_w = ML_HEADS * ML_V_DIM
    widths = (fox_w, fox_w, fox_w, fox_heads, qk_w, qk_w, v_w, ML_HEADS, ML_HEADS, v_w, d, d)
    offs = [0]
    for w in widths:
        offs.append(offs[-1] + w)
    n_gate = fox_heads + 2 * ML_HEADS
    gate_bias = jnp.pad(jnp.concatenate([b_fox_f, b_ml_i, b_ml_f]), (0, GATE_LANES - n_gate)).reshape(1, -1)

    wt = jnp.transpose(w_in)
    proj_fox = _in_proj(h16, wt, offs[0], 2 * fox_w, BF16, "in_proj_fox")
    v_t = _in_proj(h16, wt, offs[2], fox_w, BF16, "in_proj_fv", features_major=True)
    proj_qk = _in_proj(h16, wt, offs[4], 2 * qk_w, F32, "in_proj_qk")
    proj_mv = _in_proj(h16, wt, offs[6], v_w, BF16, "in_proj_mv")
    proj_og = _in_proj(h16, wt, offs[9], v_w + 2 * d, F32, "in_proj_og")
    gate_pre = _gate_proj(h16, wt, offs[3], offs[7], fox_heads)

    chunk = _tile(seq, ML_CHUNK)
    gates = _gates(gate_pre, gate_bias, batch, chunk, fox_heads)

    g3 = gates.reshape(batch, seq, GATE_LANES)
    tq = 2 * _tile(seq // 2, FOX_KEY_BLOCK)
    c_heads = jnp.transpose(g3[:, :, :fox_heads], (0, 2, 1))
    lane_i, lane_b = fox_heads, fox_heads + ML_HEADS
    rows = lambda lo: jnp.transpose(g3[:, :, lo:lo + ML_HEADS], (0, 2, 1)).reshape(
        batch, ML_HEADS, seq // chunk, 1, chunk)
    irow, brow = rows(lane_i), rows(lane_b)

    later_w = [w_proj_fox, w_proj_ml, w_out] + list(expert_w)
    views = _cast_plan(later_w, batch * fox_heads * (seq // tq))
    y_fox, cast = _fox_attention(proj_fox, v_t, c_heads, tq, batch, seq, fox_heads,
                                 col_q=0, col_k=fox_heads, to_cast=views or [])
    if views:
        later_w16 = [c.reshape(a.shape) for c, a in zip(cast, later_w)]
    else:
        later_w16 = [a.astype(BF16) for a in later_w]
    w_proj_fox16, w_proj_ml16, w_out16 = later_w16[:3]
    y_ml = _mlstm(proj_qk, conv_w, conv_b, proj_mv, proj_og, gates, brow, irow, ml_norm_g, batch, seq,
                  chunk, col_v=0, col_mo=0, lane_i=lane_i, lane_b=lane_b)
    merged = _merge(y_fox, y_ml, w_proj_fox16, w_proj_ml16, proj_og, col_ga=v_w, col_gb=v_w + d)
    return _matmul(merged, w_out16, F32, "out_proj"), later_w16[3:]


def _route_params(w_group, b_group, w_router, b_router):
    n_groups, n_exp = w_group.shape[1], w_router.shape[1]
    assert n_exp + n_groups <= GATE_LANES
    pad = GATE_LANES - n_exp - n_groups
    w_route = jnp.pad(jnp.concatenate([w_router, w_group], axis=1), ((0, 0), (0, pad)))
    b_route = jnp.pad(jnp.concatenate([b_router, b_group]), (0, pad)).reshape(1, GATE_LANES)
    return w_route, b_route, n_groups, n_exp // n_groups


def _moe(h, h_packed, routing, n_exp, w_gate, w_up, w_down, ln_g, ln_b, alpha):
    n, d = h.shape
    oi, of, cnt = routing
    e_idx, rank, gate_w = oi[:, 0:TOP_K], oi[:, TOP_K:2 * TOP_K], of[:, 0:TOP_K]
    counts = cnt[0, :n_exp]

    tb = EXPERT_SLOT_BLOCK
    n_assign = n * TOP_K
    n_blocks = (n_assign + n_exp * (tb - 1) + tb - 1) // tb
    padded = (counts + tb - 1) // tb * tb
    pad_ends = jnp.cumsum(padded)
    onehot = (e_idx[..., None] == jnp.arange(n_exp, dtype=jnp.int32)).astype(F32)
    seg_start = jnp.einsum("nke,e->nk", onehot, (pad_ends - padded).astype(F32),
                           precision=lax.Precision.HIGHEST).astype(jnp.int32)
    dest = (seg_start + rank).reshape(-1)
    tok = jnp.repeat(jnp.arange(n, dtype=jnp.int32), TOP_K)
    pairs = jnp.stack([tok, lax.bitcast_convert_type(gate_w.reshape(-1), jnp.int32)], axis=1)
    slots = jnp.zeros((n_blocks * tb, 2), jnp.int32).at[dest].set(pairs)
    slot_tok = slots[:, 0]
    slot_w = lax.bitcast_convert_type(slots[:, 1], F32)
    n_used = pad_ends[-1] // tb
    blk = jnp.arange(n_blocks, dtype=jnp.int32)
    block_e = jnp.minimum(jnp.searchsorted(pad_ends, blk * tb, side="right"), n_exp - 1).astype(jnp.int32)
    block_e = jnp.where(blk < n_used, block_e, block_e[jnp.maximum(n_used - 1, 0)])

    y_slots = _experts(h_packed, slot_tok, slot_w, block_e, n_used.reshape(1).astype(jnp.int32),
                       w_gate, w_up, w_down, tb)
    return _combine_ln(h, y_slots, dest.astype(jnp.int32), ln_g, ln_b, alpha)


def kernel(x, ln_in_g, ln_in_b, w_in, b_fox_f, b_ml_i, b_ml_f, conv_w, conv_b, ml_norm_g, w_proj_fox, w_proj_ml, w_out, ln_mix_g, ln_mix_b, w_group, b_group, w_router, b_router, w_gate, w_up, w_down, ln_moe_g, ln_moe_b):
    batch, seq, d = x.shape
    depth = w_in.shape[0]
    alpha = (2 * depth) ** 0.25
    h32 = x.reshape(batch * seq, d)
    h16 = _ln_in(h32, ln_in_g, ln_in_b)
    for l in range(depth):
        mix, expert_w16 = _mixer(h16, batch, seq, w_in[l], b_fox_f[l], b_ml_i[l], b_ml_f[l],
                                 conv_w[l], conv_b[l], ml_norm_g[l], w_proj_fox[l], w_proj_ml[l],
                                 w_out[l], [w_gate[l], w_up[l], w_down[l]])
        w_route, b_route, n_groups, per_group = _route_params(w_group[l], b_group[l],
                                                              w_router[l], b_router[l])
        h32, h_packed, *routing = _ln_route(h32, mix, ln_mix_g[l], ln_mix_b[l], alpha,
                                            w_route, b_route, n_groups, per_group,
                                            pre_norm=(ln_in_g, ln_in_b) if l == 0 else None)
        h32 = _moe(h32, h_packed, routing, n_groups * per_group, *expert_w16,
                   ln_moe_g[l], ln_moe_b[l], alpha)
        if l + 1 < depth:
            h16 = h32.astype(BF16)
    return h32.reshape(batch, seq, d)
```

```python
import functools

import jax
import jax.numpy as jnp
from jax import lax
from jax.experimental import pallas as pl
from jax.experimental.pallas import tpu as pltpu

F32 = jnp.float32
BF16 = jnp.bfloat16

LN_EPS = 1e-5
FOX_HEAD_DIM = 128
ML_HEADS = 4
ML_QK_DIM = 256
ML_V_DIM = 512
TOP_K = 2
LANES = 128
F32_SUBLANES = 8
BF16_SUBLANES = 16
GATE_LANES = LANES
LOG2_E = 1.4426950408889634

V7X_VMEM_BYTES = 64 * 1024 * 1024
VMEM_LIMIT_BYTES = V7X_VMEM_BYTES - 8 * 1024 * 1024

MATMUL_BLOCK = 1024
MERGE_ROW_BLOCK = 512
ROW_TILE = 256


def _params(*sem):
    return pltpu.CompilerParams(dimension_semantics=sem, vmem_limit_bytes=VMEM_LIMIT_BYTES)


def _tile(dim, pref):
    t = min(dim, pref)
    while dim % t:
        t //= 2
    return t


def _layer_norm(x, g, b):
    mu = jnp.mean(x, axis=-1, keepdims=True)
    xc = x - mu
    var = jnp.mean(xc * xc, axis=-1, keepdims=True)
    return xc * lax.rsqrt(var + LN_EPS) * g + b


def _log_sigmoid(x):
    return jnp.minimum(x, 0.0) - jnp.log1p(jnp.exp(-jnp.abs(x)))


def _ln_in_kernel(x_ref, g_ref, b_ref, o16_ref):
    o16_ref[...] = _layer_norm(x_ref[...], g_ref[...], b_ref[...]).astype(BF16)


def _ln_in(x, g, b):
    n, d = x.shape
    tr = _tile(n, ROW_TILE)
    row = pl.BlockSpec((tr, d), lambda i: (i, 0))
    vec = pl.BlockSpec((1, d), lambda i: (0, 0))
    return pl.pallas_call(
        _ln_in_kernel,
        grid=(n // tr,),
        in_specs=[row, vec, vec],
        out_specs=row,
        out_shape=jax.ShapeDtypeStruct((n, d), BF16),
        compiler_params=_params("parallel"),
        name="ln_in",
    )(x, g.reshape(1, d), b.reshape(1, d))


def _pack_bf16_pairs(x):
    half = x.shape[1] // 2
    bits = lambda v: lax.bitcast_convert_type(v.astype(BF16).astype(F32), jnp.uint32)
    word = lax.shift_right_logical(bits(x[:, :half]), jnp.uint32(16)) | bits(x[:, half:])
    return lax.bitcast_convert_type(word, jnp.int32)


def _unpack_bf16_pairs(w):
    u = lax.bitcast_convert_type(w, jnp.uint32)
    lo = lax.bitcast_convert_type(lax.shift_left(u, jnp.uint32(16)), F32)
    hi = lax.bitcast_convert_type(u & jnp.uint32(0xFFFF0000), F32)
    return lo, hi


def _ln_route_kernel(*refs, alpha, pre_norm, n_groups, per_group):
    if pre_norm:
        h_ref, g0_ref, b0_ref, *refs = refs
        h = _layer_norm(h_ref[...], g0_ref[...], b0_ref[...])
    else:
        h_ref, *refs = refs
        h = h_ref[...]
    y_ref, g_ref, b_ref, w_ref, rb_ref, o_ref, op_ref, *route_refs = refs
    out = _layer_norm(alpha * h + y_ref[...], g_ref[...], b_ref[...])
    o_ref[...] = out
    op_ref[...] = _pack_bf16_pairs(out)
    _route_rows(out, w_ref, rb_ref, *route_refs, n_groups=n_groups, per_group=per_group)


def _ln_route(h, y, g, b, alpha, w_route, b_route, n_groups, per_group, pre_norm=None):
    n, d = h.shape
    lanes = w_route.shape[1]
    tr = _tile(n, ROW_TILE)
    row = pl.BlockSpec((tr, d), lambda i: (i, 0))
    vec = pl.BlockSpec((1, d), lambda i: (0, 0))
    lrow = pl.BlockSpec((tr, lanes), lambda i: (i, 0))
    lvec = pl.BlockSpec((1, lanes), lambda i: (0, 0))
    pre = [v.reshape(1, d) for v in pre_norm] if pre_norm else []
    return pl.pallas_call(
        functools.partial(_ln_route_kernel, alpha=alpha, pre_norm=bool(pre_norm),
                          n_groups=n_groups, per_group=per_group),
        grid=(n // tr,),
        in_specs=[row] + [vec] * len(pre) + [row, vec, vec,
                                             pl.BlockSpec((d, lanes), lambda i: (0, 0)), lvec],
        out_specs=[row, pl.BlockSpec((tr, d // 2), lambda i: (i, 0)), lrow, lrow, lvec],
        out_shape=[jax.ShapeDtypeStruct((n, d), F32), jax.ShapeDtypeStruct((n, d // 2), jnp.int32),
                   jax.ShapeDtypeStruct((n, lanes), jnp.int32),
                   jax.ShapeDtypeStruct((n, lanes), F32),
                   jax.ShapeDtypeStruct((1, lanes), jnp.int32)],
        scratch_shapes=[pltpu.VMEM((1, lanes), F32),
                        pltpu.VMEM((d, lanes), BF16), pltpu.VMEM((d, lanes), BF16)],
        compiler_params=_params("arbitrary"),
        name="ln_mix_route",
    )(h, *pre, y, g.reshape(1, d), b.reshape(1, d), w_route, b_route)


def _mm_kernel(x_ref, w_ref, o_ref):
    o_ref[...] = jnp.dot(x_ref[...], w_ref[...], preferred_element_type=F32).astype(o_ref.dtype)


def _matmul(x, w, out_dtype, name):
    m, k = x.shape
    _, n = w.shape
    bm, bn = _tile(m, MATMUL_BLOCK), _tile(n, MATMUL_BLOCK)
    return pl.pallas_call(
        _mm_kernel,
        grid=(m // bm, n // bn),
        in_specs=[pl.BlockSpec((bm, k), lambda i, j: (i, 0)),
                  pl.BlockSpec((k, bn), lambda i, j: (0, j))],
        out_specs=pl.BlockSpec((bm, bn), lambda i, j: (i, j)),
        out_shape=jax.ShapeDtypeStruct((m, n), out_dtype),
        compiler_params=_params("parallel", "parallel"),
        name=name,
    )(x, w)


_NT = (((1,), (1,)), ((), ()))


def _in_proj_kernel(*refs, off, n_col_blocks, features_major):
    if off:
        x_ref, w_ref, wx_ref, o_ref, w16_ref = refs
    else:
        x_ref, w_ref, o_ref, w16_ref = refs
    rc = w_ref.shape[0]
    j, i = pl.program_id(0), pl.program_id(1)

    @pl.when(j < n_col_blocks)
    def _():
        w = w_ref[...]
        if off:
            w = jnp.concatenate([w, wx_ref[...]], axis=0)[off:off + rc, :]
        w16_ref[j % 2, pl.ds(pl.multiple_of(i * rc, rc), rc), :] = w.astype(BF16)

    @pl.when(j > 0)
    def _():
        a, b = x_ref[...], w16_ref[(j - 1) % 2]
        if features_major:
            a, b = b, a
        o_ref[...] = lax.dot_general(a, b, _NT, preferred_element_type=F32).astype(o_ref.dtype)


def _in_proj(x, wt, col_start, n_cols, out_dtype, name, features_major=False):
    m, k = x.shape
    bm, bn = _tile(m, MATMUL_BLOCK), _tile(n_cols, MATMUL_BLOCK)
    ni, nj = m // bm, n_cols // bn
    rc = bn // ni
    assert bn % ni == 0 and rc % BF16_SUBLANES == 0
    off = col_start % rc
    assert off % F32_SUBLANES == 0
    blk0 = (col_start - off) // rc
    chunk = lambda j, i: blk0 + jnp.minimum(j, nj - 1) * ni + i
    row = lambda j, i: jnp.where(j == 0, 0, i)
    in_specs = [pl.BlockSpec((bm, k), lambda j, i: (row(j, i), 0)),
                pl.BlockSpec((rc, k), lambda j, i: (chunk(j, i), 0))]
    args = [x, wt]
    if off:
        in_specs.append(pl.BlockSpec((rc, k), lambda j, i: (chunk(j, i) + 1, 0)))
        args.append(wt)
    if features_major:
        out_spec = pl.BlockSpec((bn, bm), lambda j, i: (jnp.maximum(j - 1, 0), row(j, i)))
        out_shape = jax.ShapeDtypeStruct((n_cols, m), out_dtype)
    else:
        out_spec = pl.BlockSpec((bm, bn), lambda j, i: (row(j, i), jnp.maximum(j - 1, 0)))
        out_shape = jax.ShapeDtypeStruct((m, n_cols), out_dtype)
    return pl.pallas_call(
        functools.partial(_in_proj_kernel, off=off, n_col_blocks=nj, features_major=features_major),
        grid=(nj + 1, ni),
        in_specs=in_specs,
        out_specs=out_spec,
        out_shape=out_shape,
        scratch_shapes=[pltpu.VMEM((2, bn, k), BF16)],
        compiler_params=_params("arbitrary", "arbitrary"),
        name=name,
    )(*args)


def _gate_proj_kernel(x_ref, wa_ref, wb_ref, o_ref, w16_ref):
    @pl.when(pl.program_id(0) == 0)
    def _():
        pad = w16_ref.shape[0] - wa_ref.shape[0] - wb_ref.shape[0]
        zeros = jnp.zeros((pad, w16_ref.shape[1]), F32)
        w16_ref[...] = jnp.concatenate([wa_ref[...], wb_ref[...], zeros], axis=0).astype(BF16)

    o_ref[...] = lax.dot_general(x_ref[...], w16_ref[...], _NT, preferred_element_type=F32)


def _gate_proj(x, wt, col_ff, col_mi, n_fox):
    m, k = x.shape
    n_ml = 2 * ML_HEADS
    assert n_fox % F32_SUBLANES == 0 and n_ml % F32_SUBLANES == 0 and n_fox + n_ml <= LANES
    assert col_ff % n_fox == 0 and col_mi % n_ml == 0
    bm = _tile(m, MATMUL_BLOCK)
    return pl.pallas_call(
        _gate_proj_kernel,
        grid=(m // bm,),
        in_specs=[pl.BlockSpec((bm, k), lambda i: (i, 0)),
                  pl.BlockSpec((n_fox, k), lambda i: (col_ff // n_fox, 0)),
                  pl.BlockSpec((n_ml, k), lambda i: (col_mi // n_ml, 0))],
        out_specs=pl.BlockSpec((bm, LANES), lambda i: (i, 0)),
        out_shape=jax.ShapeDtypeStruct((m, LANES), F32),
        scratch_shapes=[pltpu.VMEM((LANES, k), BF16)],
        compiler_params=_params("arbitrary"),
        name="in_proj_gates",
    )(x, wt, wt)


def _gates_kernel(g_ref, bias_ref, o_ref, carry_ref, *, n_fox, fox_inv_scale):
    @pl.when(pl.program_id(1) == 0)
    def _():
        carry_ref[...] = jnp.zeros_like(carry_ref)

    ts = g_ref.shape[0]
    x = g_ref[...] + bias_ref[...]
    lane = lax.broadcasted_iota(jnp.int32, x.shape, 1)
    is_input_gate = (lane >= n_fox) & (lane < n_fox + ML_HEADS)
    val = jnp.where(is_input_gate, 0.0, _log_sigmoid(x))
    tril = (lax.broadcasted_iota(jnp.int32, (ts, ts), 0)
            >= lax.broadcasted_iota(jnp.int32, (ts, ts), 1)).astype(BF16)
    hi = val.astype(BF16)
    rem = val - hi.astype(F32)
    mid = rem.astype(BF16)
    lo = (rem - mid.astype(F32)).astype(BF16)
    cs = (jnp.dot(tril, hi, preferred_element_type=F32)
          + jnp.dot(tril, mid, preferred_element_type=F32)
          + jnp.dot(tril, lo, preferred_element_type=F32))
    total = cs + jnp.where(lane < n_fox, carry_ref[...], 0.0)
    carry_ref[...] = total[ts - 1:ts, :]
    total = jnp.where(lane < n_fox, total * fox_inv_scale, total)
    o_ref[...] = jnp.where(is_input_gate, x, total)


def _gates(g, bias, batch, ts, n_fox):
    n, lanes = g.shape
    tiles = n // batch // ts
    blk = pl.BlockSpec((ts, lanes), lambda b, j: (b * tiles + j, 0))
    return pl.pallas_call(
        functools.partial(_gates_kernel, n_fox=n_fox, fox_inv_scale=FOX_HEAD_DIM ** 0.5),
        grid=(batch, tiles),
        in_specs=[blk, pl.BlockSpec((1, lanes), lambda b, j: (0, 0))],
        out_specs=blk,
        out_shape=jax.ShapeDtypeStruct((n, lanes), F32),
        scratch_shapes=[pltpu.VMEM((1, lanes), F32)],
        compiler_params=_params("parallel", "arbitrary"),
        name="gates",
    )(g, bias)


def _fox_kernel(*refs, scale, n_cast):
    q_ref, k_ref, vt_ref, cq_ref, ckr_ref = refs[:5]
    cast_in = refs[5:5 + n_cast]
    o_ref = refs[5 + n_cast]
    cast_out = refs[6 + n_cast:6 + 2 * n_cast]
    ua_ref, ub_ref, mxa_ref, mxb_ref, m_ref, acc_ref, ck_ref = refs[6 + 2 * n_cast:]
    for src, dst in zip(cast_in, cast_out):
        dst[...] = src[...].astype(dst.dtype)

    tq, dh = q_ref.shape
    tk = ua_ref.shape[0]
    qi = pl.program_id(2)

    @pl.when(qi == 0)
    def _():
        nk = ckr_ref.shape[0]
        rows = jnp.concatenate([ckr_ref[...], jnp.zeros((LANES - nk, tk), F32)], axis=0)
        cols = jnp.transpose(rows)
        for j in range(nk):
            ck_ref[j] = cols[:, j:j + 1]

    c_exp = scale * LOG2_E
    q = q_ref[...]
    cq = cq_ref[...]
    buf_a, buf_b = (ua_ref, mxa_ref), (ub_ref, mxb_ref)

    def score(kb, buf, diagonal=False, late=False):
        u_ref, mx_ref = buf
        cols = slice(tk, tq) if late else slice(0, tq)
        nq = cols.stop - cols.start
        ks = pl.multiple_of(kb * tk, tk)
        u = lax.dot_general(k_ref[pl.ds(ks, tk), :], q[cols], (((1,), (1,)), ((), ())),
                            preferred_element_type=F32) - ck_ref[kb]
        if diagonal:
            u = jnp.where(lax.broadcasted_iota(jnp.int32, (tk, nq), 0)
                          <= lax.broadcasted_iota(jnp.int32, (tk, nq), 1), u, -jnp.inf)
        u_ref[:, cols] = u
        mx_ref[:, cols] = jnp.max(u, axis=0, keepdims=True)

    def absorb(kb, buf, late=False):
        u_ref, mx_ref = buf
        cols = slice(tk, tq) if late else slice(0, tq)
        ks = pl.multiple_of(kb * tk, tk)
        m, cq_c = m_ref[:, cols], cq[:, cols]
        m_new = jnp.maximum(m, mx_ref[:, cols] + cq_c)
        p = jnp.exp2((u_ref[:, cols] + (cq_c - m_new)) * c_exp)
        a = jnp.exp2((m - m_new) * c_exp)
        ones = (lax.broadcasted_iota(jnp.int32, (BF16_SUBLANES, tk), 0) == 0).astype(BF16)
        v_aug = jnp.concatenate([vt_ref[:, pl.ds(ks, tk)], ones], axis=0)
        acc_ref[:, cols] = a * acc_ref[:, cols] + jnp.dot(v_aug, p.astype(BF16),
                                                          preferred_element_type=F32)
        m_ref[:, cols] = m_new

    m_ref[...] = jnp.full_like(m_ref, -jnp.inf)
    acc_ref[...] = jnp.zeros_like(acc_ref)

    @pl.when(qi == 0)
    def _():
        score(0, buf_a, diagonal=True)

    @pl.when(qi > 0)
    def _():
        score(0, buf_a)

        def pair(g, c):
            score(2 * g + 1, buf_b)
            absorb(2 * g, buf_a)
            score(2 * g + 2, buf_a)
            absorb(2 * g + 1, buf_b)
            return c

        lax.fori_loop(0, qi - 1, pair, 0)
        score(2 * qi - 1, buf_b)
        absorb(2 * qi - 2, buf_a)
        score(2 * qi, buf_a, diagonal=True)
        absorb(2 * qi - 1, buf_b)

    score(2 * qi + 1, buf_b, diagonal=True, late=True)
    absorb(2 * qi, buf_a)
    absorb(2 * qi + 1, buf_b, late=True)
    o_ref[...] = jnp.transpose(acc_ref[0:dh, :] / acc_ref[dh:dh + 1, :]).astype(o_ref.dtype)


def _fox_attention(qk, vt, c_heads, tq, batch, seq, heads, col_q, col_k, to_cast):
    n = qk.shape[0]
    dh = FOX_HEAD_DIM
    tk = tq // 2
    nq, nk = seq // tq, seq // tk
    assert nk <= LANES
    c_qrows = c_heads.reshape(batch, heads, nq, 1, tq)
    c_krows = c_heads.reshape(batch, heads, nk, tk)
    steps = batch * heads * nq

    def slab(a):
        rb = _cast_rows(a.shape[0], steps)
        last = a.shape[0] // rb - 1
        return pl.BlockSpec((rb, a.shape[1]),
                            lambda b, h, i: (jnp.minimum((b * heads + h) * nq + i, last), 0))
    outs = pl.pallas_call(
        functools.partial(_fox_kernel, scale=dh ** -0.5, n_cast=len(to_cast)),
        grid=(batch, heads, nq),
        in_specs=[pl.BlockSpec((tq, dh), lambda b, h, i: (b * nq + i, col_q + h)),
                  pl.BlockSpec((seq, dh), lambda b, h, i: (b, col_k + h)),
                  pl.BlockSpec((dh, seq), lambda b, h, i: (h, b)),
                  pl.BlockSpec((None, None, None, 1, tq), lambda b, h, i: (b, h, i, 0, 0)),
                  pl.BlockSpec((None, None, nk, tk), lambda b, h, i: (b, h, 0, 0))]
                 + [slab(a) for a in to_cast],
        out_specs=[pl.BlockSpec((tq, dh), lambda b, h, i: (b * nq + i, h))] + [slab(a) for a in to_cast],
        out_shape=[jax.ShapeDtypeStruct((n, heads * dh), BF16)]
                  + [jax.ShapeDtypeStruct(a.shape, BF16) for a in to_cast],
        scratch_shapes=[pltpu.VMEM((tq // 2, tq), F32), pltpu.VMEM((tq // 2, tq), F32),
                        pltpu.VMEM((1, tq), F32), pltpu.VMEM((1, tq), F32),
                        pltpu.VMEM((1, tq), F32), pltpu.VMEM((dh + BF16_SUBLANES, tq), F32),
                        pltpu.VMEM((nk, tk, 1), F32)],
        compiler_params=_params("arbitrary", "arbitrary", "arbitrary"),
        name="fox_attention",
    )(qk, qk, vt, c_qrows, c_krows, *to_cast)
    return outs[0], outs[1:]


def _cast_rows(rows, steps):
    rb = max(BF16_SUBLANES, -(-rows // steps))
    return rb if rows % rb == 0 and rb % BF16_SUBLANES == 0 else None


def _cast_plan(arrays, steps):
    views = [a.reshape(-1, a.shape[-1]) for a in arrays]
    return views if all(_cast_rows(v.shape[0], steps) for v in views) else None


def _mlstm_kernel(q_ref, qh_ref, k_ref, kh_ref, cwq_ref, cwk_ref, cbq_ref, cbk_ref, v_ref, gate_ref,
                  brow_ref, irow_ref, mo_ref, g_ref, o_ref, c_ref, n_ref, m_ref, *, lane_i, lane_b):
    L = q_ref.shape[0]
    h = pl.program_id(1)
    first_chunk = pl.program_id(2) == 0

    @pl.when(first_chunk)
    def _():
        c_ref[...] = jnp.zeros_like(c_ref)
        n_ref[...] = jnp.zeros_like(n_ref)
        m_ref[...] = jnp.zeros_like(m_ref)

    def conv_silu(x_ref, halo_ref, w_ref, b_ref):
        kw, halo = w_ref.shape[0], halo_ref.shape[0]
        ext = jnp.concatenate([jnp.where(first_chunk, 0.0, halo_ref[...]), x_ref[...]], axis=0)
        w = w_ref[...]
        acc = None
        for j in range(kw):
            off = halo - (kw - 1) + j
            term = w[j:j + 1, :] * ext[off:off + L, :]
            acc = term if acc is None else acc + term
        acc = acc + b_ref[...]
        return acc * jax.nn.sigmoid(acc)

    q32 = conv_silu(q_ref, qh_ref, cwq_ref, cbq_ref)
    k32 = conv_silu(k_ref, kh_ref, cwk_ref, cbk_ref) * (ML_QK_DIM ** -0.5)

    gates = gate_ref[...]
    lane = lax.broadcasted_iota(jnp.int32, gates.shape, 1)
    pick = lambda col: jnp.sum(jnp.where(lane == col + h, gates, 0.0), axis=-1, keepdims=True)
    icol = pick(lane_i)
    bcol = pick(lane_b)
    brow = brow_ref[...]
    irow = irow_ref[...]
    m_prev = m_ref[0:1, 0:1]

    q = q32.astype(BF16)
    v = v_ref[...]

    tri = lax.broadcasted_iota(jnp.int32, (L, L), 0) >= lax.broadcasted_iota(jnp.int32, (L, L), 1)
    d = jnp.where(tri, bcol - brow + irow, -jnp.inf)
    inter = bcol + m_prev
    m = jnp.maximum(inter, jnp.max(d, axis=-1, keepdims=True))
    w_intra = jnp.exp(d - m)
    w_inter = jnp.exp(inter - m)
    s = lax.dot_general(q, k32.astype(BF16), (((1,), (1,)), ((), ())), preferred_element_type=F32)
    sc = s * w_intra
    num = (jnp.dot(sc.astype(BF16), v, preferred_element_type=F32)
           + w_inter * jnp.dot(q, c_ref[...].astype(BF16), preferred_element_type=F32))
    qn = jnp.sum(q32 * n_ref[...], axis=-1, keepdims=True)
    den = jnp.sum(sc, axis=-1, keepdims=True) + w_inter * qn
    hid = num / jnp.maximum(jnp.abs(den), jnp.exp(-m))

    b_last = bcol[L - 1:L, :]
    gcol = b_last - bcol + icol
    m_new = jnp.maximum(b_last + m_prev, jnp.max(gcol, axis=0, keepdims=True))
    w_k = jnp.exp(gcol - m_new)
    decay = jnp.exp(b_last + m_prev - m_new)
    kw = k32 * w_k
    c_ref[...] = decay * c_ref[...] + lax.dot_general(
        kw.astype(BF16), v, (((0,), (0,)), ((), ())), preferred_element_type=F32)
    n_ref[...] = decay * n_ref[...] + jnp.sum(kw, axis=0, keepdims=True)
    m_ref[...] = jnp.broadcast_to(m_new, m_ref.shape)

    mu = jnp.mean(hid, axis=-1, keepdims=True)
    hc = hid - mu
    var = jnp.mean(hc * hc, axis=-1, keepdims=True)
    hn = hc * lax.rsqrt(var + LN_EPS)
    o_ref[...] = (hn * g_ref[...] * jax.nn.sigmoid(mo_ref[...])).astype(o_ref.dtype)


def _mlstm(qk, conv_w, conv_b, qkv, proj, gates, brow, irow, norm_g, batch, seq, chunk, col_v, col_mo,
           lane_i, lane_b):
    n = qk.shape[0]
    dk, dv = ML_QK_DIM, ML_V_DIM
    nc = seq // chunk
    kw = conv_w.shape[0]
    halo = F32_SUBLANES
    assert kw - 1 <= halo and chunk % halo == 0
    row = lambda b, h, c: b * nc + c
    before = lambda b, h, c: jnp.maximum(row(b, h, c) * (chunk // halo) - 1, 0)
    vec = pl.BlockSpec((None, None, None, 1, chunk), lambda b, h, c: (b, h, c, 0, 0))
    conv_b = conv_b.reshape(1, -1)
    return pl.pallas_call(
        functools.partial(_mlstm_kernel, lane_i=lane_i, lane_b=lane_b),
        grid=(batch, ML_HEADS, nc),
        in_specs=[pl.BlockSpec((chunk, dk), lambda b, h, c: (row(b, h, c), h)),
                  pl.BlockSpec((halo, dk), lambda b, h, c: (before(b, h, c), h)),
                  pl.BlockSpec((chunk, dk), lambda b, h, c: (row(b, h, c), ML_HEADS + h)),
                  pl.BlockSpec((halo, dk), lambda b, h, c: (before(b, h, c), ML_HEADS + h)),
                  pl.BlockSpec((kw, dk), lambda b, h, c: (0, h)),
                  pl.BlockSpec((kw, dk), lambda b, h, c: (0, ML_HEADS + h)),
                  pl.BlockSpec((1, dk), lambda b, h, c: (0, h)),
                  pl.BlockSpec((1, dk), lambda b, h, c: (0, ML_HEADS + h)),
                  pl.BlockSpec((chunk, dv), lambda b, h, c: (row(b, h, c), col_v + h)),
                  pl.BlockSpec((chunk, gates.shape[1]), lambda b, h, c: (row(b, h, c), 0)),
                  vec, vec,
                  pl.BlockSpec((chunk, dv), lambda b, h, c: (row(b, h, c), col_mo + h)),
                  pl.BlockSpec((1, dv), lambda b, h, c: (0, h))],
        out_specs=pl.BlockSpec((chunk, dv), lambda b, h, c: (row(b, h, c), h)),
        out_shape=jax.ShapeDtypeStruct((n, ML_HEADS * dv), BF16),
        scratch_shapes=[pltpu.VMEM((dk, dv), F32), pltpu.VMEM((1, dk), F32), pltpu.VMEM((8, 128), F32)],
        compiler_params=_params("parallel", "parallel", "arbitrary"),
        name="mlstm",
    )(qk, qk, qk, qk, conv_w, conv_w, conv_b, conv_b, qkv, gates, brow, irow, proj,
      norm_g.reshape(1, ML_HEADS * dv))


def _merge_kernel(yf_ref, ym_ref, wf_ref, wm_ref, ga_ref, gb_ref, o_ref):
    a = jnp.dot(yf_ref[...], wf_ref[...], preferred_element_type=F32)
    b = jnp.dot(ym_ref[...], wm_ref[...], preferred_element_type=F32)
    o_ref[...] = (jax.nn.sigmoid(ga_ref[...]) * a + jax.nn.sigmoid(gb_ref[...]) * b).astype(o_ref.dtype)


def _merge(y_fox, y_ml, w_fox, w_ml, proj, col_ga, col_gb):
    n, kf = y_fox.shape
    km = y_ml.shape[1]
    d = w_fox.shape[1]
    bm, bn = _tile(n, MERGE_ROW_BLOCK), _tile(d, MATMUL_BLOCK)
    ga0, gb0 = col_ga // bn, col_gb // bn
    return pl.pallas_call(
        _merge_kernel,
        grid=(n // bm, d // bn),
        in_specs=[pl.BlockSpec((bm, kf), lambda i, j: (i, 0)),
                  pl.BlockSpec((bm, km), lambda i, j: (i, 0)),
                  pl.BlockSpec((kf, bn), lambda i, j: (0, j)),
                  pl.BlockSpec((km, bn), lambda i, j: (0, j)),
                  pl.BlockSpec((bm, bn), lambda i, j: (i, ga0 + j)),
                  pl.BlockSpec((bm, bn), lambda i, j: (i, gb0 + j))],
        out_specs=pl.BlockSpec((bm, bn), lambda i, j: (i, j)),
        out_shape=jax.ShapeDtypeStruct((n, d), BF16),
        compiler_params=_params("parallel", "parallel"),
        name="merge_proj",
    )(y_fox, y_ml, w_fox, w_ml, proj, proj)


def _split_bf16(v):
    hi = v.astype(BF16)
    return hi, (v - hi.astype(F32)).astype(BF16)


def _route_rows(x, w_ref, b_ref, oi_ref, of_ref, cnt_ref, carry_ref, whi_ref, wlo_ref,
                *, n_groups, per_group):
    @pl.when(pl.program_id(0) == 0)
    def _():
        carry_ref[...] = jnp.zeros_like(carry_ref)
        whi_ref[...], wlo_ref[...] = _split_bf16(w_ref[...])

    tr = x.shape[0]
    n_exp = n_groups * per_group
    x_hi, x_lo = _split_bf16(x)
    dot = lambda a, b: jnp.dot(a, b, preferred_element_type=F32)
    logits = (dot(x_hi, whi_ref[...]) + dot(x_lo, whi_ref[...]) + dot(x_hi, wlo_ref[...])) + b_ref[...]
    lane = lax.broadcasted_iota(jnp.int32, logits.shape, 1)
    big = jnp.int32(logits.shape[1])
    first_lane = lambda cond: jnp.min(jnp.where(cond, lane, big), axis=-1, keepdims=True)

    is_group = (lane >= n_exp) & (lane < n_exp + n_groups)
    gl = jnp.where(is_group, logits, -jnp.inf)
    g_max = jnp.max(gl, axis=-1, keepdims=True)
    g_sel = first_lane(gl == g_max) - n_exp
    p_g_sel = 1.0 / jnp.sum(jnp.exp(gl - g_max), axis=-1, keepdims=True)

    in_group = (lane >= g_sel * per_group) & (lane < (g_sel + 1) * per_group)
    el = jnp.where(in_group, logits, -jnp.inf)
    e_max = jnp.max(el, axis=-1, keepdims=True)
    ee = jnp.exp(el - e_max)
    pe = jnp.where(in_group, ee / jnp.sum(ee, axis=-1, keepdims=True), -1.0)
    p1 = jnp.max(pe, axis=-1, keepdims=True)
    e1 = first_lane(pe == p1)
    pe2 = jnp.where(lane == e1, -1.0, pe)
    p2 = jnp.max(pe2, axis=-1, keepdims=True)
    e2 = first_lane(pe2 == p2)
    p_sum = p1 + p2
    w1 = p_g_sel * p1 / p_sum
    w2 = p_g_sel * p2 / p_sum

    onehot = (lane == e1) | (lane == e2)
    strict = (lax.broadcasted_iota(jnp.int32, (tr, tr), 0)
              > lax.broadcasted_iota(jnp.int32, (tr, tr), 1)).astype(BF16)
    before = jnp.dot(strict, onehot.astype(BF16), preferred_element_type=F32) + carry_ref[...]
    r1 = jnp.sum(jnp.where(lane == e1, before, 0.0), axis=-1, keepdims=True).astype(jnp.int32)
    r2 = jnp.sum(jnp.where(lane == e2, before, 0.0), axis=-1, keepdims=True).astype(jnp.int32)
    carry_ref[...] += jnp.sum(onehot.astype(F32), axis=0, keepdims=True)
    cnt_ref[...] = carry_ref[...].astype(jnp.int32)

    oi_ref[...] = jnp.where(lane == 0, e1, jnp.where(lane == 1, e2, jnp.where(lane == 2, r1, r2)))
    of_ref[...] = jnp.where(lane == 0, w1, w2)


def _expert_kernel(blk_e_ref, n_used_ref, tok_ref, tok_next_ref, x_hbm, wt_ref, wg_ref, wu_ref, wd_ref,
                   o_ref, xbuf, sem):
    b = pl.program_id(0)
    tb, half = xbuf.shape[1:]
    n_used = n_used_ref[0]
    slot = b % 2

    def row_copy(ids_ref, r, s):
        return pltpu.make_async_copy(x_hbm.at[pl.ds(ids_ref[r], 1), :],
                                     xbuf.at[s, pl.ds(r, 1), :], sem.at[s])

    def wait_rows():
        pltpu.make_async_copy(x_hbm.at[pl.ds(0, tb), :], xbuf.at[slot], sem.at[slot]).wait()

    def compute():
        lo, hi = _unpack_bf16_pairs(xbuf[slot])
        lo, hi = lo.astype(BF16), hi.astype(BF16)
        proj = lambda w_ref: (jnp.dot(lo, w_ref[0:half, :], preferred_element_type=F32)
                              + jnp.dot(hi, w_ref[half:2 * half, :], preferred_element_type=F32))
        gate, up = proj(wg_ref), proj(wu_ref)
        act = (gate * jax.nn.sigmoid(gate) * up).astype(BF16)
        out = jnp.dot(act, wd_ref[...], preferred_element_type=F32) * wt_ref[...]
        o_ref[...] = _pack_bf16_pairs(out)

    @pl.when((b == 0) & (n_used > 0))
    def _():
        def start(r, c):
            row_copy(tok_ref, r, slot).start()
            return c
        lax.fori_loop(0, tb, start, 0)

    @pl.when(b + 1 < n_used)
    def _():
        wait_rows()
        for r in range(tb):
            row_copy(tok_next_ref, r, 1 - slot).start()
        compute()

    @pl.when(b + 1 == n_used)
    def _():
        wait_rows()
        compute()

    @pl.when(b >= n_used)
    def _():
        o_ref[...] = jnp.zeros_like(o_ref)


def _experts(xp, slot_tok, slot_w, block_e, n_used, w_gate, w_up, w_down, tb):
    n, half = xp.shape
    _, d, f = w_gate.shape
    assert d == 2 * half
    n_blocks = block_e.shape[0]
    smem_tb = lambda shift: pl.BlockSpec(
        (tb,), lambda b, *_: (jnp.minimum(b + shift, n_blocks - 1),), memory_space=pltpu.SMEM)
    grid_spec = pltpu.PrefetchScalarGridSpec(
        num_scalar_prefetch=2,
        grid=(n_blocks,),
        in_specs=[smem_tb(0), smem_tb(1),
                  pl.BlockSpec(memory_space=pl.ANY),
                  pl.BlockSpec((tb, 1), lambda b, *_: (b, 0)),
                  pl.BlockSpec((None, d, f), lambda b, be, nu: (be[b], 0, 0)),
                  pl.BlockSpec((None, d, f), lambda b, be, nu: (be[b], 0, 0)),
                  pl.BlockSpec((None, f, d), lambda b, be, nu: (be[b], 0, 0))],
        out_specs=pl.BlockSpec((tb, half), lambda b, *_: (b, 0)),
        scratch_shapes=[pltpu.VMEM((2, tb, half), jnp.int32), pltpu.SemaphoreType.DMA((2,))],
    )
    return pl.pallas_call(
        _expert_kernel,
        grid_spec=grid_spec,
        out_shape=jax.ShapeDtypeStruct((n_blocks * tb, half), jnp.int32),
        compiler_params=_params("arbitrary"),
        name="experts",
    )(block_e, n_used, slot_tok, slot_tok, xp, slot_w.reshape(-1, 1), w_gate, w_up, w_down)


def _combine_kernel(dest_ref, dest_next_ref, y_hbm, h_ref, g_ref, b_ref, o_ref, buf, sem, *, alpha):
    i = pl.program_id(0)
    last = pl.num_programs(0) - 1
    tm = h_ref.shape[0]
    slot = i % 2

    def row_copy(ids_ref, r, k, s):
        return pltpu.make_async_copy(y_hbm.at[pl.ds(ids_ref[r * TOP_K + k], 1), :],
                                     buf.at[s, k, pl.ds(r, 1), :], sem.at[s])

    def compute():
        y = None
        for k in range(TOP_K):
            y_k = jnp.concatenate(_unpack_bf16_pairs(buf[slot, k]), axis=1)
            y = y_k if y is None else y + y_k
        o_ref[...] = _layer_norm(alpha * h_ref[...] + y, g_ref[...], b_ref[...])

    @pl.when(i == 0)
    def _():
        def start(r, c):
            for k in range(TOP_K):
                row_copy(dest_ref, r, k, slot).start()
            return c
        lax.fori_loop(0, tm, start, 0)

    for k in range(TOP_K):
        pltpu.make_async_copy(y_hbm.at[pl.ds(0, tm), :], buf.at[slot, k], sem.at[slot]).wait()

    @pl.when(i < last)
    def _():
        for r in range(tm):
            for k in range(TOP_K):
                row_copy(dest_next_ref, r, k, 1 - slot).start()
        compute()

    @pl.when(i == last)
    def _():
        compute()


def _combine_ln(h, y_slots, dest, g, b, alpha):
    n, d = h.shape
    tm = _tile(n, ROW_TILE)
    tiles = n // tm
    return pl.pallas_call(
        functools.partial(_combine_kernel, alpha=alpha),
        grid=(tiles,),
        in_specs=[pl.BlockSpec((tm * TOP_K,), lambda i: (i,), memory_space=pltpu.SMEM),
                  pl.BlockSpec((tm * TOP_K,), lambda i: (jnp.minimum(i + 1, tiles - 1),),
                               memory_space=pltpu.SMEM),
                  pl.BlockSpec(memory_space=pl.ANY),
                  pl.BlockSpec((tm, d), lambda i: (i, 0)),
                  pl.BlockSpec((1, d), lambda i: (0, 0)),
                  pl.BlockSpec((1, d), lambda i: (0, 0))],
        out_specs=pl.BlockSpec((tm, d), lambda i: (i, 0)),
        out_shape=jax.ShapeDtypeStruct((n, d), F32),
        scratch_shapes=[pltpu.VMEM((2, TOP_K, tm, d // 2), jnp.int32), pltpu.SemaphoreType.DMA((2,))],
        compiler_params=_params("arbitrary"),
        name="combine_ln",
    )(dest, dest, y_slots, h, g.reshape(1, d), b.reshape(1, d))


EXPERT_SLOT_BLOCK = 256
ML_CHUNK = 512
FOX_KEY_BLOCK = 512


def _mixer(h16, batch, seq, w_in, b_fox_f, b_ml_i, b_ml_f, conv_w, conv_b, ml_norm_g,
           w_proj_fox, w_proj_ml, w_out, expert_w):
    n, d = h16.shape
    fox_w = d // 2
    fox_heads = fox_w // FOX_HEAD_DIM
    qk_w = ML_HEADS * ML_QK_DIM
    v_w = ML_HEADS * ML_V_DIM
    widths = (fox_w, fox_w, fox_w, fox_heads, qk_w, qk_w, v_w, ML_HEADS, ML_HEADS, v_w, d, d)
    offs = [0]
    for w in widths:
        offs.append(offs[-1] + w)
    n_gate = fox_heads + 2 * ML_HEADS
    gate_bias = jnp.pad(jnp.concatenate([b_fox_f, b_ml_i, b_ml_f]), (0, GATE_LANES - n_gate)).reshape(1, -1)

    wt = jnp.transpose(w_in)
    proj_fox = _in_proj(h16, wt, offs[0], 2 * fox_w, BF16, "in_proj_fox")
    v_t = _in_proj(h16, wt, offs[2], fox_w, BF16, "in_proj_fv", features_major=True)
    proj_qk = _in_proj(h16, wt, offs[4], 2 * qk_w, F32, "in_proj_qk")
    proj_mv = _in_proj(h16, wt, offs[6], v_w, BF16, "in_proj_mv")
    proj_og = _in_proj(h16, wt, offs[9], v_w + 2 * d, F32, "in_proj_og")
    gate_pre = _gate_proj(h16, wt, offs[3], offs[7], fox_heads)

    chunk = _tile(seq, ML_CHUNK)
    gates = _gates(gate_pre, gate_bias, batch, chunk, fox_heads)

    g3 = gates.reshape(batch, seq, GATE_LANES)
    tq = 2 * _tile(seq // 2, FOX_KEY_BLOCK)
    c_heads = jnp.transpose(g3[:, :, :fox_heads], (0, 2, 1))
    lane_i, lane_b = fox_heads, fox_heads + ML_HEADS
    rows = lambda lo: jnp.transpose(g3[:, :, lo:lo + ML_HEADS], (0, 2, 1)).reshape(
        batch, ML_HEADS, seq // chunk, 1, chunk)
    irow, brow = rows(lane_i), rows(lane_b)

    later_w = [w_proj_fox, w_proj_ml, w_out] + list(expert_w)
    views = _cast_plan(later_w, batch * fox_heads * (seq // tq))
    y_fox, cast = _fox_attention(proj_fox, v_t, c_heads, tq, batch, seq, fox_heads,
                                 col_q=0, col_k=fox_heads, to_cast=views or [])
    if views:
        later_w16 = [c.reshape(a.shape) for c, a in zip(cast, later_w)]
    else:
        later_w16 = [a.astype(BF16) for a in later_w]
    w_proj_fox16, w_proj_ml16, w_out16 = later_w16[:3]
    y_ml = _mlstm(proj_qk, conv_w, conv_b, proj_mv, proj_og, gates, brow, irow, ml_norm_g, batch, seq,
                  chunk, col_v=0, col_mo=0, lane_i=lane_i, lane_b=lane_b)
    merged = _merge(y_fox, y_ml, w_proj_fox16, w_proj_ml16, proj_og, col_ga=v_w, col_gb=v_w + d)
    return _matmul(merged, w_out16, F32, "out_proj"), later_w16[3:]


def _route_params(w_group, b_group, w_router, b_router):
    n_groups, n_exp = w_group.shape[1], w_router.shape[1]
    assert n_exp + n_groups <= GATE_LANES
    pad = GATE_LANES - n_exp - n_groups
    w_route = jnp.pad(jnp.concatenate([w_router, w_group], axis=1), ((0, 0), (0, pad)))
    b_route = jnp.pad(jnp.concatenate([b_router, b_group]), (0, pad)).reshape(1, GATE_LANES)
    return w_route, b_route, n_groups, n_exp // n_groups


def _moe(h, h_packed, routing, n_exp, w_gate, w_up, w_down, ln_g, ln_b, alpha):
    n, d = h.shape
    oi, of, cnt = routing
    e_idx, rank, gate_w = oi[:, 0:TOP_K], oi[:, TOP_K:2 * TOP_K], of[:, 0:TOP_K]
    counts = cnt[0, :n_exp]

    tb = EXPERT_SLOT_BLOCK
    n_assign = n * TOP_K
    n_blocks = (n_assign + n_exp * (tb - 1) + tb - 1) // tb
    padded = (counts + tb - 1) // tb * tb
    pad_ends = jnp.cumsum(padded)
    onehot = (e_idx[..., None] == jnp.arange(n_exp, dtype=jnp.int32)).astype(F32)
    seg_start = jnp.einsum("nke,e->nk", onehot, (pad_ends - padded).astype(F32),
                           precision=lax.Precision.HIGHEST).astype(jnp.int32)
    dest = (seg_start + rank).reshape(-1)
    tok = jnp.repeat(jnp.arange(n, dtype=jnp.int32), TOP_K)
    pairs = jnp.stack([tok, lax.bitcast_convert_type(gate_w.reshape(-1), jnp.int32)], axis=1)
    slots = jnp.zeros((n_blocks * tb, 2), jnp.int32).at[dest].set(pairs)
    slot_tok = slots[:, 0]
    slot_w = lax.bitcast_convert_type(slots[:, 1], F32)
    n_used = pad_ends[-1] // tb
    blk = jnp.arange(n_blocks, dtype=jnp.int32)
    block_e = jnp.minimum(jnp.searchsorted(pad_ends, blk * tb, side="right"), n_exp - 1).astype(jnp.int32)
    block_e = jnp.where(blk < n_used, block_e, block_e[jnp.maximum(n_used - 1, 0)])

    y_slots = _experts(h_packed, slot_tok, slot_w, block_e, n_used.reshape(1).astype(jnp.int32),
                       w_gate, w_up, w_down, tb)
    return _combine_ln(h, y_slots, dest.astype(jnp.int32), ln_g, ln_b, alpha)


def kernel(x, ln_in_g, ln_in_b, w_in, b_fox_f, b_ml_i, b_ml_f, conv_w, conv_b, ml_norm_g, w_proj_fox, w_proj_ml, w_out, ln_mix_g, ln_mix_b, w_group, b_group, w_router, b_router, w_gate, w_up, w_down, ln_moe_g, ln_moe_b):
    batch, seq, d = x.shape
    depth = w_in.shape[0]
    alpha = (2 * depth) ** 0.25
    h32 = x.reshape(batch * seq, d)
    h16 = _ln_in(h32, ln_in_g, ln_in_b)
    for l in range(depth):
        mix, expert_w16 = _mixer(h16, batch, seq, w_in[l], b_fox_f[l], b_ml_i[l], b_ml_f[l],
                                 conv_w[l], conv_b[l], ml_norm_g[l], w_proj_fox[l], w_proj_ml[l],
                                 w_out[l], [w_gate[l], w_up[l], w_down[l]])
        w_route, b_route, n_groups, per_group = _route_params(w_group[l], b_group[l],
                                                              w_router[l], b_router[l])
        h32, h_packed, *routing = _ln_route(h32, mix, ln_mix_g[l], ln_mix_b[l], alpha,
                                            w_route, b_route, n_groups, per_group,
                                            pre_norm=(ln_in_g, ln_in_b) if l == 0 else None)
        h32 = _moe(h32, h_packed, routing, n_groups * per_group, *expert_w16,
                   ln_moe_g[l], ln_moe_b[l], alpha)
        if l + 1 < depth:
            h16 = h32.astype(BF16)
    return h32.reshape(batch, seq, d)
```
